```python
import math
import jax
import jax.numpy as jnp
from jax import lax
import numpy as np

D_MODEL = 2048
BATCH = 4
SEQ = 2048
DEPTH = 4
DEC_BATCH = 8
DEC_SEQ = 8
PAST_LEN = 16384
PAGE_SIZE = 128

GROUP_W = D_MODEL // 4
DH_A = 64
H_A = GROUP_W // DH_A
H_IDX = 16
D_IDX = 64
TOPK_MAX = 256
S5_CH = GROUP_W
S5_GROUP = 16
S5_G = S5_CH // S5_GROUP
S5_P = 64
DH_C = 64
H_C = GROUP_W // (2 * DH_C)
DH_D = 64
H_D = GROUP_W // DH_D
N_MEM = 256
H_X = 4
DH_X = 128
D_FF = 11 * D_MODEL // 4
CONV_W = 3
NUM_BUCKETS = 32
MAX_DISTANCE = 128
Q_BLOCK = 128
EPS = 1e-6
NEG = -1e30

IN_SIZES = (H_A * DH_A, H_A * DH_A, H_A * DH_A, H_IDX * D_IDX, D_IDX, H_IDX,
            S5_CH,
            H_C * 2 * DH_C, H_C * 2 * DH_C, H_C * 2 * DH_C,
            H_D * DH_D, H_D * DH_D, H_D * DH_D, H_D)
D_IN = sum(IN_SIZES)

kernel_name = 'hymba_dsa_s5_diff_fox_step'


def _rmsnorm(x, g):
    xf = x.astype(jnp.float32)
    y = xf * lax.rsqrt(jnp.mean(xf * xf, axis=-1, keepdims=True) + EPS)
    return (y * g.astype(jnp.float32)).astype(x.dtype)


def _split_in(z):
    offsets = [int(v) for v in np.cumsum(IN_SIZES)[:-1]]
    return jnp.split(z, offsets, axis=-1)


def _gather_pages(pool, page_table):
    g = pool[page_table]
    return g.reshape(g.shape[0], g.shape[1] * g.shape[2], *g.shape[3:])


def _t5_bucket(dist):
    n = jnp.maximum(dist, 0)
    exact = NUM_BUCKETS // 2
    nf = jnp.maximum(n, 1).astype(jnp.float32)
    large = exact + (jnp.log(nf / exact) / math.log(MAX_DISTANCE / exact)
                     * (NUM_BUCKETS - exact)).astype(jnp.int32)
    return jnp.where(n < exact, n, jnp.minimum(large, NUM_BUCKETS - 1))


def _sweep_query_blocks(fn, qpos, *q_arrays):
    t = qpos.shape[0]
    qb = min(Q_BLOCK, t)
    nb = t // qb
    blocks = tuple(a.reshape(a.shape[0], nb, qb, *a.shape[2:]).swapaxes(0, 1) for a in q_arrays)
    out = lax.map(lambda xs: fn(xs[0], *xs[1]), (qpos.reshape(nb, qb), blocks))
    return out.swapaxes(0, 1).reshape(out.shape[1], t, *out.shape[3:])


def _dsa_attention(q, qi, wi, qpos, ki_all, fetch_kv, bias_tab):
    n_keys = ki_all.shape[1]
    topk = min(TOPK_MAX, n_keys // 4)
    kpos = jnp.arange(n_keys)

    def block(pos, qblk, qiblk, wiblk):
        dots = jnp.einsum('bqhd,bsd->bqhs', qiblk, ki_all).astype(jnp.float32) * (D_IDX ** -0.5)
        score = jnp.einsum('bqh,bqhs->bqs', wiblk.astype(jnp.float32) * (H_IDX ** -0.5), jax.nn.relu(dots))
        score = jnp.where(kpos[None, None, :] <= pos[None, :, None], score, -jnp.inf)
        _, sel = lax.top_k(score, topk)
        kv = fetch_kv(sel)
        logits = jnp.einsum('bqhd,bqkhd->bqhk', qblk, kv[:, :, :, 0]).astype(jnp.float32) * (DH_A ** -0.5)
        bias = bias_tab[_t5_bucket(pos[None, :, None] - sel)].astype(jnp.float32)
        logits = logits + jnp.swapaxes(bias, 2, 3)
        valid = (sel <= pos[None, :, None])[:, :, None, :]
        p = jax.nn.softmax(jnp.where(valid, logits, NEG), axis=-1)
        return jnp.einsum('bqhk,bqkhd->bqhd', p.astype(kv.dtype), kv[:, :, :, 1])

    return _sweep_query_blocks(block, qpos, q, qi, wi)


def _diff_attention(q, k_all, v_all, qpos, lam, bias_tab):
    kpos = jnp.arange(k_all.shape[1])

    def block(pos, qblk):
        logits = jnp.einsum('bqhjd,bshjd->bjhqs', qblk, k_all).astype(jnp.float32) * (DH_C ** -0.5)
        bias = bias_tab[_t5_bucket(pos[:, None] - kpos[None, :])].astype(jnp.float32)
        logits = logits + jnp.transpose(bias, (2, 0, 1))[None, None]
        causal = (kpos[None, :] <= pos[:, None])[None, None, None]
        p = jax.nn.softmax(jnp.where(causal, logits, NEG), axis=-1)
        w = p[:, 0] - lam * p[:, 1]
        return jnp.einsum('bhqs,bshe->bqhe', w.astype(v_all.dtype), v_all)

    return _sweep_query_blocks(block, qpos, q)


def _forgetting_attention(q, k_all, v_all, cq, ck, qpos):
    kpos = jnp.arange(k_all.shape[1])
    ck_t = jnp.swapaxes(ck, 1, 2)[:, :, None, :]

    def block(pos, qblk, cqblk):
        logits = jnp.einsum('bqhd,bshd->bhqs', qblk, k_all).astype(jnp.float32) * (DH_D ** -0.5)
        logits = logits + jnp.swapaxes(cqblk, 1, 2)[..., None] - ck_t
        causal = (kpos[None, :] <= pos[:, None])[None, None]
        p = jax.nn.softmax(jnp.where(causal, logits, NEG), axis=-1)
        return jnp.einsum('bhqs,bshd->bqhd', p.astype(v_all.dtype), v_all)

    return _sweep_query_blocks(block, qpos, q, cq)


def _linear_combine(left, right):
    a_l, b_l = left
    a_r, b_r = right
    return a_l * a_r, a_r * b_l + b_r


def _s5_mixer(u, lw, h0):
    f32 = jnp.float32
    bsz, t, _ = u.shape
    lam = lax.complex(lw['s5_a_re'].astype(f32), lw['s5_a_im'].astype(f32))
    dt = jnp.exp(lw['s5_log_dt'].astype(f32))[:, None]
    a_bar = jnp.exp(lam * dt)
    b_mat = lax.complex(lw['s5_b_re'].astype(f32), lw['s5_b_im'].astype(f32))
    b_bar = ((a_bar - 1.0) / lam)[..., None] * b_mat
    c_mat = lax.complex(lw['s5_c_re'].astype(f32), lw['s5_c_im'].astype(f32))
    uf = u.astype(f32)
    bu = jnp.einsum('gpc,btgc->btgp', b_bar,
                    uf.reshape(bsz, t, S5_G, S5_GROUP).astype(jnp.complex64))
    if h0 is not None:
        h_prev = lax.complex(h0[0].astype(f32), h0[1].astype(f32))
        bu = bu.at[:, 0].add(a_bar * h_prev)
    _, h = lax.associative_scan(_linear_combine, (jnp.broadcast_to(a_bar, bu.shape), bu), axis=1)
    y = jnp.einsum('gcp,btgp->btgc', c_mat, h).real.reshape(bsz, t, S5_CH) + lw['s5_d'].astype(f32) * uf
    y = jax.nn.gelu(y)
    y = y * jax.nn.sigmoid(y @ lw['s5_w_glu'].astype(f32))
    return y.astype(u.dtype), h[:, -1]


def _conv_ffn(hn, lw, buf):
    t = hn.shape[1]
    g = hn @ lw['w_gate']
    gp = jnp.concatenate([buf.astype(g.dtype), g], axis=1)
    gc = lw['conv_b'] + lw['conv_w'][0] * gp[:, 0:t]
    for j in range(1, CONV_W):
        gc = gc + lw['conv_w'][j] * gp[:, j:j + t]
    h = jax.nn.gelu(gc) * (hn @ lw['w_up'])
    return h @ lw['w_down'], gp[:, t:]


def _layer(x, mem_kv, lw, rel_bias, layer_idx, past):
    f32 = jnp.float32
    bsz, t, _ = x.shape
    start = 0 if past is None else PAST_LEN
    qpos = start + jnp.arange(t)
    pt = None if past is None else past['page_table']

    hn = _rmsnorm(x, lw['g_mix_pre'])
    (a_q, a_k, a_v, a_qi, a_ki, a_wi, b_u, c_q, c_k, c_v,
     d_q, d_k, d_v, d_f) = _split_in(hn @ lw['w_in'])

    a_q = a_q.reshape(bsz, t, H_A, DH_A)
    a_kv = jnp.stack([a_k, a_v], axis=2).reshape(bsz, t, 2, H_A, DH_A)
    a_qi = a_qi.reshape(bsz, t, H_IDX, D_IDX)
    bidx = jnp.arange(bsz)[:, None, None]
    if past is None:
        ki_all = a_ki

        def fetch(sel):
            return a_kv[bidx, sel]
    else:
        ki_all = jnp.concatenate([_gather_pages(past['a_idx_k'], pt), a_ki], axis=1)
        pool = past['a_kv']

        def fetch(sel):
            sp = jnp.minimum(sel, PAST_LEN - 1)
            rows_past = pool[pt[bidx, sp // PAGE_SIZE], sp % PAGE_SIZE]
            rows_new = a_kv[bidx, jnp.maximum(sel - PAST_LEN, 0)]
            return jnp.where((sel < PAST_LEN)[..., None, None, None], rows_past, rows_new)
    o_a = _dsa_attention(a_q, a_qi, a_wi, qpos, ki_all, fetch, rel_bias[:, :H_A])

    h0 = None if past is None else (past['s5_re'], past['s5_im'])
    o_b, s5_last = _s5_mixer(b_u, lw, h0)

    c_q = c_q.reshape(bsz, t, H_C, 2, DH_C)
    c_kv = jnp.stack([c_k, c_v], axis=2).reshape(bsz, t, 2, H_C, 2 * DH_C)
    c_all = c_kv if past is None else jnp.concatenate([_gather_pages(past['c_kv'], pt), c_kv], axis=1)
    lam_init = 0.8 - 0.6 * math.exp(-0.3 * layer_idx)
    lam = (jnp.exp(jnp.sum(lw['lam_q1'].astype(f32) * lw['lam_k1'].astype(f32)))
           - jnp.exp(jnp.sum(lw['lam_q2'].astype(f32) * lw['lam_k2'].astype(f32))) + lam_init)
    o_c = _diff_attention(c_q, c_all[:, :, 0].reshape(bsz, -1, H_C, 2, DH_C), c_all[:, :, 1],
                          qpos, lam, rel_bias[:, H_A:])
    o_c = _rmsnorm(o_c, lw['diff_sub_g']) * (1.0 - lam_init)

    d_q = d_q.reshape(bsz, t, H_D, DH_D)
    d_kv = jnp.stack([d_k, d_v], axis=2).reshape(bsz, t, 2, H_D, DH_D)
    logf = jax.nn.log_sigmoid((d_f + lw['fox_b_f']).astype(f32))
    c_new = jnp.cumsum(logf, axis=1)
    if past is None:
        d_all, ck = d_kv, c_new
    else:
        d_all = jnp.concatenate([_gather_pages(past['d_kv'], pt), d_kv], axis=1)
        lf_past = _gather_pages(past['d_logf'], pt).astype(f32)
        ck = jnp.concatenate([lf_past - lax.cumsum(lf_past, axis=1, reverse=True), c_new], axis=1)
    o_d = _forgetting_attention(d_q, d_all[:, :, 0], d_all[:, :, 1], c_new, ck, qpos)

    mix = jnp.concatenate([o_a.reshape(bsz, t, -1), o_b, o_c.reshape(bsz, t, -1),
                           o_d.reshape(bsz, t, -1)], axis=-1)
    x = x + _rmsnorm(mix @ lw['w_out'], lw['g_mix_post'])

    hn = _rmsnorm(x, lw['g_x_pre'])
    qx = (hn @ lw['w_qx']).reshape(bsz, t, H_X, DH_X)
    logits = jnp.einsum('bqhd,bmhd->bhqm', qx, mem_kv[:, :, 0]).astype(f32) * (DH_X ** -0.5)
    p = jax.nn.softmax(logits, axis=-1)
    ox = jnp.einsum('bhqm,bmhd->bqhd', p.astype(mem_kv.dtype), mem_kv[:, :, 1]).reshape(bsz, t, -1)
    x = x + _rmsnorm(ox @ lw['w_ox'], lw['g_x_post'])

    buf = jnp.zeros((bsz, CONV_W - 1, D_FF), x.dtype) if past is None else past['ffn_conv']
    f_out, conv_state = _conv_ffn(_rmsnorm(x, lw['g_ffn_pre']), lw, buf)
    x = x + _rmsnorm(f_out, lw['g_ffn_post'])

    new = {'a_kv': a_kv, 'a_idx_k': a_ki, 'c_kv': c_kv, 'd_kv': d_kv,
           'd_logf': logf.astype(x.dtype),
           's5_re': s5_last.real.astype(x.dtype), 's5_im': s5_last.imag.astype(x.dtype),
           'ffn_conv': conv_state}
    return x, new


def setup_inputs(seed: int = 0) -> dict:
    key = jax.random.key(seed)
    ks = iter(jax.random.split(key, 64))

    def nrm(shape, scale=1.0):
        return jax.random.normal(next(ks), shape, jnp.float32) * scale

    def gain(shape):
        return 1.0 + nrm(shape, 0.05)

    n_pages = PAST_LEN // PAGE_SIZE
    n_pool = (DEC_BATCH * n_pages * 5) // 4
    page_table = jax.random.permutation(next(ks), n_pool)[:DEC_BATCH * n_pages]
    page_table = page_table.reshape(DEC_BATCH, n_pages).astype(jnp.int32)
    a_im0 = jnp.pi * jnp.arange(S5_P, dtype=jnp.float32)
    return {
        'x_prompt': nrm((BATCH, SEQ, D_MODEL)),
        'x_sample': nrm((DEC_BATCH, DEC_SEQ, D_MODEL)),
        'mem_prompt': nrm((BATCH, N_MEM, D_MODEL)),
        'cache_a_kv': nrm((DEPTH, n_pool, PAGE_SIZE, 2, H_A, DH_A)),
        'cache_a_idx_k': nrm((DEPTH, n_pool, PAGE_SIZE, D_IDX)),
        'cache_c_kv': nrm((DEPTH, n_pool, PAGE_SIZE, 2, H_C, 2 * DH_C)),
        'cache_d_kv': nrm((DEPTH, n_pool, PAGE_SIZE, 2, H_D, DH_D)),
        'cache_d_logf': jax.nn.log_sigmoid(2.0 + nrm((DEPTH, n_pool, PAGE_SIZE, H_D))),
        'cache_mem_kv': nrm((DEPTH, DEC_BATCH, N_MEM, 2, H_X, DH_X)),
        'state_s5_re': nrm((DEPTH, DEC_BATCH, S5_G, S5_P), 0.1),
        'state_s5_im': nrm((DEPTH, DEC_BATCH, S5_G, S5_P), 0.1),
        'state_ffn_conv': nrm((DEPTH, DEC_BATCH, CONV_W - 1, D_FF)),
        'page_table': page_table,
        'rel_bias': nrm((NUM_BUCKETS, H_A + H_C), 0.5),
        'g_mix_pre': gain((DEPTH, D_MODEL)),
        'w_in': nrm((DEPTH, D_MODEL, D_IN), D_MODEL ** -0.5),
        's5_a_re': -0.5 + nrm((DEPTH, S5_G, S5_P), 0.01),
        's5_a_im': a_im0 + nrm((DEPTH, S5_G, S5_P), 0.01),
        's5_log_dt': jax.random.uniform(next(ks), (DEPTH, S5_G), jnp.float32,
                                        math.log(1e-3), math.log(1e-1)),
        's5_b_re': nrm((DEPTH, S5_G, S5_P, S5_GROUP), (2 * S5_GROUP) ** -0.5),
        's5_b_im': nrm((DEPTH, S5_G, S5_P, S5_GROUP), (2 * S5_GROUP) ** -0.5),
        's5_c_re': nrm((DEPTH, S5_G, S5_GROUP, S5_P), S5_P ** -0.5),
        's5_c_im': nrm((DEPTH, S5_G, S5_GROUP, S5_P), S5_P ** -0.5),
        's5_d': nrm((DEPTH, S5_CH)),
        's5_w_glu': nrm((DEPTH, S5_CH, S5_CH), S5_CH ** -0.5),
        'lam_q1': nrm((DEPTH, DH_C), 0.1),
        'lam_k1': nrm((DEPTH, DH_C), 0.1),
        'lam_q2': nrm((DEPTH, DH_C), 0.1),
        'lam_k2': nrm((DEPTH, DH_C), 0.1),
        'diff_sub_g': gain((DEPTH, 2 * DH_C)),
        'fox_b_f': 2.0 + nrm((DEPTH, H_D), 0.5),
        'w_out': nrm((DEPTH, D_MODEL, D_MODEL), D_MODEL ** -0.5),
        'g_mix_post': gain((DEPTH, D_MODEL)),
        'g_x_pre': gain((DEPTH, D_MODEL)),
        'w_qx': nrm((DEPTH, D_MODEL, H_X * DH_X), D_MODEL ** -0.5),
        'w_kvx': nrm((DEPTH, D_MODEL, 2 * H_X * DH_X), D_MODEL ** -0.5),
        'w_ox': nrm((DEPTH, H_X * DH_X, D_MODEL), (H_X * DH_X) ** -0.5),
        'g_x_post': gain((DEPTH, D_MODEL)),
        'g_ffn_pre': gain((DEPTH, D_MODEL)),
        'w_gate': nrm((DEPTH, D_MODEL, D_FF), D_MODEL ** -0.5),
        'w_up': nrm((DEPTH, D_MODEL, D_FF), D_MODEL ** -0.5),
        'conv_w': nrm((DEPTH, CONV_W, D_FF), CONV_W ** -0.5),
        'conv_b': nrm((DEPTH, D_FF), 0.01),
        'w_down': nrm((DEPTH, D_FF, D_MODEL), D_FF ** -0.5),
        'g_ffn_post': gain((DEPTH, D_MODEL)),
    }


def reference(x_prompt, x_sample, mem_prompt, cache_a_kv, cache_a_idx_k, cache_c_kv, cache_d_kv,
              cache_d_logf, cache_mem_kv, state_s5_re, state_s5_im, state_ffn_conv, page_table,
              rel_bias, g_mix_pre, w_in, s5_a_re, s5_a_im, s5_log_dt, s5_b_re, s5_b_im, s5_c_re,
              s5_c_im, s5_d, s5_w_glu, lam_q1, lam_k1, lam_q2, lam_k2, diff_sub_g, fox_b_f, w_out,
              g_mix_post, g_x_pre, w_qx, w_kvx, w_ox, g_x_post, g_ffn_pre, w_gate, w_up, conv_w,
              conv_b, w_down, g_ffn_post):
    names = ('a_kv', 'a_idx_k', 'c_kv', 'd_kv', 'd_logf', 's5_re', 's5_im', 'ffn_conv')
    new_p = {n: [] for n in names}
    new_s = {n: [] for n in names}
    mem_kv_p = []
    yp, ys = x_prompt, x_sample
    for l in range(DEPTH):
        lw = {'g_mix_pre': g_mix_pre[l], 'w_in': w_in[l],
              's5_a_re': s5_a_re[l], 's5_a_im': s5_a_im[l], 's5_log_dt': s5_log_dt[l],
              's5_b_re': s5_b_re[l], 's5_b_im': s5_b_im[l], 's5_c_re': s5_c_re[l], 's5_c_im': s5_c_im[l],
              's5_d': s5_d[l], 's5_w_glu': s5_w_glu[l],
              'lam_q1': lam_q1[l], 'lam_k1': lam_k1[l], 'lam_q2': lam_q2[l], 'lam_k2': lam_k2[l],
              'diff_sub_g': diff_sub_g[l], 'fox_b_f': fox_b_f[l],
              'w_out': w_out[l], 'g_mix_post': g_mix_post[l],
              'g_x_pre': g_x_pre[l], 'w_qx': w_qx[l], 'w_ox': w_ox[l], 'g_x_post': g_x_post[l],
              'g_ffn_pre': g_ffn_pre[l], 'w_gate': w_gate[l], 'w_up': w_up[l], 'conv_w': conv_w[l],
              'conv_b': conv_b[l], 'w_down': w_down[l], 'g_ffn_post': g_ffn_post[l]}
        mkv = (mem_prompt @ w_kvx[l]).reshape(mem_prompt.shape[0], N_MEM, 2, H_X, DH_X)
        mem_kv_p.append(mkv)
        yp, np_l = _layer(yp, mkv, lw, rel_bias, l, None)
        past = {'page_table': page_table, 'a_kv': cache_a_kv[l], 'a_idx_k': cache_a_idx_k[l],
                'c_kv': cache_c_kv[l], 'd_kv': cache_d_kv[l], 'd_logf': cache_d_logf[l],
                's5_re': state_s5_re[l], 's5_im': state_s5_im[l], 'ffn_conv': state_ffn_conv[l]}
        ys, ns_l = _layer(ys, cache_mem_kv[l], lw, rel_bias, l, past)
        for n in names:
            new_p[n].append(np_l[n])
            new_s[n].append(ns_l[n])
    P = {n: jnp.stack(v) for n, v in new_p.items()}
    S = {n: jnp.stack(v) for n, v in new_s.items()}
    mem_kv_prompt = jnp.stack(mem_kv_p)
    return (yp, ys,
            P['a_kv'], S['a_kv'], P['a_idx_k'], S['a_idx_k'],
            P['c_kv'], S['c_kv'], P['d_kv'], S['d_kv'], P['d_logf'], S['d_logf'],
            mem_kv_prompt,
            P['s5_re'], S['s5_re'], P['s5_im'], S['s5_im'],
            P['ffn_conv'], S['ffn_conv'])
```

```python
import functools
import math

import numpy as np
import jax
import jax.numpy as jnp
from jax import lax
from jax.experimental import pallas as pl
from jax.experimental.pallas import tpu as pltpu

F32 = jnp.float32
BF16 = jnp.bfloat16
I32 = jnp.int32

EPS = 1e-6
NEG = -1e30
M_INIT = -3e38
INT_MIN = -2 ** 31
BIG_IDX = 2 ** 30

DH = 64
H_IDX = 16
D_IDX = 64
TOPK_MAX = 256
S5_GROUP = 16
S5_P = 64
DH_X = 128
NUM_BUCKETS = 32
MAX_DISTANCE = 128
CONV_W = 3
PAGE = 128

LANES = 128
SUBLANES = 8
VMEM_LIMIT = 52 * 1024 * 1024

ATT_BLK = 256
PAGES_PER_STEP = 8


def _t5_thresholds():
    exact = NUM_BUCKETS // 2
    n = np.arange(exact, MAX_DISTANCE + 1).astype(np.float64)
    large = exact + np.floor(np.log(n / exact) / math.log(MAX_DISTANCE / exact)
                             * (NUM_BUCKETS - exact)).astype(np.int64)
    bucket = np.minimum(large, NUM_BUCKETS - 1)
    return [int(n[np.argmax(bucket >= b)]) for b in range(exact + 1, NUM_BUCKETS)]


_T5_THR = _t5_thresholds()


def _cparams(sem):
    return pltpu.CompilerParams(dimension_semantics=sem, vmem_limit_bytes=VMEM_LIMIT)


def _dot(a, b):
    return jnp.dot(a, b, preferred_element_type=F32)


def _dot_nt(a, b):
    return lax.dot_general(a, b, (((1,), (1,)), ((), ())), preferred_element_type=F32)


def _rms(x, g):
    y = x * lax.rsqrt(jnp.mean(x * x, axis=-1, keepdims=True) + EPS)
    return y * g


def _gelu(x):
    c = math.sqrt(2.0 / math.pi)
    return 0.5 * x * (1.0 + jnp.tanh(c * (x + 0.044715 * (x * x * x))))


def _sigmoid(x):
    return 1.0 / (1.0 + jnp.exp(-x))


def _log_sigmoid(x):
    return jnp.minimum(x, 0.0) - jnp.log(1.0 + jnp.exp(-jnp.abs(x)))


def _norm_cast_kernel(x_ref, g_ref, o_ref):
    o_ref[...] = _rms(x_ref[...], g_ref[...]).astype(o_ref.dtype)


def _norm_cast(x, g):
    m, d = x.shape
    tm = min(m, 512)
    return pl.pallas_call(
        _norm_cast_kernel,
        grid=(m // tm,),
        in_specs=[pl.BlockSpec((tm, d), lambda i: (i, 0)),
                  pl.BlockSpec((1, d), lambda i: (0, 0))],
        out_specs=pl.BlockSpec((tm, d), lambda i: (i, 0)),
        out_shape=jax.ShapeDtypeStruct((m, d), BF16),
        compiler_params=_cparams(("parallel",)),
        name="norm_cast",
    )(x, g.reshape(1, d))


def _mm_kernel(a_ref, w_ref, o_ref):
    o_ref[...] = _dot(a_ref[...], w_ref[...]).astype(o_ref.dtype)


def _matmul(a, w, out_dtype):
    m, k = a.shape
    n = w.shape[1]
    tm = min(m, 1024)
    tn = min(n, 512)
    return pl.pallas_call(
        _mm_kernel,
        grid=(m // tm, n // tn),
        in_specs=[pl.BlockSpec((tm, k), lambda i, j: (i, 0)),
                  pl.BlockSpec((k, tn), lambda i, j: (0, j))],
        out_specs=pl.BlockSpec((tm, tn), lambda i, j: (i, j)),
        out_shape=jax.ShapeDtypeStruct((m, n), out_dtype),
        compiler_params=_cparams(("parallel", "parallel")),
        name="matmul",
    )(a, w)


def _proj_res_kernel(*refs, n_in, emit_next):
    a_refs = refs[:n_in]
    w_ref, x_ref, gp_ref, gn_ref = refs[n_in:n_in + 4]
    outs = refs[n_in + 4:]
    off = 0
    acc = None
    for a_ref in a_refs:
        kk = a_ref.shape[1]
        part = _dot(a_ref[...], w_ref[off:off + kk, :])
        acc = part if acc is None else acc + part
        off += kk
    xn = x_ref[...] + _rms(acc, gp_ref[...])
    outs[0][...] = xn
    if emit_next:
        outs[1][...] = _rms(xn, gn_ref[...]).astype(BF16)


def _proj_res(a_list, w, x, g_post, g_next):
    m, d = x.shape
    k = w.shape[0]
    tm = min(m, 256)
    emit_next = g_next is not None
    gn = g_next if emit_next else g_post
    in_specs = [pl.BlockSpec((tm, a.shape[1]), lambda i: (i, 0)) for a in a_list]
    in_specs += [pl.BlockSpec((k, d), lambda i: (0, 0)),
                 pl.BlockSpec((tm, d), lambda i: (i, 0)),
                 pl.BlockSpec((1, d), lambda i: (0, 0)),
                 pl.BlockSpec((1, d), lambda i: (0, 0))]
    out_specs = [pl.BlockSpec((tm, d), lambda i: (i, 0))]
    out_shape = [jax.ShapeDtypeStruct((m, d), F32)]
    if emit_next:
        out_specs.append(pl.BlockSpec((tm, d), lambda i: (i, 0)))
        out_shape.append(jax.ShapeDtypeStruct((m, d), BF16))
    res = pl.pallas_call(
        functools.partial(_proj_res_kernel, n_in=len(a_list), emit_next=emit_next),
        grid=(m // tm,),
        in_specs=in_specs,
        out_specs=out_specs,
        out_shape=out_shape,
        compiler_params=_cparams(("parallel",)),
        name="proj_res",
    )(*a_list, w, x, g_post.reshape(1, d), gn.reshape(1, d))
    return res[0], (res[1] if emit_next else None)


def _proj_res_kt_kernel(a_ref, w_ref, x_ref, gp_ref, gn_ref, *rest, emit_next):
    if emit_next:
        xo_ref, hn_ref, acc_ref = rest
    else:
        xo_ref, acc_ref = rest
    kk = pl.program_id(1)

    @pl.when(kk == 0)
    def _():
        acc_ref[...] = jnp.zeros_like(acc_ref)

    acc_ref[...] += _dot(a_ref[...], w_ref[...])

    @pl.when(kk == pl.num_programs(1) - 1)
    def _():
        xn = x_ref[...] + _rms(acc_ref[...], gp_ref[...])
        xo_ref[...] = xn
        if emit_next:
            hn_ref[...] = _rms(xn, gn_ref[...]).astype(BF16)


def _proj_res_ktiled(a, w, x, g_post, g_next):
    m, d = x.shape
    k = w.shape[0]
    tm = min(m, 512)
    tk = 512
    emit_next = g_next is not None
    gn = g_next if emit_next else g_post
    out_specs = [pl.BlockSpec((tm, d), lambda i, j: (i, 0))]
    out_shape = [jax.ShapeDtypeStruct((m, d), F32)]
    if emit_next:
        out_specs.append(pl.BlockSpec((tm, d), lambda i, j: (i, 0)))
        out_shape.append(jax.ShapeDtypeStruct((m, d), BF16))
    res = pl.pallas_call(
        functools.partial(_proj_res_kt_kernel, emit_next=emit_next),
        grid=(m // tm, k // tk),
        in_specs=[pl.BlockSpec((tm, tk), lambda i, j: (i, j)),
                  pl.BlockSpec((tk, d), lambda i, j: (j, 0)),
                  pl.BlockSpec((tm, d), lambda i, j: (i, 0)),
                  pl.BlockSpec((1, d), lambda i, j: (0, 0)),
                  pl.BlockSpec((1, d), lambda i, j: (0, 0))],
        out_specs=out_specs,
        out_shape=out_shape,
        scratch_shapes=[pltpu.VMEM((tm, d), F32)],
        compiler_params=_cparams(("parallel", "arbitrary")),
        name="proj_res_ktiled",
    )(a, w, x, g_post.reshape(1, d), gn.reshape(1, d))
    return res[0], (res[1] if emit_next else None)


def _ffn_hidden_kernel(hn_ref, wg_ref, wu_ref, cw_ref, cb_ref, hb1_ref, hb2_ref,
                       h_ref, cs_ref, *, seq_len, n_seq):
    hn = hn_ref[...]
    g = _dot(hn, wg_ref[...])
    u = _dot(hn, wu_ref[...])
    w0 = cw_ref[0:1, :]
    w1 = cw_ref[1:2, :]
    w2 = cw_ref[2:3, :]
    cb = cb_ref[...]
    if seq_len > SUBLANES:
        gc = cb + w0 * pltpu.roll(g, 2, 0) + w1 * pltpu.roll(g, 1, 0) + w2 * g
        h_ref[...] = (_gelu(gc) * u).astype(h_ref.dtype)
    row = lax.broadcasted_iota(I32, (SUBLANES, 1), 0)
    for s in range(n_seq):
        r0 = s * seq_len
        g8 = g[r0:r0 + SUBLANES, :]
        p1 = jnp.where(row < 1, hb1_ref[s], pltpu.roll(g8, 1, 0))
        p2 = jnp.where(row < 2, hb2_ref[s], pltpu.roll(g8, 2, 0))
        gc8 = cb + w0 * p2 + w1 * p1 + w2 * g8
        h_ref[r0:r0 + SUBLANES, :] = (_gelu(gc8) * u[r0:r0 + SUBLANES, :]).astype(h_ref.dtype)
        cs_ref[s] = g[r0 + seq_len - 2:r0 + seq_len, :]


def _ffn_hidden(hn, w_gate, w_up, conv_w, conv_b, buf, seq_len):
    m, d = hn.shape
    f = w_gate.shape[1]
    bsz = m // seq_len
    n_seq = 1 if seq_len > SUBLANES else bsz
    tt = seq_len * n_seq
    tf = 256
    zeros = jnp.zeros((bsz, SUBLANES - 2, f), F32)
    hb1 = jnp.concatenate([buf[:, 1:2], jnp.zeros((bsz, 1, f), F32), zeros], axis=1)
    hb2 = jnp.concatenate([buf, zeros], axis=1)
    h, cs = pl.pallas_call(
        functools.partial(_ffn_hidden_kernel, seq_len=seq_len, n_seq=n_seq),
        grid=(m // tt, f // tf),
        in_specs=[pl.BlockSpec((tt, d), lambda i, j: (i, 0)),
                  pl.BlockSpec((d, tf), lambda i, j: (0, j)),
                  pl.BlockSpec((d, tf), lambda i, j: (0, j)),
                  pl.BlockSpec((CONV_W, tf), lambda i, j: (0, j)),
                  pl.BlockSpec((1, tf), lambda i, j: (0, j)),
                  pl.BlockSpec((n_seq, SUBLANES, tf), lambda i, j: (i, 0, j)),
                  pl.BlockSpec((n_seq, SUBLANES, tf), lambda i, j: (i, 0, j))],
        out_specs=[pl.BlockSpec((tt, tf), lambda i, j: (i, j)),
                   pl.BlockSpec((n_seq, CONV_W - 1, tf), lambda i, j: (i, 0, j))],
        out_shape=[jax.ShapeDtypeStruct((m, f), BF16),
                   jax.ShapeDtypeStruct((bsz, CONV_W - 1, f), F32)],
        compiler_params=_cparams(("parallel", "parallel")),
        name="ffn_hidden",
    )(hn, w_gate, w_up, conv_w, conv_b.reshape(1, f), hb1, hb2)
    return h, cs


def _bias_tile_kernel(tab_ref, o_ref, *, off):
    nh, r, c = o_ref.shape
    i = lax.broadcasted_iota(I32, (r, c), 0)
    j = lax.broadcasted_iota(I32, (r, c), 1)
    n = jnp.maximum(off + i - j, 0)
    large = jnp.full((r, c), NUM_BUCKETS // 2, I32)
    for thr in _T5_THR:
        large = large + jnp.where(n >= thr, 1, 0)
    bucket = jnp.where(n < NUM_BUCKETS // 2, n, large)

    def head(h, carry):
        val = jnp.full((r, c), tab_ref[NUM_BUCKETS - 1, h], F32)
        for b in range(NUM_BUCKETS - 2, -1, -1):
            val = jnp.where(bucket == b, tab_ref[b, h], val)
        o_ref[h] = val
        return carry
    lax.fori_loop(0, nh, head, 0)


def _bias_tile(rel_bias, off, r, c):
    nh = rel_bias.shape[1]
    return pl.pallas_call(
        functools.partial(_bias_tile_kernel, off=off),
        in_specs=[pl.BlockSpec(memory_space=pltpu.SMEM)],
        out_specs=pl.BlockSpec(memory_space=pltpu.VMEM),
        out_shape=jax.ShapeDtypeStruct((nh, r, c), F32),
        compiler_params=pltpu.CompilerParams(vmem_limit_bytes=VMEM_LIMIT),
        name="t5_bias_tile",
    )(rel_bias)


def _logsig_cumsum_kernel(x_ref, b_ref, lf_ref, c_ref):
    t = x_ref.shape[1]
    lf = _log_sigmoid(x_ref[0] + b_ref[...])
    lf_ref[0] = lf
    row = lax.broadcasted_iota(I32, (t, 1), 0)
    c = lf
    s = 1
    while s < t:
        c = c + jnp.where(row >= s, pltpu.roll(c, s, 0), 0.0)
        s *= 2
    c_ref[0] = c


def _logsig_cumsum(x, bvec):
    bsz, t, w = x.shape
    return pl.pallas_call(
        _logsig_cumsum_kernel,
        grid=(bsz,),
        in_specs=[pl.BlockSpec((1, t, w), lambda b: (b, 0, 0)),
                  pl.BlockSpec((1, w), lambda b: (0, 0))],
        out_specs=[pl.BlockSpec((1, t, w), lambda b: (b, 0, 0)),
                   pl.BlockSpec((1, t, w), lambda b: (b, 0, 0))],
        out_shape=[jax.ShapeDtypeStruct((bsz, t, w), F32),
                   jax.ShapeDtypeStruct((bsz, t, w), F32)],
        compiler_params=_cparams(("parallel",)),
        name="logsig_cumsum",
    )(x, bvec)


def _page_suffix_kernel(pt_ref, *refs, n_h):
    del pt_ref
    pp = PAGES_PER_STEP
    page_refs = refs[:pp]
    o_ref, x_ref, carry_ref = refs[pp:]
    s = pl.program_id(1)
    w = o_ref.shape[2]

    @pl.when(s == 0)
    def _():
        carry_ref[...] = jnp.zeros_like(carry_ref)

    for p in range(pp):
        x_ref[p:p + 1, :] = page_refs[p][0, 0]
    x0 = x_ref[...]
    lane = lax.broadcasted_iota(I32, (1, w), 1)
    row = lax.broadcasted_iota(I32, (pp, 1), 0)
    x = x0
    tot = x0
    sh = n_h
    while sh < w:
        x = x + jnp.where(lane + sh < w, pltpu.roll(x, w - sh, 1), 0.0)
        tot = tot + pltpu.roll(tot, sh, 1)
        sh *= 2
    z = tot
    sh = 1
    while sh < pp:
        z = z + jnp.where(row + sh < pp, pltpu.roll(z, pp - sh, 0), 0.0)
        sh *= 2
    carry = carry_ref[...]
    o_ref[0] = (x - x0) + (z - tot) + carry
    carry_ref[...] = carry + z[0:1, :]


def _page_suffix(lf_pool, layer, page_table):
    n_pool, page, n_h = lf_pool.shape[1:]
    bsz, n_pages = page_table.shape
    pp = PAGES_PER_STEP
    ns = n_pages // pp
    w = page * n_h
    pool = lf_pool.reshape(lf_pool.shape[0], n_pool, 1, w)

    def page_map(p):
        return lambda b, s, pt: (layer, pt[b, (ns - 1 - s) * pp + p], 0, 0)

    out = pl.pallas_call(
        functools.partial(_page_suffix_kernel, n_h=n_h),
        grid_spec=pltpu.PrefetchScalarGridSpec(
            num_scalar_prefetch=1,
            grid=(bsz, ns),
            in_specs=[pl.BlockSpec((1, 1, 1, w), page_map(p)) for p in range(pp)],
            out_specs=pl.BlockSpec((1, pp, w), lambda b, s, pt: (b, ns - 1 - s, 0)),
            scratch_shapes=[pltpu.VMEM((pp, w), F32), pltpu.VMEM((1, w), F32)]),
        out_shape=jax.ShapeDtypeStruct((bsz, n_pages, w), F32),
        compiler_params=_cparams(("parallel", "arbitrary")),
        name="page_suffix",
    )(page_table, *([pool] * pp))
    return out.reshape(bsz, n_pages * page, n_h)


def _flash_update(s, v, m_ref, l_ref, acc_ref, idx):
    m_old = m_ref[idx]
    m_new = jnp.maximum(m_old, jnp.max(s, axis=-1, keepdims=True))
    alpha = jnp.exp(m_old - m_new)
    p = jnp.exp(s - m_new)
    l_ref[idx] = alpha * l_ref[idx] + jnp.sum(p, axis=-1, keepdims=True)
    acc_ref[idx] = alpha * acc_ref[idx] + _dot(p.astype(BF16), v)
    m_ref[idx] = m_new


def _init_flash(m_ref, l_ref, acc_ref):
    m_ref[...] = jnp.full(m_ref.shape, M_INIT, F32)
    l_ref[...] = jnp.zeros_like(l_ref)
    acc_ref[...] = jnp.zeros_like(acc_ref)


def _sortable(x):
    bits = lax.bitcast_convert_type(x + 0.0, I32)
    return jnp.where(bits < 0, bits ^ 0x7FFFFFFF, bits)


def _lane_fold(x):
    n = x.shape[1] // LANES
    acc = x[:, 0:LANES]
    for i in range(1, n):
        acc = acc + x[:, i * LANES:(i + 1) * LANES]
    return acc


def _topk_select(count_fn, rows, k, n_idx_bits):
    kf = float(k)

    def bit_body(i, ans):
        cand = ans + lax.shift_left(jnp.int32(1), 31 - i)
        cnt = count_fn(lambda key, idx: key >= cand)
        return jnp.where(cnt >= kf, cand, ans)

    thr = lax.fori_loop(0, 32, bit_body, jnp.full((rows, 1), INT_MIN, I32))
    n_ge = count_fn(lambda key, idx: key >= thr)
    n_gt = count_fn(lambda key, idx: key > thr)
    need = kf - n_gt

    def tie_search():
        def idx_body(i, c):
            cand = c + lax.shift_left(jnp.int32(1), n_idx_bits - 1 - i)
            cnt = count_fn(lambda key, idx: (key == thr) & (idx < cand))
            return jnp.where(cnt < need, cand, c)
        return lax.fori_loop(0, n_idx_bits, idx_body, jnp.zeros((rows, 1), I32))

    excess = jnp.max(n_ge - kf) > 0.0
    cut = lax.cond(excess, tie_search, lambda: jnp.full((rows, 1), BIG_IDX, I32))
    cut = jnp.where(n_ge > kf, cut, BIG_IDX)
    return thr, cut


def _causal_blocks(step_fn, qi):
    def far_body(j, c):
        step_fn(j, "far")
        return c
    lax.fori_loop(0, qi - 1, far_body, 0)

    @pl.when(qi >= 1)
    def _():
        step_fn(jnp.maximum(qi - 1, 0), "prev")

    step_fn(qi, "diag")


def _causal_neg(tq, tk):
    row = lax.broadcasted_iota(I32, (tq, tk), 0)
    col = lax.broadcasted_iota(I32, (tq, tk), 1)
    return col <= row


def _fox_prompt_kernel(q_ref, kv_ref, cq_ref, nck_ref, o_ref, m_ref, l_ref, acc_ref, *, n_h, col0):
    tq = q_ref.shape[1]
    tk = tq
    qi = pl.program_id(1)
    _init_flash(m_ref, l_ref, acc_ref)
    q = q_ref[0] * 0.125
    cq = cq_ref[0]
    causal = _causal_neg(tq, tk)
    hw = n_h * DH

    def step(j, kind):
        kvb = kv_ref[0, pl.ds(pl.multiple_of(j * tk, tk), tk), :]
        nck = nck_ref[0, j]
        for h in range(n_h):
            k = kvb[:, h * DH:(h + 1) * DH].astype(BF16)
            v = kvb[:, hw + h * DH:hw + (h + 1) * DH].astype(BF16)
            s = _dot_nt(q[:, h * DH:(h + 1) * DH], k)
            s = s + cq[:, col0 + h:col0 + h + 1] + nck[h:h + 1, :]
            if kind == "diag":
                s = jnp.where(causal, s, NEG)
            _flash_update(s, v, m_ref, l_ref, acc_ref, h)

    _causal_blocks(step, qi)
    for h in range(n_h):
        o_ref[0, :, h * DH:(h + 1) * DH] = (acc_ref[h] / l_ref[h]).astype(o_ref.dtype)


def _fox_prompt(q_all, q_blk, kv, c_all, negck, n_h, col0):
    bsz, t, _ = kv.shape
    tq = min(t, ATT_BLK)
    nb = t // tq
    hw = n_h * DH
    nck = negck.reshape(bsz, n_h, nb, tq).transpose(0, 2, 1, 3)
    return pl.pallas_call(
        functools.partial(_fox_prompt_kernel, n_h=n_h, col0=col0),
        grid=(bsz, nb),
        in_specs=[pl.BlockSpec((1, tq, hw), lambda b, i: (b, i, q_blk)),
                  pl.BlockSpec((1, t, 2 * hw), lambda b, i: (b, 0, 0)),
                  pl.BlockSpec((1, tq, LANES), lambda b, i: (b, i, 0)),
                  pl.BlockSpec((1, nb, n_h, tq), lambda b, i: (b, 0, 0, 0))],
        out_specs=pl.BlockSpec((1, tq, hw), lambda b, i: (b, i, 0)),
        out_shape=jax.ShapeDtypeStruct((bsz, t, hw), BF16),
        scratch_shapes=[pltpu.VMEM((n_h, tq, 1), F32), pltpu.VMEM((n_h, tq, 1), F32),
                        pltpu.VMEM((n_h, tq, DH), F32)],
        compiler_params=_cparams(("parallel", "parallel")),
        name="fox_prompt",
    )(q_all, kv, c_all, nck)


def _diff_finish(a0, l0, a1, l1, lam, gsub, scale):
    o = a0 / l0 - lam * (a1 / l1)
    return _rms(o, gsub) * scale


def _diff_prompt_kernel(far_ref, lam_ref, q_ref, kv_ref, bias_ref, gs_ref, o_ref,
                        m_ref, l_ref, acc_ref, *, n_h, out_scale):
    tq = q_ref.shape[1]
    tk = tq
    qi = pl.program_id(1)
    _init_flash(m_ref, l_ref, acc_ref)
    q = q_ref[0] * 0.125
    causal = _causal_neg(tq, tk)
    dv = 2 * DH
    hw = n_h * dv

    def step(j, kind):
        kvb = kv_ref[0, pl.ds(pl.multiple_of(j * tk, tk), tk), :]
        for h in range(n_h):
            v = kvb[:, hw + h * dv:hw + (h + 1) * dv].astype(BF16)
            if kind == "far":
                bias = far_ref[h]
            else:
                bias = bias_ref[h, 0 if kind == "prev" else 1]
            for jj in range(2):
                vh = 2 * h + jj
                k = kvb[:, vh * DH:(vh + 1) * DH].astype(BF16)
                s = _dot_nt(q[:, vh * DH:(vh + 1) * DH], k) + bias
                if kind == "diag":
                    s = jnp.where(causal, s, NEG)
                _flash_update(s, v, m_ref, l_ref, acc_ref, vh)

    _causal_blocks(step, qi)
    lam = lam_ref[0]
    for h in range(n_h):
        o = _diff_finish(acc_ref[2 * h], l_ref[2 * h], acc_ref[2 * h + 1], l_ref[2 * h + 1],
                         lam, gs_ref[...], out_scale)
        o_ref[0, :, h * dv:(h + 1) * dv] = o.astype(o_ref.dtype)


def _diff_prompt(q_all, q_blk, kv, bias, far, lam, gsub, n_h, out_scale):
    bsz, t, _ = kv.shape
    tq = min(t, ATT_BLK)
    nb = t // tq
    hw = n_h * 2 * DH
    smem = pl.BlockSpec(memory_space=pltpu.SMEM)
    return pl.pallas_call(
        functools.partial(_diff_prompt_kernel, n_h=n_h, out_scale=out_scale),
        grid=(bsz, nb),
        in_specs=[smem, smem,
                  pl.BlockSpec((1, tq, hw), lambda b, i: (b, i, q_blk)),
                  pl.BlockSpec((1, t, 2 * hw), lambda b, i: (b, 0, 0)),
                  pl.BlockSpec((n_h, 2, tq, tq), lambda b, i: (0, 0, 0, 0)),
                  pl.BlockSpec((1, 2 * DH), lambda b, i: (0, 0))],
        out_specs=pl.BlockSpec((1, tq, hw), lambda b, i: (b, i, 0)),
        out_shape=jax.ShapeDtypeStruct((bsz, t, hw), BF16),
        scratch_shapes=[pltpu.VMEM((2 * n_h, tq, 1), F32), pltpu.VMEM((2 * n_h, tq, 1), F32),
                        pltpu.VMEM((2 * n_h, tq, 2 * DH), F32)],
        compiler_params=_cparams(("parallel", "parallel")),
        name="diff_prompt",
    )(far, lam, q_all, kv, bias, gsub.reshape(1, 2 * DH))


def _dsa_prompt_kernel(far_ref, q_ref, qi_ref, wq_ref, ki_ref, kv_ref, bias_ref, o_ref,
                       key_ref, m_ref, l_ref, acc_ref, *, n_h, topk, wi_col0):
    tq = q_ref.shape[1]
    tk = tq
    nb = key_ref.shape[0]
    qi = pl.program_id(1)
    _init_flash(m_ref, l_ref, acc_ref)
    causal = _causal_neg(tq, tk)
    hw = n_h * DH

    qidx = qi_ref[0]
    wrow = wq_ref[0][:, wi_col0:wi_col0 + H_IDX] * (1.0 / (math.sqrt(H_IDX) * math.sqrt(D_IDX)))

    def score_block(j, diag):
        kib = ki_ref[0, pl.ds(pl.multiple_of(j * tk, tk), tk), :][:, 0:D_IDX].astype(BF16)
        sc = jnp.zeros((tq, tk), F32)
        for h in range(H_IDX):
            d = _dot_nt(qidx[:, h * D_IDX:(h + 1) * D_IDX], kib)
            sc = sc + jnp.maximum(d, 0.0) * wrow[:, h:h + 1]
        if diag:
            sc = jnp.where(causal, sc, -jnp.inf)
        key_ref[j] = _sortable(sc)

    def score_body(j, c):
        score_block(j, False)
        return c
    lax.fori_loop(0, qi, score_body, 0)
    score_block(qi, True)

    col = lax.broadcasted_iota(I32, (tq, tk), 1)

    def count_fn(pred):
        def body(j, part):
            hit = pred(key_ref[j], col + j * tk)
            return part + _lane_fold(jnp.where(hit, 1.0, 0.0))
        part = lax.fori_loop(0, qi + 1, body, jnp.zeros((tq, LANES), F32))
        return jnp.sum(part, axis=1, keepdims=True)

    n_bits = max(1, int(math.ceil(math.log2(nb * tk))))
    thr, cut = _topk_select(count_fn, tq, topk, n_bits)

    q = q_ref[0] * 0.125

    def step(j, kind):
        kvb = kv_ref[0, pl.ds(pl.multiple_of(j * tk, tk), tk), :]
        key = key_ref[j]
        sel = (key > thr) | ((key == thr) & (col + j * tk <= cut))
        if kind == "diag":
            sel = sel & causal
        for h in range(n_h):
            k = kvb[:, h * DH:(h + 1) * DH].astype(BF16)
            v = kvb[:, hw + h * DH:hw + (h + 1) * DH].astype(BF16)
            if kind == "far":
                bias = far_ref[h]
            else:
                bias = bias_ref[h, 0 if kind == "prev" else 1]
            s = _dot_nt(q[:, h * DH:(h + 1) * DH], k) + bias
            s = jnp.where(sel, s, NEG)
            _flash_update(s, v, m_ref, l_ref, acc_ref, h)

    _causal_blocks(step, qi)
    for h in range(n_h):
        o_ref[0, :, h * DH:(h + 1) * DH] = (acc_ref[h] / l_ref[h]).astype(o_ref.dtype)


def _dsa_prompt(q_all, q_blk, small, kv, bias, far, n_h, topk, wi_col0):
    bsz, t, _ = kv.shape
    tq = min(t, ATT_BLK)
    nb = t // tq
    hw = n_h * DH
    smem = pl.BlockSpec(memory_space=pltpu.SMEM)
    return pl.pallas_call(
        functools.partial(_dsa_prompt_kernel, n_h=n_h, topk=topk, wi_col0=wi_col0),
        grid=(bsz, nb),
        in_specs=[smem,
                  pl.BlockSpec((1, tq, hw), lambda b, i: (b, i, q_blk)),
                  pl.BlockSpec((1, tq, H_IDX * D_IDX), lambda b, i: (b, i, 0)),
                  pl.BlockSpec((1, tq, LANES), lambda b, i: (b, i, 0)),
                  pl.BlockSpec((1, t, LANES), lambda b, i: (b, 0, 0)),
                  pl.BlockSpec((1, t, 2 * hw), lambda b, i: (b, 0, 0)),
                  pl.BlockSpec((n_h, 2, tq, tq), lambda b, i: (0, 0, 0, 0))],
        out_specs=pl.BlockSpec((1, tq, hw), lambda b, i: (b, i, 0)),
        out_shape=jax.ShapeDtypeStruct((bsz, t, hw), BF16),
        scratch_shapes=[pltpu.VMEM((nb, tq, tq), I32),
                        pltpu.VMEM((n_h, tq, 1), F32), pltpu.VMEM((n_h, tq, 1), F32),
                        pltpu.VMEM((n_h, tq, DH), F32)],
        compiler_params=_cparams(("parallel", "parallel")),
        name="dsa_prompt",
    )(far, q_all, q_all, small, small, kv, bias)


def _dsa_index_sample_kernel(pt_ref, qi_ref, w_ref, *refs, topk, n_q):
    del pt_ref
    pp = PAGES_PER_STEP
    page_refs = refs[:pp]
    kinew_ref, o_ref = refs[pp:]
    s = pl.program_id(1)
    ns = pl.num_programs(1) - 1
    nblk = o_ref.shape[1]
    wblk = o_ref.shape[3]
    qrows = qi_ref[0]
    w = w_ref[0]

    def scores(ki):
        d = _dot_nt(qrows, ki.astype(BF16))
        r = jnp.maximum(d, 0.0) * w
        sc = r[0:n_q, :]
        for h in range(1, H_IDX):
            sc = sc + r[h * n_q:(h + 1) * n_q, :]
        return sc

    @pl.when(s < ns)
    def _():
        o_ref[0, s] = jnp.concatenate([scores(page_refs[p][0, 0]) for p in range(pp)], axis=1)

    @pl.when(s == ns)
    def _():
        row = lax.broadcasted_iota(I32, (n_q, PAGE), 0)
        colp = lax.broadcasted_iota(I32, (n_q, PAGE), 1)
        sc = jnp.where(colp <= row, scores(kinew_ref[0]), -jnp.inf)
        pad = jnp.full((n_q, wblk - PAGE), -jnp.inf, F32)
        o_ref[0, ns] = jnp.concatenate([sc, pad], axis=1)

        col = lax.broadcasted_iota(I32, (n_q, wblk), 1)

        def count_fn(pred):
            def body(j, part):
                hit = pred(_sortable(o_ref[0, j]), col + j * wblk)
                return part + _lane_fold(jnp.where(hit, 1.0, 0.0))
            part = lax.fori_loop(0, nblk, body, jnp.zeros((n_q, LANES), F32))
            return jnp.sum(part, axis=1, keepdims=True)

        n_bits = max(1, int(math.ceil(math.log2(nblk * wblk))))
        thr, cut = _topk_select(count_fn, n_q, topk, n_bits)

        def mask_body(j, c):
            key = _sortable(o_ref[0, j])
            sel = (key > thr) | ((key == thr) & (col + j * wblk <= cut))
            valid = o_ref[0, j] > -jnp.inf
            o_ref[0, j] = jnp.where(sel & valid, 0.0, NEG)
            return c
        lax.fori_loop(0, nblk, mask_body, 0)


def _dsa_index_sample(qi_rows, w_rows, ki_pool, layer, page_table, ki_new, topk, n_q):
    bsz, n_pages = page_table.shape
    pp = PAGES_PER_STEP
    ns = n_pages // pp
    wblk = pp * PAGE

    def page_map(p):
        return lambda b, s, pt: (layer, pt[b, jnp.minimum(s, ns - 1) * pp + p], 0, 0)

    rows = qi_rows.shape[1]
    return pl.pallas_call(
        functools.partial(_dsa_index_sample_kernel, topk=topk, n_q=n_q),
        grid_spec=pltpu.PrefetchScalarGridSpec(
            num_scalar_prefetch=1,
            grid=(bsz, ns + 1),
            in_specs=[pl.BlockSpec((1, rows, D_IDX), lambda b, s, pt: (b, 0, 0)),
                      pl.BlockSpec((1, rows, 1), lambda b, s, pt: (b, 0, 0))]
                     + [pl.BlockSpec((1, 1, PAGE, D_IDX), page_map(p)) for p in range(pp)]
                     + [pl.BlockSpec((1, PAGE, D_IDX), lambda b, s, pt: (b, 0, 0))],
            out_specs=pl.BlockSpec((1, ns + 1, n_q, wblk), lambda b, s, pt: (b, 0, 0, 0))),
        out_shape=jax.ShapeDtypeStruct((bsz, ns + 1, n_q, wblk), F32),
        compiler_params=_cparams(("parallel", "arbitrary")),
        name="dsa_index_sample",
    )(page_table, qi_rows, w_rows, *([ki_pool] * pp), ki_new)


def _paged_attn_kernel(pt_ref, *refs, mode, n_q, n_h, dv, out_scale):
    del pt_ref
    pp = PAGES_PER_STEP
    it = iter(refs)
    lam_ref = next(it) if mode == "diff" else None
    qbd_ref = next(it)
    page_refs = [next(it) for _ in range(pp)]
    kvnew_ref = next(it)
    add_ref = next(it) if mode in ("dsa", "fox") else None
    rowc_ref = next(it)
    bprev_ref = next(it)
    bnew_ref = next(it)
    gs_ref = next(it) if mode == "diff" else None
    o_ref, m_ref, l_ref, acc_ref = next(it), next(it), next(it), next(it)

    s = pl.program_id(1)
    ns = pl.num_programs(1) - 1
    rows = qbd_ref.shape[1]
    kw = qbd_ref.shape[2]
    n_vh = rows // n_q

    @pl.when(s == 0)
    def _():
        _init_flash(m_ref, l_ref, acc_ref)

    qbd = qbd_ref[0]
    rowc = rowc_ref[0]

    def row_add(blk):
        if mode == "dsa":
            return jnp.concatenate([blk] * n_vh, axis=0)
        if mode == "fox":
            return jnp.concatenate(
                [jnp.broadcast_to(blk[h:h + 1, :], (n_q, blk.shape[1])) for h in range(n_vh)], axis=0)
        return None

    @pl.when(s < ns)
    def _():
        kvs = [page_refs[p][0, 0] for p in range(pp)]
        x = jnp.concatenate([_dot_nt(qbd, kv[:, 0:kw].astype(BF16)) for kv in kvs], axis=1)
        x = x + rowc
        if add_ref is not None:
            x = x + row_add(add_ref[0, 0])
        last = jnp.where(s == ns - 1, bprev_ref[...], 0.0)
        x = jnp.concatenate([x[:, 0:(pp - 1) * PAGE], x[:, (pp - 1) * PAGE:] + last], axis=1)
        m_old = m_ref[0]
        m_new = jnp.maximum(m_old, jnp.max(x, axis=-1, keepdims=True))
        alpha = jnp.exp(m_old - m_new)
        p32 = jnp.exp(x - m_new)
        l_ref[0] = alpha * l_ref[0] + jnp.sum(p32, axis=-1, keepdims=True)
        p = p32.astype(BF16)
        pv = _dot(p[:, 0:PAGE], kvs[0][:, kw:].astype(BF16))
        for i in range(1, pp):
            pv = pv + _dot(p[:, i * PAGE:(i + 1) * PAGE], kvs[i][:, kw:].astype(BF16))
        acc_ref[0] = alpha * acc_ref[0] + pv
        m_ref[0] = m_new

    @pl.when(s == ns)
    def _():
        kv = kvnew_ref[0]
        x = _dot_nt(qbd, kv[:, 0:kw].astype(BF16)) + rowc + bnew_ref[...]
        if add_ref is not None:
            x = x + row_add(add_ref[0, 0][:, 0:PAGE])
        _flash_update(x, kv[:, kw:].astype(BF16), m_ref, l_ref, acc_ref, 0)
        acc = acc_ref[0]
        l = l_ref[0]
        if mode == "diff":
            lam = lam_ref[0]
            outs = []
            for h in range(n_h):
                r0 = 2 * h * n_q
                r1 = r0 + n_q
                outs.append(_diff_finish(acc[r0:r0 + n_q, h * dv:(h + 1) * dv], l[r0:r0 + n_q],
                                         acc[r1:r1 + n_q, h * dv:(h + 1) * dv], l[r1:r1 + n_q],
                                         lam, gs_ref[...], out_scale))
        else:
            outs = [acc[h * n_q:(h + 1) * n_q, h * dv:(h + 1) * dv] / l[h * n_q:(h + 1) * n_q]
                    for h in range(n_h)]
        o_ref[0] = jnp.concatenate(outs, axis=1).astype(o_ref.dtype)


def _paged_attn(mode, qbd, kv_pool, layer, page_table, kv_new, add, rowc, bprev, bnew,
                n_q, n_h, dv, lam=None, gsub=None, out_scale=1.0):
    bsz, n_pages = page_table.shape
    pp = PAGES_PER_STEP
    ns = n_pages // pp
    rows, kw = qbd.shape[1:]
    vw = kv_pool.shape[-1] - kw
    pool = kv_pool

    def page_map(p):
        return lambda b, s, pt: (layer, pt[b, jnp.minimum(s, ns - 1) * pp + p], 0, 0)

    in_specs = []
    args = []
    if mode == "diff":
        in_specs.append(pl.BlockSpec(memory_space=pltpu.SMEM))
        args.append(lam)
    in_specs.append(pl.BlockSpec((1, rows, kw), lambda b, s, pt: (b, 0, 0)))
    args.append(qbd)
    in_specs += [pl.BlockSpec((1, 1, PAGE, kw + vw), page_map(p)) for p in range(pp)]
    args += [pool] * pp
    in_specs.append(pl.BlockSpec((1, PAGE, kw + vw), lambda b, s, pt: (b, 0, 0)))
    args.append(kv_new)
    if add is not None:
        in_specs.append(pl.BlockSpec((1, 1, add.shape[2], add.shape[3]), lambda b, s, pt: (b, s, 0, 0)))
        args.append(add)
    in_specs.append(pl.BlockSpec((1, rows, 1), lambda b, s, pt: (b, 0, 0)))
    args.append(rowc)
    in_specs.append(pl.BlockSpec((rows, PAGE), lambda b, s, pt: (0, 0)))
    args.append(bprev)
    in_specs.append(pl.BlockSpec((rows, PAGE), lambda b, s, pt: (0, 0)))
    args.append(bnew)
    if mode == "diff":
        in_specs.append(pl.BlockSpec((1, dv), lambda b, s, pt: (0, 0)))
        args.append(gsub.reshape(1, dv))
    return pl.pallas_call(
        functools.partial(_paged_attn_kernel, mode=mode, n_q=n_q, n_h=n_h, dv=dv, out_scale=out_scale),
        grid_spec=pltpu.PrefetchScalarGridSpec(
            num_scalar_prefetch=1,
            grid=(bsz, ns + 1),
            in_specs=in_specs,
            out_specs=pl.BlockSpec((1, n_q, n_h * dv), lambda b, s, pt: (b, 0, 0)),
            scratch_shapes=[pltpu.VMEM((1, rows, 1), F32), pltpu.VMEM((1, rows, 1), F32),
                            pltpu.VMEM((1, rows, vw), F32)]),
        out_shape=jax.ShapeDtypeStruct((bsz, n_q, n_h * dv), BF16),
        compiler_params=_cparams(("parallel", "arbitrary")),
        name="paged_attn_" + mode,
    )(page_table, *args)


def _s5_kernel(u_ref, h0r_ref, h0i_ref, bre_ref, bim_ref, cre_ref, cim_ref, alr_ref, ali_ref,
               acr_ref, aci_ref, d_ref, wglu_ref, o_ref, sr_ref, si_ref,
               hr_ref, hi_ref, cr_ref, ci_ref):
    t = pl.program_id(1)
    ln = u_ref.shape[1]

    @pl.when(t == 0)
    def _():
        cr_ref[...] = h0r_ref[0]
        ci_ref[...] = h0i_ref[0]

    u = u_ref[0]
    ub = u.astype(BF16)
    xr = _dot(ub, bre_ref[...])
    xi = _dot(ub, bim_ref[...])
    row = lax.broadcasted_iota(I32, (ln, 1), 0) % SUBLANES
    for kk, sh in enumerate((1, 2, 4)):
        ar = alr_ref[kk:kk + 1, :]
        ai = ali_ref[kk:kk + 1, :]
        pr = pltpu.roll(xr, sh, 0)
        pi = pltpu.roll(xi, sh, 0)
        keep = row >= sh
        xr, xi = (xr + jnp.where(keep, ar * pr - ai * pi, 0.0),
                  xi + jnp.where(keep, ar * pi + ai * pr, 0.0))
    hr_ref[...] = xr
    hi_ref[...] = xi
    acr = acr_ref[...]
    aci = aci_ref[...]

    def group(r, carry):
        cr, ci = carry
        sl = pl.ds(pl.multiple_of(r * SUBLANES, SUBLANES), SUBLANES)
        br = hr_ref[sl, :] + acr * cr - aci * ci
        bi = hi_ref[sl, :] + acr * ci + aci * cr
        hr_ref[sl, :] = br
        hi_ref[sl, :] = bi
        return br[SUBLANES - 1:SUBLANES, :], bi[SUBLANES - 1:SUBLANES, :]

    cr, ci = lax.fori_loop(0, ln // SUBLANES, group, (cr_ref[...], ci_ref[...]))
    cr_ref[...] = cr
    ci_ref[...] = ci
    y = _dot(hr_ref[...].astype(BF16), cre_ref[...]) - _dot(hi_ref[...].astype(BF16), cim_ref[...])
    y = _gelu(y + d_ref[...] * u)
    z = _dot(y.astype(BF16), wglu_ref[...])
    o_ref[0] = (y * _sigmoid(z)).astype(o_ref.dtype)
    sr_ref[0] = cr
    si_ref[0] = ci


def _s5_mixer(u, h0r, h0i, prm):
    bsz, t, ch = u.shape
    n_state = h0r.shape[-1]
    ln = min(t, 256)
    const = lambda shape: pl.BlockSpec(shape, lambda b, i: (0, 0))
    state = pl.BlockSpec((1, 1, n_state), lambda b, i: (b, 0, 0))
    return pl.pallas_call(
        _s5_kernel,
        grid=(bsz, t // ln),
        in_specs=[pl.BlockSpec((1, ln, ch), lambda b, i: (b, i, 0)), state, state,
                  const((ch, n_state)), const((ch, n_state)),
                  const((n_state, ch)), const((n_state, ch)),
                  const((SUBLANES, n_state)), const((SUBLANES, n_state)),
                  const((SUBLANES, n_state)), const((SUBLANES, n_state)),
                  const((1, ch)), const((ch, ch))],
        out_specs=[pl.BlockSpec((1, ln, ch), lambda b, i: (b, i, 0)), state, state],
        out_shape=[jax.ShapeDtypeStruct((bsz, t, ch), BF16),
                   jax.ShapeDtypeStruct((bsz, 1, n_state), F32),
                   jax.ShapeDtypeStruct((bsz, 1, n_state), F32)],
        scratch_shapes=[pltpu.VMEM((ln, n_state), F32), pltpu.VMEM((ln, n_state), F32),
                        pltpu.VMEM((1, n_state), F32), pltpu.VMEM((1, n_state), F32)],
        compiler_params=_cparams(("parallel", "arbitrary")),
        name="s5_mixer",
    )(u, h0r, h0i, prm["bre"], prm["bim"], prm["cre"], prm["cim"],
      prm["alr"], prm["ali"], prm["acr"], prm["aci"], prm["d"], prm["wglu"])


def _s5_params(a_re, a_im, log_dt, b_re, b_im, c_re, c_im, d, w_glu):
    g, p = a_re.shape
    c = b_re.shape[-1]
    dt = jnp.exp(log_dt)[:, None]
    mag = jnp.exp(a_re * dt)
    ar = mag * jnp.cos(a_im * dt)
    ai = mag * jnp.sin(a_im * dt)
    den = a_re * a_re + a_im * a_im
    fr = ((ar - 1.0) * a_re + ai * a_im) / den
    fi = (ai * a_re - (ar - 1.0) * a_im) / den
    bbr = fr[..., None] * b_re - fi[..., None] * b_im
    bbi = fr[..., None] * b_im + fi[..., None] * b_re
    eye = jnp.eye(g, dtype=F32)

    def in_proj(x):
        return jnp.einsum("gpc,gh->gchp", x, eye).reshape(g * c, g * p).astype(BF16)

    def out_proj(x):
        return jnp.einsum("gcp,gh->gphc", x, eye).reshape(g * p, g * c).astype(BF16)

    def powers(n_list):
        rs, is_ = [], []
        for n in n_list:
            m = jnp.exp(a_re * dt * n)
            rs.append((m * jnp.cos(a_im * dt * n)).reshape(1, g * p))
            is_.append((m * jnp.sin(a_im * dt * n)).reshape(1, g * p))
        pad = SUBLANES - len(n_list)
        if pad:
            rs += [jnp.zeros((pad, g * p), F32)]
            is_ += [jnp.zeros((pad, g * p), F32)]
        return jnp.concatenate(rs, axis=0), jnp.concatenate(is_, axis=0)

    alr, ali = powers([1, 2, 4])
    acr, aci = powers(list(range(1, SUBLANES + 1)))
    return {"bre": in_proj(bbr), "bim": in_proj(bbi), "cre": out_proj(c_re), "cim": out_proj(c_im),
            "alr": alr, "ali": ali, "acr": acr, "aci": aci,
            "d": d.reshape(1, g * c), "wglu": w_glu.astype(BF16)}


def _xattn_kernel(hn_ref, wq_ref, mkv_ref, o_ref, *, n_h):
    qx = _dot(hn_ref[0], wq_ref[...])
    mkv = mkv_ref[0]
    hw = n_h * DH_X
    scale = DH_X ** -0.5
    for h in range(n_h):
        qh = qx[:, h * DH_X:(h + 1) * DH_X].astype(BF16)
        kh = mkv[:, h * DH_X:(h + 1) * DH_X].astype(BF16)
        vh = mkv[:, hw + h * DH_X:hw + (h + 1) * DH_X].astype(BF16)
        s = _dot_nt(qh, kh) * scale
        m = jnp.max(s, axis=-1, keepdims=True)
        p = jnp.exp(s - m)
        l = jnp.sum(p, axis=-1, keepdims=True)
        o = _dot(p.astype(BF16), vh) / l
        o_ref[0, :, h * DH_X:(h + 1) * DH_X] = o.astype(o_ref.dtype)


def _xattn(hn, w_qx, mem_kv, n_h):
    bsz, t, d = hn.shape
    n_mem = mem_kv.shape[1]
    hw = n_h * DH_X
    tq = min(t, 256)
    return pl.pallas_call(
        functools.partial(_xattn_kernel, n_h=n_h),
        grid=(bsz, t // tq),
        in_specs=[pl.BlockSpec((1, tq, d), lambda b, i: (b, i, 0)),
                  pl.BlockSpec((d, hw), lambda b, i: (0, 0)),
                  pl.BlockSpec((1, n_mem, 2 * hw), lambda b, i: (b, 0, 0))],
        out_specs=pl.BlockSpec((1, tq, hw), lambda b, i: (b, i, 0)),
        out_shape=jax.ShapeDtypeStruct((bsz, t, hw), BF16),
        compiler_params=_cparams(("parallel", "parallel")),
        name="xattn",
    )(hn, w_qx, mem_kv)


def _block_diag_q(q, n_vh, scale):
    bsz, n_q, _ = q.shape
    qh = (q.astype(F32) * scale).reshape(bsz, n_q, n_vh, DH)
    eye = jnp.eye(n_vh, dtype=F32)
    out = jnp.einsum("bqhd,hg->bhqgd", qh, eye)
    return out.reshape(bsz, n_vh * n_q, n_vh * DH).astype(BF16)


def _rows_from_heads(x, n_q):
    return x.reshape(x.shape[0] * n_q, x.shape[2])


def kernel(x_prompt, x_sample, mem_prompt, cache_a_kv, cache_a_idx_k, cache_c_kv, cache_d_kv,
           cache_d_logf, cache_mem_kv, state_s5_re, state_s5_im, state_ffn_conv, page_table,
           rel_bias, g_mix_pre, w_in, s5_a_re, s5_a_im, s5_log_dt, s5_b_re, s5_b_im, s5_c_re,
           s5_c_im, s5_d, s5_w_glu, lam_q1, lam_k1, lam_q2, lam_k2, diff_sub_g, fox_b_f, w_out,
           g_mix_post, g_x_pre, w_qx, w_kvx, w_ox, g_x_post, g_ffn_pre, w_gate, w_up, conv_w,
           conv_b, w_down, g_ffn_post):
    bp, t, d = x_prompt.shape
    bs, ts, _ = x_sample.shape
    depth = w_in.shape[0]
    n_pool = cache_a_kv.shape[1]
    n_pages = page_table.shape[1]
    past = n_pages * PAGE
    h_a = cache_a_kv.shape[4]
    h_c = cache_c_kv.shape[4]
    h_d = cache_d_kv.shape[4]
    h_x = cache_mem_kv.shape[4]
    n_mem = cache_mem_kv.shape[2]
    s5_g, s5_p = s5_a_re.shape[1:]
    s5_ch = s5_g * S5_GROUP
    n_state = s5_g * s5_p
    d_ff = w_gate.shape[2]
    wa, wc, wd = h_a * DH, h_c * 2 * DH, h_d * DH
    wqi = H_IDX * D_IDX
    assert wa == wc == wd == s5_ch and ts == SUBLANES and past % (PAGES_PER_STEP * PAGE) == 0
    pp = PAGES_PER_STEP
    ns = n_pages // pp
    topk_p = min(TOPK_MAX, t // 4)
    topk_s = min(TOPK_MAX, (past + ts) // 4)

    sizes = (wa, wa, wa, wqi, D_IDX, H_IDX, s5_ch, wc, wc, wc, wd, wd, wd, h_d)
    offs = np.concatenate([[0], np.cumsum(sizes)]).tolist()
    (o_aq, o_ak, o_av, o_aqi, o_aki, o_awi, o_bu, o_cq, o_ck, o_cv, o_dq, o_dk, o_dv, o_df) = offs[:-1]
    col_wi, col_df = D_IDX, D_IDX + H_IDX

    tq = min(t, ATT_BLK)
    bias_p = jnp.stack([_bias_tile(rel_bias, tq, tq, tq), _bias_tile(rel_bias, 0, tq, tq)], axis=1)
    bias_s_prev = _bias_tile(rel_bias, PAGE, ts, PAGE)
    bias_s_new = _bias_tile(rel_bias, 0, ts, PAGE)
    far = rel_bias[NUM_BUCKETS - 1]
    causal_new = jnp.where(jnp.arange(PAGE)[None, :] <= jnp.arange(ts)[:, None], 0.0, NEG).astype(F32)

    def sample_tiles(h0, n_h, rep):
        hs = [h0 + i // rep for i in range(n_h * rep)]
        farc = jnp.stack([jnp.full((ts, 1), 1.0, F32) * far[h] for h in hs]).reshape(-1, 1)
        prev = jnp.stack([bias_s_prev[h] for h in hs]).reshape(-1, PAGE) - farc
        new = jnp.stack([bias_s_new[h] + causal_new for h in hs]).reshape(-1, PAGE) - farc
        return jnp.broadcast_to(farc[None], (bs,) + farc.shape), prev, new

    rowc_a, bprev_a, bnew_a = sample_tiles(0, h_a, 1)
    rowc_c, bprev_c, bnew_c = sample_tiles(h_a, h_c, 2)
    bprev_d = jnp.zeros((h_d * ts, PAGE), F32)
    bnew_d = jnp.tile(causal_new, (h_d, 1))

    pool_a = cache_a_kv.reshape(depth, n_pool, PAGE, 2 * wa)
    pool_c = cache_c_kv.reshape(depth, n_pool, PAGE, 2 * wc)
    pool_d = cache_d_kv.reshape(depth, n_pool, PAGE, 2 * wd)

    xp = x_prompt.reshape(bp * t, d)
    xs = x_sample.reshape(bs * ts, d)
    mem_bf = mem_prompt.reshape(bp * n_mem, d).astype(BF16)
    hn_p = _norm_cast(xp, g_mix_pre[0])
    hn_s = _norm_cast(xs, g_mix_pre[0])

    outs = {k: [] for k in ("a_kv_p", "a_kv_s", "a_ki_p", "a_ki_s", "c_kv_p", "c_kv_s", "d_kv_p",
                            "d_kv_s", "lf_p", "lf_s", "mem_kv", "s5r_p", "s5r_s", "s5i_p", "s5i_s",
                            "cs_p", "cs_s")}

    for l in range(depth):
        wl = w_in[l]
        w_akv = wl[:, o_ak:o_aqi].astype(BF16)
        w_ckv = wl[:, o_ck:o_dq].astype(BF16)
        w_dkv = wl[:, o_dk:o_df].astype(BF16)
        w_q = jnp.concatenate([wl[:, o_aqi:o_aki], wl[:, o_aq:o_ak], wl[:, o_cq:o_ck],
                               wl[:, o_dq:o_dk]], axis=1).astype(BF16)
        w_bu = wl[:, o_bu:o_cq].astype(BF16)
        w_small = jnp.concatenate([wl[:, o_aki:o_bu], wl[:, o_df:],
                                   jnp.zeros((d, LANES - D_IDX - H_IDX - h_d), F32)], axis=1).astype(BF16)
        qblk_a, qblk_c, qblk_d = wqi // wa, wqi // wa + 1, wqi // wa + 2
        lam_init = 0.8 - 0.6 * math.exp(-0.3 * l)
        lam = (jnp.exp(jnp.sum(lam_q1[l] * lam_k1[l])) - jnp.exp(jnp.sum(lam_q2[l] * lam_k2[l]))
               + lam_init).reshape(1)
        bvec = jnp.zeros((1, LANES), F32).at[0, col_df:col_df + h_d].set(fox_b_f[l])
        s5p = _s5_params(s5_a_re[l], s5_a_im[l], s5_log_dt[l], s5_b_re[l], s5_b_im[l],
                         s5_c_re[l], s5_c_im[l], s5_d[l], s5_w_glu[l])
        w_out_b = w_out[l].astype(BF16)
        w_qx_b = w_qx[l].astype(BF16)
        w_ox_b = w_ox[l].astype(BF16)
        w_gate_b = w_gate[l].astype(BF16)
        w_up_b = w_up[l].astype(BF16)
        w_down_b = w_down[l].astype(BF16)
        g_next = g_mix_pre[l + 1] if l + 1 < depth else None

        mkv_p = _matmul(mem_bf, w_kvx[l].astype(BF16), F32).reshape(bp, n_mem, 2 * h_x * DH_X)
        outs["mem_kv"].append(mkv_p)

        def mixer_inputs(hn):
            return (_matmul(hn, w_akv, F32), _matmul(hn, w_ckv, F32), _matmul(hn, w_dkv, F32),
                    _matmul(hn, w_q, BF16), _matmul(hn, w_bu, F32), _matmul(hn, w_small, F32))

        a_kv, c_kv, d_kv, q_all, b_u, small = mixer_inputs(hn_p)
        a_kv3 = a_kv.reshape(bp, t, 2 * wa)
        c_kv3 = c_kv.reshape(bp, t, 2 * wc)
        d_kv3 = d_kv.reshape(bp, t, 2 * wd)
        q3 = q_all.reshape(bp, t, -1)
        small3 = small.reshape(bp, t, LANES)
        lf3, c3 = _logsig_cumsum(small3, bvec)
        negck = -jnp.swapaxes(c3[:, :, col_df:col_df + h_d], 1, 2)
        o_a = _dsa_prompt(q3, qblk_a, small3, a_kv3, bias_p[:h_a], far[:h_a], h_a, topk_p, col_wi)
        o_b, s5r, s5i = _s5_mixer(b_u.reshape(bp, t, s5_ch), jnp.zeros((bp, 1, n_state), F32),
                                  jnp.zeros((bp, 1, n_state), F32), s5p)
        o_c = _diff_prompt(q3, qblk_c, c_kv3, bias_p[h_a:], far[h_a:], lam, diff_sub_g[l], h_c,
                           1.0 - lam_init)
        o_d = _fox_prompt(q3, qblk_d, d_kv3, c3, negck, h_d, col_df)
        mix = [o.reshape(bp * t, -1) for o in (o_a, o_b, o_c, o_d)]
        xp, hn = _proj_res(mix, w_out_b, xp, g_mix_post[l], g_x_pre[l])
        ox = _xattn(hn.reshape(bp, t, d), w_qx_b, mkv_p, h_x)
        xp, hn = _proj_res([ox.reshape(bp * t, -1)], w_ox_b, xp, g_x_post[l], g_ffn_pre[l])
        hid, cs = _ffn_hidden(hn, w_gate_b, w_up_b, conv_w[l], conv_b[l],
                              jnp.zeros((bp, CONV_W - 1, d_ff), F32), t)
        xp, hn_p = _proj_res_ktiled(hid, w_down_b, xp, g_ffn_post[l], g_next)
        outs["a_kv_p"].append(a_kv3)
        outs["a_ki_p"].append(small3[:, :, 0:D_IDX])
        outs["c_kv_p"].append(c_kv3)
        outs["d_kv_p"].append(d_kv3)
        outs["lf_p"].append(lf3[:, :, col_df:col_df + h_d])
        outs["s5r_p"].append(s5r)
        outs["s5i_p"].append(s5i)
        outs["cs_p"].append(cs)

        a_kv, c_kv, d_kv, q_all, b_u, small = mixer_inputs(hn_s)
        a_kv3 = a_kv.reshape(bs, ts, 2 * wa)
        c_kv3 = c_kv.reshape(bs, ts, 2 * wc)
        d_kv3 = d_kv.reshape(bs, ts, 2 * wd)
        q3 = q_all.reshape(bs, ts, -1)
        small3 = small.reshape(bs, ts, LANES)
        lf3, c3 = _logsig_cumsum(small3, bvec)
        pad_new = lambda x: jnp.pad(x, ((0, 0), (0, PAGE - ts), (0, 0)))

        qi_rows = jnp.swapaxes(q3[:, :, 0:wqi].reshape(bs, ts, H_IDX, D_IDX), 1, 2)
        qi_rows = qi_rows.reshape(bs, H_IDX * ts, D_IDX)
        w_rows = jnp.swapaxes(small3[:, :, col_wi:col_wi + H_IDX], 1, 2).reshape(bs, H_IDX * ts, 1)
        w_rows = w_rows * (1.0 / (math.sqrt(H_IDX) * math.sqrt(D_IDX)))
        amask = _dsa_index_sample(qi_rows, w_rows, cache_a_idx_k, l, page_table,
                                  pad_new(small3[:, :, 0:D_IDX]), topk_s, ts)
        o_a = _paged_attn("dsa", _block_diag_q(q3[:, :, wqi:wqi + wa], h_a, 0.125), pool_a, l,
                          page_table, pad_new(a_kv3), amask, rowc_a, bprev_a, bnew_a, ts, h_a, DH)
        o_b, s5r, s5i = _s5_mixer(b_u.reshape(bs, ts, s5_ch),
                                  state_s5_re[l].reshape(bs, 1, n_state),
                                  state_s5_im[l].reshape(bs, 1, n_state), s5p)
        o_c = _paged_attn("diff", _block_diag_q(q3[:, :, wqi + wa:wqi + wa + wc], 2 * h_c, 0.125),
                          pool_c, l, page_table, pad_new(c_kv3), None, rowc_c, bprev_c, bnew_c,
                          ts, h_c, 2 * DH, lam=lam, gsub=diff_sub_g[l], out_scale=1.0 - lam_init)
        c_new = c3[:, :, col_df:col_df + h_d]
        sfx = _page_suffix(cache_d_logf, l, page_table)
        nck_past = jnp.swapaxes(sfx, 1, 2).reshape(bs, h_d, ns, pp * PAGE).transpose(0, 2, 1, 3)
        nck_new = jnp.pad(-jnp.swapaxes(c_new, 1, 2), ((0, 0), (0, 0), (0, pp * PAGE - ts)))
        nck = jnp.concatenate([nck_past, nck_new[:, None]], axis=1)
        rowc_d = jnp.swapaxes(c_new, 1, 2).reshape(bs, h_d * ts, 1)
        o_d = _paged_attn("fox", _block_diag_q(q3[:, :, wqi + wa + wc:], h_d, 0.125), pool_d, l,
                          page_table, pad_new(d_kv3), nck, rowc_d, bprev_d, bnew_d, ts, h_d, DH)
        mix = [o.reshape(bs * ts, -1) for o in (o_a, o_b, o_c, o_d)]
        xs, hn = _proj_res(mix, w_out_b, xs, g_mix_post[l], g_x_pre[l])
        mkv_s = cache_mem_kv[l].reshape(bs, n_mem, 2 * h_x * DH_X)
        ox = _xattn(hn.reshape(bs, ts, d), w_qx_b, mkv_s, h_x)
        xs, hn = _proj_res([ox.reshape(bs * ts, -1)], w_ox_b, xs, g_x_post[l], g_ffn_pre[l])
        hid, cs = _ffn_hidden(hn, w_gate_b, w_up_b, conv_w[l], conv_b[l], state_ffn_conv[l], ts)
        xs, hn_s = _proj_res_ktiled(hid, w_down_b, xs, g_ffn_post[l], g_next)
        outs["a_kv_s"].append(a_kv3)
        outs["a_ki_s"].append(small3[:, :, 0:D_IDX])
        outs["c_kv_s"].append(c_kv3)
        outs["d_kv_s"].append(d_kv3)
        outs["lf_s"].append(lf3[:, :, col_df:col_df + h_d])
        outs["s5r_s"].append(s5r)
        outs["s5i_s"].append(s5i)
        outs["cs_s"].append(cs)

    st = {k: jnp.stack(v) for k, v in outs.items()}
    return (xp.reshape(bp, t, d), xs.reshape(bs, ts, d),
            st["a_kv_p"].reshape(depth, bp, t, 2, h_a, DH),
            st["a_kv_s"].reshape(depth, bs, ts, 2, h_a, DH),
            st["a_ki_p"], st["a_ki_s"],
            st["c_kv_p"].reshape(depth, bp, t, 2, h_c, 2 * DH),
            st["c_kv_s"].reshape(depth, bs, ts, 2, h_c, 2 * DH),
            st["d_kv_p"].reshape(depth, bp, t, 2, h_d, DH),
            st["d_kv_s"].reshape(depth, bs, ts, 2, h_d, DH),
            st["lf_p"], st["lf_s"],
            st["mem_kv"].reshape(depth, bp, n_mem, 2, h_x, DH_X),
            st["s5r_p"].reshape(depth, bp, s5_g, s5_p), st["s5r_s"].reshape(depth, bs, s5_g, s5_p),
            st["s5i_p"].reshape(depth, bp, s5_g, s5_p), st["s5i_s"].reshape(depth, bs, s5_g, s5_p),
            st["cs_p"], st["cs_s"])
```

```python
import functools
import math

import numpy as np
import jax
import jax.numpy as jnp
from jax import lax
from jax.experimental import pallas as pl
from jax.experimental.pallas import tpu as pltpu

F32 = jnp.float32
BF16 = jnp.bfloat16
I32 = jnp.int32

EPS = 1e-6
NEG = -1e30
M_INIT = -3e38
INT_MIN = -2 ** 31
BIG_IDX = 2 ** 30

DH = 64
H_IDX = 16
D_IDX = 64
TOPK_MAX = 256
S5_GROUP = 16
S5_P = 64
DH_X = 128
NUM_BUCKETS = 32
MAX_DISTANCE = 128
CONV_W = 3
PAGE = 128

LANES = 128
SUBLANES = 8
VMEM_LIMIT = 52 * 1024 * 1024

ATT_BLK = 256
PAGES_PER_STEP = 8


def _t5_thresholds():
    exact = NUM_BUCKETS // 2
    n = np.arange(exact, MAX_DISTANCE + 1).astype(np.float64)
    large = exact + np.floor(np.log(n / exact) / math.log(MAX_DISTANCE / exact)
                             * (NUM_BUCKETS - exact)).astype(np.int64)
    bucket = np.minimum(large, NUM_BUCKETS - 1)
    return [int(n[np.argmax(bucket >= b)]) for b in range(exact + 1, NUM_BUCKETS)]


_T5_THR = _t5_thresholds()


def _cparams(sem):
    return pltpu.CompilerParams(dimension_semantics=sem, vmem_limit_bytes=VMEM_LIMIT)


def _dot(a, b):
    return jnp.dot(a, b, preferred_element_type=F32)


def _dot_nt(a, b):
    return lax.dot_general(a, b, (((1,), (1,)), ((), ())), preferred_element_type=F32)


def _rms(x, g):
    y = x * lax.rsqrt(jnp.mean(x * x, axis=-1, keepdims=True) + EPS)
    return y * g


def _gelu(x):
    c = math.sqrt(2.0 / math.pi)
    return 0.5 * x * (1.0 + jnp.tanh(c * (x + 0.044715 * (x * x * x))))


def _sigmoid(x):
    return 1.0 / (1.0 + jnp.exp(-x))


def _log_sigmoid(x):
    return jnp.minimum(x, 0.0) - jnp.log(1.0 + jnp.exp(-jnp.abs(x)))


def _norm_cast_kernel(x_ref, g_ref, o_ref):
    o_ref[...] = _rms(x_ref[...], g_ref[...]).astype(o_ref.dtype)


def _norm_cast(x, g):
    m, d = x.shape
    tm = min(m, 512)
    return pl.pallas_call(
        _norm_cast_kernel,
        grid=(m // tm,),
        in_specs=[pl.BlockSpec((tm, d), lambda i: (i, 0)),
                  pl.BlockSpec((1, d), lambda i: (0, 0))],
        out_specs=pl.BlockSpec((tm, d), lambda i: (i, 0)),
        out_shape=jax.ShapeDtypeStruct((m, d), BF16),
        compiler_params=_cparams(("parallel",)),
        name="norm_cast",
    )(x, g.reshape(1, d))


def _mm_kernel(a_ref, w_ref, o_ref):
    o_ref[...] = _dot(a_ref[...], w_ref[...]).astype(o_ref.dtype)


def _matmul(a, w, out_dtype):
    m, k = a.shape
    n = w.shape[1]
    tm = min(m, 1024)
    tn = min(n, 512)
    return pl.pallas_call(
        _mm_kernel,
        grid=(m // tm, n // tn),
        in_specs=[pl.BlockSpec((tm, k), lambda i, j: (i, 0)),
                  pl.BlockSpec((k, tn), lambda i, j: (0, j))],
        out_specs=pl.BlockSpec((tm, tn), lambda i, j: (i, j)),
        out_shape=jax.ShapeDtypeStruct((m, n), out_dtype),
        compiler_params=_cparams(("parallel", "parallel")),
        name="matmul",
    )(a, w)


def _proj_res_kernel(*refs, n_in, emit_next):
    a_refs = refs[:n_in]
    w_ref, x_ref, gp_ref, gn_ref = refs[n_in:n_in + 4]
    outs = refs[n_in + 4:]
    off = 0
    acc = None
    for a_ref in a_refs:
        kk = a_ref.shape[1]
        part = _dot(a_ref[...], w_ref[off:off + kk, :])
        acc = part if acc is None else acc + part
        off += kk
    xn = x_ref[...] + _rms(acc, gp_ref[...])
    outs[0][...] = xn
    if emit_next:
        outs[1][...] = _rms(xn, gn_ref[...]).astype(BF16)


def _proj_res(a_list, w, x, g_post, g_next):
    m, d = x.shape
    k = w.shape[0]
    tm = min(m, 256)
    emit_next = g_next is not None
    gn = g_next if emit_next else g_post
    in_specs = [pl.BlockSpec((tm, a.shape[1]), lambda i: (i, 0)) for a in a_list]
    in_specs += [pl.BlockSpec((k, d), lambda i: (0, 0)),
                 pl.BlockSpec((tm, d), lambda i: (i, 0)),
                 pl.BlockSpec((1, d), lambda i: (0, 0)),
                 pl.BlockSpec((1, d), lambda i: (0, 0))]
    out_specs = [pl.BlockSpec((tm, d), lambda i: (i, 0))]
    out_shape = [jax.ShapeDtypeStruct((m, d), F32)]
    if emit_next:
        out_specs.append(pl.BlockSpec((tm, d), lambda i: (i, 0)))
        out_shape.append(jax.ShapeDtypeStruct((m, d), BF16))
    res = pl.pallas_call(
        functools.partial(_proj_res_kernel, n_in=len(a_list), emit_next=emit_next),
        grid=(m // tm,),
        in_specs=in_specs,
        out_specs=out_specs,
        out_shape=out_shape,
        compiler_params=_cparams(("parallel",)),
        name="proj_res",
    )(*a_list, w, x, g_post.reshape(1, d), gn.reshape(1, d))
    return res[0], (res[1] if emit_next else None)


def _proj_res_kt_kernel(a_ref, w_ref, x_ref, gp_ref, gn_ref, *rest, emit_next):
    if emit_next:
        xo_ref, hn_ref, acc_ref = rest
    else:
        xo_ref, acc_ref = rest
    kk = pl.program_id(1)

    @pl.when(kk == 0)
    def _():
        acc_ref[...] = jnp.zeros_like(acc_ref)

    acc_ref[...] += _dot(a_ref[...], w_ref[...])

    @pl.when(kk == pl.num_programs(1) - 1)
    def _():
        xn = x_ref[...] + _rms(acc_ref[...], gp_ref[...])
        xo_ref[...] = xn
        if emit_next:
            hn_ref[...] = _rms(xn, gn_ref[...]).astype(BF16)


def _proj_res_ktiled(a, w, x, g_post, g_next):
    m, d = x.shape
    k = w.shape[0]
    tm = min(m, 512)
    tk = 512
    emit_next = g_next is not None
    gn = g_next if emit_next else g_post
    out_specs = [pl.BlockSpec((tm, d), lambda i, j: (i, 0))]
    out_shape = [jax.ShapeDtypeStruct((m, d), F32)]
    if emit_next:
        out_specs.append(pl.BlockSpec((tm, d), lambda i, j: (i, 0)))
        out_shape.append(jax.ShapeDtypeStruct((m, d), BF16))
    res = pl.pallas_call(
        functools.partial(_proj_res_kt_kernel, emit_next=emit_next),
        grid=(m // tm, k // tk),
        in_specs=[pl.BlockSpec((tm, tk), lambda i, j: (i, j)),
                  pl.BlockSpec((tk, d), lambda i, j: (j, 0)),
                  pl.BlockSpec((tm, d), lambda i, j: (i, 0)),
                  pl.BlockSpec((1, d), lambda i, j: (0, 0)),
                  pl.BlockSpec((1, d), lambda i, j: (0, 0))],
        out_specs=out_specs,
        out_shape=out_shape,
        scratch_shapes=[pltpu.VMEM((tm, d), F32)],
        compiler_params=_cparams(("parallel", "arbitrary")),
        name="proj_res_ktiled",
    )(a, w, x, g_post.reshape(1, d), gn.reshape(1, d))
    return res[0], (res[1] if emit_next else None)


def _ffn_hidden_kernel(hn_ref, wg_ref, wu_ref, cw_ref, cb_ref, hb1_ref, hb2_ref,
                       h_ref, cs_ref, *, seq_len, n_seq):
    hn = hn_ref[...]
    g = _dot(hn, wg_ref[...])
    u = _dot(hn, wu_ref[...])
    w0 = cw_ref[0:1, :]
    w1 = cw_ref[1:2, :]
    w2 = cw_ref[2:3, :]
    cb = cb_ref[...]
    if seq_len > SUBLANES:
        gc = cb + w0 * pltpu.roll(g, 2, 0) + w1 * pltpu.roll(g, 1, 0) + w2 * g
        h_ref[...] = (_gelu(gc) * u).astype(h_ref.dtype)
    row = lax.broadcasted_iota(I32, (SUBLANES, 1), 0)
    for s in range(n_seq):
        r0 = s * seq_len
        g8 = g[r0:r0 + SUBLANES, :]
        p1 = jnp.where(row < 1, hb1_ref[s], pltpu.roll(g8, 1, 0))
        p2 = jnp.where(row < 2, hb2_ref[s], pltpu.roll(g8, 2, 0))
        gc8 = cb + w0 * p2 + w1 * p1 + w2 * g8
        h_ref[r0:r0 + SUBLANES, :] = (_gelu(gc8) * u[r0:r0 + SUBLANES, :]).astype(h_ref.dtype)
        cs_ref[s] = g[r0 + seq_len - 2:r0 + seq_len, :]


def _ffn_hidden(hn, w_gate, w_up, conv_w, conv_b, buf, seq_len):
    m, d = hn.shape
    f = w_gate.shape[1]
    bsz = m // seq_len
    n_seq = 1 if seq_len > SUBLANES else bsz
    tt = seq_len * n_seq
    tf = 256
    zeros = jnp.zeros((bsz, SUBLANES - 2, f), F32)
    hb1 = jnp.concatenate([buf[:, 1:2], jnp.zeros((bsz, 1, f), F32), zeros], axis=1)
    hb2 = jnp.concatenate([buf, zeros], axis=1)
    h, cs = pl.pallas_call(
        functools.partial(_ffn_hidden_kernel, seq_len=seq_len, n_seq=n_seq),
        grid=(m // tt, f // tf),
        in_specs=[pl.BlockSpec((tt, d), lambda i, j: (i, 0)),
                  pl.BlockSpec((d, tf), lambda i, j: (0, j)),
                  pl.BlockSpec((d, tf), lambda i, j: (0, j)),
                  pl.BlockSpec((CONV_W, tf), lambda i, j: (0, j)),
                  pl.BlockSpec((1, tf), lambda i, j: (0, j)),
                  pl.BlockSpec((n_seq, SUBLANES, tf), lambda i, j: (i, 0, j)),
                  pl.BlockSpec((n_seq, SUBLANES, tf), lambda i, j: (i, 0, j))],
        out_specs=[pl.BlockSpec((tt, tf), lambda i, j: (i, j)),
                   pl.BlockSpec((n_seq, CONV_W - 1, tf), lambda i, j: (i, 0, j))],
        out_shape=[jax.ShapeDtypeStruct((m, f), BF16),
                   jax.ShapeDtypeStruct((bsz, CONV_W - 1, f), F32)],
        compiler_params=_cparams(("parallel", "parallel")),
        name="ffn_hidden",
    )(hn, w_gate, w_up, conv_w, conv_b.reshape(1, f), hb1, hb2)
    return h, cs


def _bias_tile_kernel(tab_ref, o_ref, *, off):
    nh, r, c = o_ref.shape
    i = lax.broadcasted_iota(I32, (r, c), 0)
    j = lax.broadcasted_iota(I32, (r, c), 1)
    n = jnp.maximum(off + i - j, 0)
    large = jnp.full((r, c), NUM_BUCKETS // 2, I32)
    for thr in _T5_THR:
        large = large + jnp.where(n >= thr, 1, 0)
    bucket = jnp.where(n < NUM_BUCKETS // 2, n, large)

    def head(h, carry):
        val = jnp.full((r, c), tab_ref[NUM_BUCKETS - 1, h], F32)
        for b in range(NUM_BUCKETS - 2, -1, -1):
            val = jnp.where(bucket == b, tab_ref[b, h], val)
        o_ref[h] = val
        return carry
    lax.fori_loop(0, nh, head, 0)


def _bias_tile(rel_bias, off, r, c):
    nh = rel_bias.shape[1]
    return pl.pallas_call(
        functools.partial(_bias_tile_kernel, off=off),
        in_specs=[pl.BlockSpec(memory_space=pltpu.SMEM)],
        out_specs=pl.BlockSpec(memory_space=pltpu.VMEM),
        out_shape=jax.ShapeDtypeStruct((nh, r, c), F32),
        compiler_params=pltpu.CompilerParams(vmem_limit_bytes=VMEM_LIMIT),
        name="t5_bias_tile",
    )(rel_bias)


def _logsig_cumsum_kernel(x_ref, b_ref, lf_ref, c_ref):
    t = x_ref.shape[1]
    lf = _log_sigmoid(x_ref[0] + b_ref[...])
    lf_ref[0] = lf
    row = lax.broadcasted_iota(I32, (t, 1), 0)
    c = lf
    s = 1
    while s < t:
        c = c + jnp.where(row >= s, pltpu.roll(c, s, 0), 0.0)
        s *= 2
    c_ref[0] = c


def _logsig_cumsum(x, bvec):
    bsz, t, w = x.shape
    return pl.pallas_call(
        _logsig_cumsum_kernel,
        grid=(bsz,),
        in_specs=[pl.BlockSpec((1, t, w), lambda b: (b, 0, 0)),
                  pl.BlockSpec((1, w), lambda b: (0, 0))],
        out_specs=[pl.BlockSpec((1, t, w), lambda b: (b, 0, 0)),
                   pl.BlockSpec((1, t, w), lambda b: (b, 0, 0))],
        out_shape=[jax.ShapeDtypeStruct((bsz, t, w), F32),
                   jax.ShapeDtypeStruct((bsz, t, w), F32)],
        compiler_params=_cparams(("parallel",)),
        name="logsig_cumsum",
    )(x, bvec)


def _page_suffix_kernel(pt_ref, *refs):
    del pt_ref
    pp = PAGES_PER_STEP
    page_refs = refs[:pp]
    o_ref, carry_ref = refs[pp:]
    s = pl.program_id(1)

    @pl.when(s == 0)
    def _():
        carry_ref[...] = jnp.zeros_like(carry_ref)

    x0 = jnp.concatenate([page_refs[p][0, 0] for p in range(pp)], axis=1)
    w = x0.shape[1]
    lane = lax.broadcasted_iota(I32, (1, w), 1) % PAGE
    x = x0
    sh = 1
    while sh < PAGE:
        x = x + jnp.where(lane + sh < PAGE, pltpu.roll(x, w - sh, 1), 0.0)
        sh *= 2
    run = carry_ref[...]
    pieces = [None] * pp
    for p in reversed(range(pp)):
        sl = slice(p * PAGE, (p + 1) * PAGE)
        pieces[p] = (x[:, sl] - x0[:, sl]) + run
        run = run + jnp.sum(x0[:, sl], axis=1, keepdims=True)
    o_ref[0, 0] = jnp.concatenate(pieces, axis=1)
    carry_ref[...] = run


def _page_suffix(lf_pool_t, layer, page_table):
    n_h = lf_pool_t.shape[2]
    bsz, n_pages = page_table.shape
    pp = PAGES_PER_STEP
    ns = n_pages // pp

    def page_map(p):
        return lambda b, s, pt: (layer, pt[b, (ns - 1 - s) * pp + p], 0, 0)

    return pl.pallas_call(
        _page_suffix_kernel,
        grid_spec=pltpu.PrefetchScalarGridSpec(
            num_scalar_prefetch=1,
            grid=(bsz, ns),
            in_specs=[pl.BlockSpec((1, 1, n_h, PAGE), page_map(p)) for p in range(pp)],
            out_specs=pl.BlockSpec((1, 1, n_h, pp * PAGE), lambda b, s, pt: (b, ns - 1 - s, 0, 0)),
            scratch_shapes=[pltpu.VMEM((n_h, 1), F32)]),
        out_shape=jax.ShapeDtypeStruct((bsz, ns, n_h, pp * PAGE), F32),
        compiler_params=_cparams(("parallel", "arbitrary")),
        name="page_suffix",
    )(page_table, *([lf_pool_t] * pp))


def _flash_update(s, v, m_ref, l_ref, acc_ref, idx):
    m_old = m_ref[idx]
    m_new = jnp.maximum(m_old, jnp.max(s, axis=-1, keepdims=True))
    alpha = jnp.exp(m_old - m_new)
    p = jnp.exp(s - m_new)
    l_ref[idx] = alpha * l_ref[idx] + jnp.sum(p, axis=-1, keepdims=True)
    acc_ref[idx] = alpha * acc_ref[idx] + _dot(p.astype(BF16), v)
    m_ref[idx] = m_new


def _init_flash(m_ref, l_ref, acc_ref):
    m_ref[...] = jnp.full(m_ref.shape, M_INIT, F32)
    l_ref[...] = jnp.zeros_like(l_ref)
    acc_ref[...] = jnp.zeros_like(acc_ref)


def _sortable(x):
    bits = lax.bitcast_convert_type(x + 0.0, I32)
    return jnp.where(bits < 0, bits ^ 0x7FFFFFFF, bits)


def _lane_fold(x):
    n = x.shape[1] // LANES
    acc = x[:, 0:LANES]
    for i in range(1, n):
        acc = acc + x[:, i * LANES:(i + 1) * LANES]
    return acc


def _topk_select(count_fn, rows, k, n_idx_bits):
    kf = float(k)

    def bit_body(i, ans):
        cand = ans + lax.shift_left(jnp.int32(1), 31 - i)
        cnt = count_fn(lambda key, idx: key >= cand)
        return jnp.where(cnt >= kf, cand, ans)

    thr = lax.fori_loop(0, 32, bit_body, jnp.full((rows, 1), INT_MIN, I32))
    n_ge = count_fn(lambda key, idx: key >= thr)
    n_gt = count_fn(lambda key, idx: key > thr)
    need = kf - n_gt

    def tie_search():
        def idx_body(i, c):
            cand = c + lax.shift_left(jnp.int32(1), n_idx_bits - 1 - i)
            cnt = count_fn(lambda key, idx: (key == thr) & (idx < cand))
            return jnp.where(cnt < need, cand, c)
        return lax.fori_loop(0, n_idx_bits, idx_body, jnp.zeros((rows, 1), I32))

    excess = jnp.max(n_ge - kf) > 0.0
    cut = lax.cond(excess, tie_search, lambda: jnp.full((rows, 1), BIG_IDX, I32))
    cut = jnp.where(n_ge > kf, cut, BIG_IDX)
    return thr, cut


def _causal_blocks(step_fn, qi):
    def far_body(j, c):
        step_fn(j, "far")
        return c
    lax.fori_loop(0, qi - 1, far_body, 0)

    @pl.when(qi >= 1)
    def _():
        step_fn(jnp.maximum(qi - 1, 0), "prev")

    step_fn(qi, "diag")


def _causal_neg(tq, tk):
    row = lax.broadcasted_iota(I32, (tq, tk), 0)
    col = lax.broadcasted_iota(I32, (tq, tk), 1)
    return col <= row


def _fox_prompt_kernel(q_ref, kv_ref, cq_ref, nck_ref, o_ref, m_ref, l_ref, acc_ref, *, n_h, col0):
    tq = q_ref.shape[1]
    tk = tq
    qi = pl.program_id(1)
    _init_flash(m_ref, l_ref, acc_ref)
    q = q_ref[0] * 0.125
    cq = cq_ref[0]
    causal = _causal_neg(tq, tk)
    hw = n_h * DH

    def step(j, kind):
        kvb = kv_ref[0, pl.ds(pl.multiple_of(j * tk, tk), tk), :]
        nck = nck_ref[0, j]
        for h in range(n_h):
            k = kvb[:, h * DH:(h + 1) * DH].astype(BF16)
            v = kvb[:, hw + h * DH:hw + (h + 1) * DH].astype(BF16)
            s = _dot_nt(q[:, h * DH:(h + 1) * DH], k)
            s = s + cq[:, col0 + h:col0 + h + 1] + nck[h:h + 1, :]
            if kind == "diag":
                s = jnp.where(causal, s, NEG)
            _flash_update(s, v, m_ref, l_ref, acc_ref, h)

    _causal_blocks(step, qi)
    for h in range(n_h):
        o_ref[0, :, h * DH:(h + 1) * DH] = (acc_ref[h] / l_ref[h]).astype(o_ref.dtype)


def _fox_prompt(q_all, q_blk, kv, c_all, negck, n_h, col0):
    bsz, t, _ = kv.shape
    tq = min(t, ATT_BLK)
    nb = t // tq
    hw = n_h * DH
    nck = negck.reshape(bsz, n_h, nb, tq).transpose(0, 2, 1, 3)
    return pl.pallas_call(
        functools.partial(_fox_prompt_kernel, n_h=n_h, col0=col0),
        grid=(bsz, nb),
        in_specs=[pl.BlockSpec((1, tq, hw), lambda b, i: (b, i, q_blk)),
                  pl.BlockSpec((1, t, 2 * hw), lambda b, i: (b, 0, 0)),
                  pl.BlockSpec((1, tq, LANES), lambda b, i: (b, i, 0)),
                  pl.BlockSpec((1, nb, n_h, tq), lambda b, i: (b, 0, 0, 0))],
        out_specs=pl.BlockSpec((1, tq, hw), lambda b, i: (b, i, 0)),
        out_shape=jax.ShapeDtypeStruct((bsz, t, hw), BF16),
        scratch_shapes=[pltpu.VMEM((n_h, tq, 1), F32), pltpu.VMEM((n_h, tq, 1), F32),
                        pltpu.VMEM((n_h, tq, DH), F32)],
        compiler_params=_cparams(("parallel", "parallel")),
        name="fox_prompt",
    )(q_all, kv, c_all, nck)


def _diff_finish(a0, l0, a1, l1, lam, gsub, scale):
    o = a0 / l0 - lam * (a1 / l1)
    return _rms(o, gsub) * scale


def _diff_prompt_kernel(far_ref, lam_ref, q_ref, kv_ref, bias_ref, gs_ref, o_ref,
                        m_ref, l_ref, acc_ref, *, n_h, out_scale):
    tq = q_ref.shape[1]
    tk = tq
    qi = pl.program_id(1)
    _init_flash(m_ref, l_ref, acc_ref)
    q = q_ref[0] * 0.125
    causal = _causal_neg(tq, tk)
    dv = 2 * DH
    hw = n_h * dv

    def step(j, kind):
        kvb = kv_ref[0, pl.ds(pl.multiple_of(j * tk, tk), tk), :]
        for h in range(n_h):
            v = kvb[:, hw + h * dv:hw + (h + 1) * dv].astype(BF16)
            if kind == "far":
                bias = far_ref[h]
            else:
                bias = bias_ref[h, 0 if kind == "prev" else 1]
            for jj in range(2):
                vh = 2 * h + jj
                k = kvb[:, vh * DH:(vh + 1) * DH].astype(BF16)
                s = _dot_nt(q[:, vh * DH:(vh + 1) * DH], k) + bias
                if kind == "diag":
                    s = jnp.where(causal, s, NEG)
                _flash_update(s, v, m_ref, l_ref, acc_ref, vh)

    _causal_blocks(step, qi)
    lam = lam_ref[0]
    for h in range(n_h):
        o = _diff_finish(acc_ref[2 * h], l_ref[2 * h], acc_ref[2 * h + 1], l_ref[2 * h + 1],
                         lam, gs_ref[...], out_scale)
        o_ref[0, :, h * dv:(h + 1) * dv] = o.astype(o_ref.dtype)


def _diff_prompt(q_all, q_blk, kv, bias, far, lam, gsub, n_h, out_scale):
    bsz, t, _ = kv.shape
    tq = min(t, ATT_BLK)
    nb = t // tq
    hw = n_h * 2 * DH
    smem = pl.BlockSpec(memory_space=pltpu.SMEM)
    return pl.pallas_call(
        functools.partial(_diff_prompt_kernel, n_h=n_h, out_scale=out_scale),
        grid=(bsz, nb),
        in_specs=[smem, smem,
                  pl.BlockSpec((1, tq, hw), lambda b, i: (b, i, q_blk)),
                  pl.BlockSpec((1, t, 2 * hw), lambda b, i: (b, 0, 0)),
                  pl.BlockSpec((n_h, 2, tq, tq), lambda b, i: (0, 0, 0, 0)),
                  pl.BlockSpec((1, 2 * DH), lambda b, i: (0, 0))],
        out_specs=pl.BlockSpec((1, tq, hw), lambda b, i: (b, i, 0)),
        out_shape=jax.ShapeDtypeStruct((bsz, t, hw), BF16),
        scratch_shapes=[pltpu.VMEM((2 * n_h, tq, 1), F32), pltpu.VMEM((2 * n_h, tq, 1), F32),
                        pltpu.VMEM((2 * n_h, tq, 2 * DH), F32)],
        compiler_params=_cparams(("parallel", "parallel")),
        name="diff_prompt",
    )(far, lam, q_all, kv, bias, gsub.reshape(1, 2 * DH))


def _dsa_prompt_kernel(far_ref, q_ref, qi_ref, wq_ref, ki_ref, kv_ref, bias_ref, o_ref,
                       key_ref, m_ref, l_ref, acc_ref, *, n_h, topk, wi_col0):
    tq = q_ref.shape[1]
    tk = tq
    nb = key_ref.shape[0]
    qi = pl.program_id(1)
    _init_flash(m_ref, l_ref, acc_ref)
    causal = _causal_neg(tq, tk)
    hw = n_h * DH

    qidx = qi_ref[0]
    wrow = wq_ref[0][:, wi_col0:wi_col0 + H_IDX] * (1.0 / (math.sqrt(H_IDX) * math.sqrt(D_IDX)))

    def score_block(j, diag):
        kib = ki_ref[0, pl.ds(pl.multiple_of(j * tk, tk), tk), :][:, 0:D_IDX].astype(BF16)
        sc = jnp.zeros((tq, tk), F32)
        for h in range(H_IDX):
            d = _dot_nt(qidx[:, h * D_IDX:(h + 1) * D_IDX], kib)
            sc = sc + jnp.maximum(d, 0.0) * wrow[:, h:h + 1]
        if diag:
            sc = jnp.where(causal, sc, -jnp.inf)
        key_ref[j] = _sortable(sc)

    def score_body(j, c):
        score_block(j, False)
        return c
    lax.fori_loop(0, qi, score_body, 0)
    score_block(qi, True)

    col = lax.broadcasted_iota(I32, (tq, tk), 1)

    def count_fn(pred):
        def body(j, part):
            hit = pred(key_ref[j], col + j * tk)
            return part + _lane_fold(jnp.where(hit, 1.0, 0.0))
        part = lax.fori_loop(0, qi + 1, body, jnp.zeros((tq, LANES), F32))
        return jnp.sum(part, axis=1, keepdims=True)

    n_bits = max(1, int(math.ceil(math.log2(nb * tk))))
    thr, cut = _topk_select(count_fn, tq, topk, n_bits)

    q = q_ref[0] * 0.125

    def step(j, kind):
        kvb = kv_ref[0, pl.ds(pl.multiple_of(j * tk, tk), tk), :]
        key = key_ref[j]
        sel = (key > thr) | ((key == thr) & (col + j * tk <= cut))
        if kind == "diag":
            sel = sel & causal
        for h in range(n_h):
            k = kvb[:, h * DH:(h + 1) * DH].astype(BF16)
            v = kvb[:, hw + h * DH:hw + (h + 1) * DH].astype(BF16)
            if kind == "far":
                bias = far_ref[h]
            else:
                bias = bias_ref[h, 0 if kind == "prev" else 1]
            s = _dot_nt(q[:, h * DH:(h + 1) * DH], k) + bias
            s = jnp.where(sel, s, NEG)
            _flash_update(s, v, m_ref, l_ref, acc_ref, h)

    _causal_blocks(step, qi)
    for h in range(n_h):
        o_ref[0, :, h * DH:(h + 1) * DH] = (acc_ref[h] / l_ref[h]).astype(o_ref.dtype)


def _dsa_prompt(q_all, q_blk, small, kv, bias, far, n_h, topk, wi_col0):
    bsz, t, _ = kv.shape
    tq = min(t, ATT_BLK)
    nb = t // tq
    hw = n_h * DH
    smem = pl.BlockSpec(memory_space=pltpu.SMEM)
    return pl.pallas_call(
        functools.partial(_dsa_prompt_kernel, n_h=n_h, topk=topk, wi_col0=wi_col0),
        grid=(bsz, nb),
        in_specs=[smem,
                  pl.BlockSpec((1, tq, hw), lambda b, i: (b, i, q_blk)),
                  pl.BlockSpec((1, tq, H_IDX * D_IDX), lambda b, i: (b, i, 0)),
                  pl.BlockSpec((1, tq, LANES), lambda b, i: (b, i, 0)),
                  pl.BlockSpec((1, t, LANES), lambda b, i: (b, 0, 0)),
                  pl.BlockSpec((1, t, 2 * hw), lambda b, i: (b, 0, 0)),
                  pl.BlockSpec((n_h, 2, tq, tq), lambda b, i: (0, 0, 0, 0))],
        out_specs=pl.BlockSpec((1, tq, hw), lambda b, i: (b, i, 0)),
        out_shape=jax.ShapeDtypeStruct((bsz, t, hw), BF16),
        scratch_shapes=[pltpu.VMEM((nb, tq, tq), I32),
                        pltpu.VMEM((n_h, tq, 1), F32), pltpu.VMEM((n_h, tq, 1), F32),
                        pltpu.VMEM((n_h, tq, DH), F32)],
        compiler_params=_cparams(("parallel", "parallel")),
        name="dsa_prompt",
    )(far, q_all, q_all, small, small, kv, bias)


def _dsa_index_sample_kernel(pt_ref, qi_ref, w_ref, *refs, topk, n_q):
    del pt_ref
    pp = PAGES_PER_STEP
    page_refs = refs[:pp]
    kinew_ref, o_ref = refs[pp:]
    s = pl.program_id(1)
    ns = pl.num_programs(1) - 1
    nblk = o_ref.shape[1]
    wblk = o_ref.shape[3]
    qrows = qi_ref[0]
    w = w_ref[0]

    def scores(ki_t):
        d = _dot(qrows, ki_t.astype(BF16))
        r = jnp.maximum(d, 0.0) * w
        sc = r[0:n_q, :]
        for h in range(1, H_IDX):
            sc = sc + r[h * n_q:(h + 1) * n_q, :]
        return sc

    @pl.when(s < ns)
    def _():
        o_ref[0, s] = jnp.concatenate([scores(page_refs[p][0, 0]) for p in range(pp)], axis=1)

    @pl.when(s == ns)
    def _():
        row = lax.broadcasted_iota(I32, (n_q, PAGE), 0)
        colp = lax.broadcasted_iota(I32, (n_q, PAGE), 1)
        sc = jnp.where(colp <= row, scores(kinew_ref[0]), -jnp.inf)
        pad = jnp.full((n_q, wblk - PAGE), -jnp.inf, F32)
        o_ref[0, ns] = jnp.concatenate([sc, pad], axis=1)

        col = lax.broadcasted_iota(I32, (n_q, wblk), 1)

        def count_fn(pred):
            def body(j, part):
                hit = pred(_sortable(o_ref[0, j]), col + j * wblk)
                return part + _lane_fold(jnp.where(hit, 1.0, 0.0))
            part = lax.fori_loop(0, nblk, body, jnp.zeros((n_q, LANES), F32))
            return jnp.sum(part, axis=1, keepdims=True)

        n_bits = max(1, int(math.ceil(math.log2(nblk * wblk))))
        thr, cut = _topk_select(count_fn, n_q, topk, n_bits)

        def mask_body(j, c):
            key = _sortable(o_ref[0, j])
            sel = (key > thr) | ((key == thr) & (col + j * wblk <= cut))
            valid = o_ref[0, j] > -jnp.inf
            o_ref[0, j] = jnp.where(sel & valid, 0.0, NEG)
            return c
        lax.fori_loop(0, nblk, mask_body, 0)


def _dsa_index_sample(qi_rows, w_rows, ki_pool, layer, page_table, ki_new, topk, n_q):
    bsz, n_pages = page_table.shape
    pp = PAGES_PER_STEP
    ns = n_pages // pp
    wblk = pp * PAGE

    def page_map(p):
        return lambda b, s, pt: (layer, pt[b, jnp.minimum(s, ns - 1) * pp + p], 0, 0)

    rows = qi_rows.shape[1]
    return pl.pallas_call(
        functools.partial(_dsa_index_sample_kernel, topk=topk, n_q=n_q),
        grid_spec=pltpu.PrefetchScalarGridSpec(
            num_scalar_prefetch=1,
            grid=(bsz, ns + 1),
            in_specs=[pl.BlockSpec((1, rows, D_IDX), lambda b, s, pt: (b, 0, 0)),
                      pl.BlockSpec((1, rows, 1), lambda b, s, pt: (b, 0, 0))]
                     + [pl.BlockSpec((1, 1, D_IDX, PAGE), page_map(p)) for p in range(pp)]
                     + [pl.BlockSpec((1, D_IDX, PAGE), lambda b, s, pt: (b, 0, 0))],
            out_specs=pl.BlockSpec((1, ns + 1, n_q, wblk), lambda b, s, pt: (b, 0, 0, 0))),
        out_shape=jax.ShapeDtypeStruct((bsz, ns + 1, n_q, wblk), F32),
        compiler_params=_cparams(("parallel", "arbitrary")),
        name="dsa_index_sample",
    )(page_table, qi_rows, w_rows, *([ki_pool] * pp), ki_new)


def _paged_attn_kernel(pt_ref, *refs, mode, n_q, n_h, dv, out_scale):
    del pt_ref
    pp = PAGES_PER_STEP
    it = iter(refs)
    lam_ref = next(it) if mode == "diff" else None
    q_ref = next(it)
    page_refs = [next(it) for _ in range(pp)]
    kvnew_ref = next(it)
    add_ref = next(it) if mode in ("dsa", "fox") else None
    addnew_ref = next(it) if mode in ("dsa", "fox") else None
    rowc_ref = next(it)
    bprev_ref = next(it)
    bnew_ref = next(it)
    gs_ref = next(it) if mode == "diff" else None
    o_ref, m_ref, l_ref, acc_ref = next(it), next(it), next(it), next(it)

    s = pl.program_id(1)
    ns = pl.num_programs(1) - 1
    rows = q_ref.shape[1]
    n_vh = rows // n_q

    @pl.when(s == 0)
    def _():
        _init_flash(m_ref, l_ref, acc_ref)

    q = q_ref[0]
    rowc = rowc_ref[0]

    if mode == "diff":
        rph = rows // n_h
        stride = 2 * n_h

        def logits(get):
            return jnp.concatenate(
                [_dot_nt(q[h * rph:(h + 1) * rph], get(pl.ds(h, PAGE, stride=stride)).astype(BF16))
                 for h in range(n_h)], axis=0)

        def weighted_values(p, get):
            return jnp.concatenate(
                [_dot(p[h * rph:(h + 1) * rph], get(pl.ds(n_h + h, PAGE, stride=stride)).astype(BF16))
                 for h in range(n_h)], axis=0)
    else:
        kw = q.shape[1]

        def logits(get):
            return _dot(q, get(slice(0, kw)).astype(BF16))

        def weighted_values(p, get):
            return _dot_nt(p, get(slice(kw, 2 * kw)).astype(BF16))

    def row_add(blk):
        if mode == "dsa":
            return jnp.concatenate([blk] * n_vh, axis=0)
        if mode == "fox":
            return jnp.concatenate(
                [jnp.broadcast_to(blk[h:h + 1, :], (n_q, blk.shape[1])) for h in range(n_vh)], axis=0)
        return None

    def update(x, values_fn):
        m_old = m_ref[0]
        m_new = jnp.maximum(m_old, jnp.max(x, axis=-1, keepdims=True))
        alpha = jnp.exp(m_old - m_new)
        p32 = jnp.exp(x - m_new)
        l_ref[0] = alpha * l_ref[0] + jnp.sum(p32, axis=-1, keepdims=True)
        acc_ref[0] = alpha * acc_ref[0] + values_fn(p32.astype(BF16))
        m_ref[0] = m_new

    @pl.when(s < ns)
    def _():
        gets = [(lambda idx, r=page_refs[p]: r[0, 0, idx, :]) for p in range(pp)]
        x = jnp.concatenate([logits(g) for g in gets], axis=1)
        x = x + rowc
        if add_ref is not None:
            x = x + row_add(add_ref[0, 0])
        last = jnp.where(s == ns - 1, bprev_ref[...], 0.0)
        x = jnp.concatenate([x[:, 0:(pp - 1) * PAGE], x[:, (pp - 1) * PAGE:] + last], axis=1)

        def values_fn(p):
            pv = weighted_values(p[:, 0:PAGE], gets[0])
            for i in range(1, pp):
                pv = pv + weighted_values(p[:, i * PAGE:(i + 1) * PAGE], gets[i])
            return pv
        update(x, values_fn)

    @pl.when(s == ns)
    def _():
        get = lambda idx: kvnew_ref[0, idx, :]
        x = logits(get) + rowc + bnew_ref[...]
        if add_ref is not None:
            x = x + row_add(addnew_ref[0])
        update(x, lambda p: weighted_values(p, get))
        acc = acc_ref[0]
        l = l_ref[0]
        if mode == "diff":
            lam = lam_ref[0]
            outs = []
            for h in range(n_h):
                r0 = h * rph
                r1 = r0 + n_q
                outs.append(_diff_finish(acc[r0:r0 + n_q], l[r0:r0 + n_q], acc[r1:r1 + n_q], l[r1:r1 + n_q],
                                         lam, gs_ref[...], out_scale))
        else:
            outs = [acc[h * n_q:(h + 1) * n_q, h * dv:(h + 1) * dv] / l[h * n_q:(h + 1) * n_q]
                    for h in range(n_h)]
        o_ref[0] = jnp.concatenate(outs, axis=1).astype(o_ref.dtype)


def _paged_attn(mode, q, pool, layer, page_table, kv_new, add, add_new, rowc, bprev, bnew,
                n_q, n_h, dv, lam=None, gsub=None, out_scale=1.0):
    bsz, n_pages = page_table.shape
    pp = PAGES_PER_STEP
    ns = n_pages // pp
    rows, qw = q.shape[1:]
    prow, pcol = pool.shape[2:]
    acc_w = dv if mode == "diff" else qw

    def page_map(p):
        return lambda b, s, pt: (layer, pt[b, jnp.minimum(s, ns - 1) * pp + p], 0, 0)

    in_specs = []
    args = []
    if mode == "diff":
        in_specs.append(pl.BlockSpec(memory_space=pltpu.SMEM))
        args.append(lam)
    in_specs.append(pl.BlockSpec((1, rows, qw), lambda b, s, pt: (b, 0, 0)))
    args.append(q)
    in_specs += [pl.BlockSpec((1, 1, prow, pcol), page_map(p)) for p in range(pp)]
    args += [pool] * pp
    in_specs.append(pl.BlockSpec((1, prow, pcol), lambda b, s, pt: (b, 0, 0)))
    args.append(kv_new)
    if add is not None:
        in_specs.append(pl.BlockSpec((1, 1, add.shape[2], add.shape[3]),
                                     lambda b, s, pt: (b, jnp.minimum(s, ns - 1), 0, 0)))
        args.append(add)
        in_specs.append(pl.BlockSpec((1, add_new.shape[1], PAGE), lambda b, s, pt: (b, 0, 0)))
        args.append(add_new)
    in_specs.append(pl.BlockSpec((1, rows, 1), lambda b, s, pt: (b, 0, 0)))
    args.append(rowc)
    in_specs.append(pl.BlockSpec((rows, PAGE), lambda b, s, pt: (0, 0)))
    args.append(bprev)
    in_specs.append(pl.BlockSpec((rows, PAGE), lambda b, s, pt: (0, 0)))
    args.append(bnew)
    if mode == "diff":
        in_specs.append(pl.BlockSpec((1, dv), lambda b, s, pt: (0, 0)))
        args.append(gsub.reshape(1, dv))
    return pl.pallas_call(
        functools.partial(_paged_attn_kernel, mode=mode, n_q=n_q, n_h=n_h, dv=dv, out_scale=out_scale),
        grid_spec=pltpu.PrefetchScalarGridSpec(
            num_scalar_prefetch=1,
            grid=(bsz, ns + 1),
            in_specs=in_specs,
            out_specs=pl.BlockSpec((1, n_q, n_h * dv), lambda b, s, pt: (b, 0, 0)),
            scratch_shapes=[pltpu.VMEM((1, rows, 1), F32), pltpu.VMEM((1, rows, 1), F32),
                            pltpu.VMEM((1, rows, acc_w), F32)]),
        out_shape=jax.ShapeDtypeStruct((bsz, n_q, n_h * dv), BF16),
        compiler_params=_cparams(("parallel", "arbitrary")),
        name="paged_attn_" + mode,
    )(page_table, *args)


def _s5_kernel(u_ref, h0r_ref, h0i_ref, bre_ref, bim_ref, cre_ref, cim_ref, alr_ref, ali_ref,
               acr_ref, aci_ref, d_ref, wglu_ref, o_ref, sr_ref, si_ref,
               hr_ref, hi_ref, cr_ref, ci_ref):
    t = pl.program_id(1)
    ln = u_ref.shape[1]

    @pl.when(t == 0)
    def _():
        cr_ref[...] = h0r_ref[0]
        ci_ref[...] = h0i_ref[0]

    u = u_ref[0]
    ub = u.astype(BF16)
    xr = _dot(ub, bre_ref[...])
    xi = _dot(ub, bim_ref[...])
    row = lax.broadcasted_iota(I32, (ln, 1), 0) % SUBLANES
    for kk, sh in enumerate((1, 2, 4)):
        ar = alr_ref[kk:kk + 1, :]
        ai = ali_ref[kk:kk + 1, :]
        pr = pltpu.roll(xr, sh, 0)
        pi = pltpu.roll(xi, sh, 0)
        keep = row >= sh
        xr, xi = (xr + jnp.where(keep, ar * pr - ai * pi, 0.0),
                  xi + jnp.where(keep, ar * pi + ai * pr, 0.0))
    hr_ref[...] = xr
    hi_ref[...] = xi
    acr = acr_ref[...]
    aci = aci_ref[...]

    def group(r, carry):
        cr, ci = carry
        sl = pl.ds(pl.multiple_of(r * SUBLANES, SUBLANES), SUBLANES)
        br = hr_ref[sl, :] + acr * cr - aci * ci
        bi = hi_ref[sl, :] + acr * ci + aci * cr
        hr_ref[sl, :] = br
        hi_ref[sl, :] = bi
        return br[SUBLANES - 1:SUBLANES, :], bi[SUBLANES - 1:SUBLANES, :]

    cr, ci = lax.fori_loop(0, ln // SUBLANES, group, (cr_ref[...], ci_ref[...]))
    cr_ref[...] = cr
    ci_ref[...] = ci
    y = _dot(hr_ref[...].astype(BF16), cre_ref[...]) - _dot(hi_ref[...].astype(BF16), cim_ref[...])
    y = _gelu(y + d_ref[...] * u)
    z = _dot(y.astype(BF16), wglu_ref[...])
    o_ref[0] = (y * _sigmoid(z)).astype(o_ref.dtype)
    sr_ref[0] = cr
    si_ref[0] = ci


def _s5_mixer(u, h0r, h0i, prm):
    bsz, t, ch = u.shape
    n_state = h0r.shape[-1]
    ln = min(t, 256)
    const = lambda shape: pl.BlockSpec(shape, lambda b, i: (0, 0))
    state = pl.BlockSpec((1, 1, n_state), lambda b, i: (b, 0, 0))
    return pl.pallas_call(
        _s5_kernel,
        grid=(bsz, t // ln),
        in_specs=[pl.BlockSpec((1, ln, ch), lambda b, i: (b, i, 0)), state, state,
                  const((ch, n_state)), const((ch, n_state)),
                  const((n_state, ch)), const((n_state, ch)),
                  const((SUBLANES, n_state)), const((SUBLANES, n_state)),
                  const((SUBLANES, n_state)), const((SUBLANES, n_state)),
                  const((1, ch)), const((ch, ch))],
        out_specs=[pl.BlockSpec((1, ln, ch), lambda b, i: (b, i, 0)), state, state],
        out_shape=[jax.ShapeDtypeStruct((bsz, t, ch), BF16),
                   jax.ShapeDtypeStruct((bsz, 1, n_state), F32),
                   jax.ShapeDtypeStruct((bsz, 1, n_state), F32)],
        scratch_shapes=[pltpu.VMEM((ln, n_state), F32), pltpu.VMEM((ln, n_state), F32),
                        pltpu.VMEM((1, n_state), F32), pltpu.VMEM((1, n_state), F32)],
        compiler_params=_cparams(("parallel", "arbitrary")),
        name="s5_mixer",
    )(u, h0r, h0i, prm["bre"], prm["bim"], prm["cre"], prm["cim"],
      prm["alr"], prm["ali"], prm["acr"], prm["aci"], prm["d"], prm["wglu"])


def _s5_params(a_re, a_im, log_dt, b_re, b_im, c_re, c_im, d, w_glu):
    g, p = a_re.shape
    c = b_re.shape[-1]
    dt = jnp.exp(log_dt)[:, None]
    mag = jnp.exp(a_re * dt)
    ar = mag * jnp.cos(a_im * dt)
    ai = mag * jnp.sin(a_im * dt)
    den = a_re * a_re + a_im * a_im
    fr = ((ar - 1.0) * a_re + ai * a_im) / den
    fi = (ai * a_re - (ar - 1.0) * a_im) / den
    bbr = fr[..., None] * b_re - fi[..., None] * b_im
    bbi = fr[..., None] * b_im + fi[..., None] * b_re
    eye = jnp.eye(g, dtype=F32)

    def in_proj(x):
        return jnp.einsum("gpc,gh->gchp", x, eye).reshape(g * c, g * p).astype(BF16)

    def out_proj(x):
        return jnp.einsum("gcp,gh->gphc", x, eye).reshape(g * p, g * c).astype(BF16)

    def powers(n_list):
        rs, is_ = [], []
        for n in n_list:
            m = jnp.exp(a_re * dt * n)
            rs.append((m * jnp.cos(a_im * dt * n)).reshape(1, g * p))
            is_.append((m * jnp.sin(a_im * dt * n)).reshape(1, g * p))
        pad = SUBLANES - len(n_list)
        if pad:
            rs += [jnp.zeros((pad, g * p), F32)]
            is_ += [jnp.zeros((pad, g * p), F32)]
        return jnp.concatenate(rs, axis=0), jnp.concatenate(is_, axis=0)

    alr, ali = powers([1, 2, 4])
    acr, aci = powers(list(range(1, SUBLANES + 1)))
    return {"bre": in_proj(bbr), "bim": in_proj(bbi), "cre": out_proj(c_re), "cim": out_proj(c_im),
            "alr": alr, "ali": ali, "acr": acr, "aci": aci,
            "d": d.reshape(1, g * c), "wglu": w_glu.astype(BF16)}


def _xattn_kernel(hn_ref, wq_ref, mkv_ref, o_ref, *, n_h):
    qx = _dot(hn_ref[0], wq_ref[...])
    mkv = mkv_ref[0]
    hw = n_h * DH_X
    scale = DH_X ** -0.5
    for h in range(n_h):
        qh = qx[:, h * DH_X:(h + 1) * DH_X].astype(BF16)
        kh = mkv[:, h * DH_X:(h + 1) * DH_X].astype(BF16)
        vh = mkv[:, hw + h * DH_X:hw + (h + 1) * DH_X].astype(BF16)
        s = _dot_nt(qh, kh) * scale
        m = jnp.max(s, axis=-1, keepdims=True)
        p = jnp.exp(s - m)
        l = jnp.sum(p, axis=-1, keepdims=True)
        o = _dot(p.astype(BF16), vh) / l
        o_ref[0, :, h * DH_X:(h + 1) * DH_X] = o.astype(o_ref.dtype)


def _xattn(hn, w_qx, mem_kv, n_h):
    bsz, t, d = hn.shape
    n_mem = mem_kv.shape[1]
    hw = n_h * DH_X
    tq = min(t, 256)
    return pl.pallas_call(
        functools.partial(_xattn_kernel, n_h=n_h),
        grid=(bsz, t // tq),
        in_specs=[pl.BlockSpec((1, tq, d), lambda b, i: (b, i, 0)),
                  pl.BlockSpec((d, hw), lambda b, i: (0, 0)),
                  pl.BlockSpec((1, n_mem, 2 * hw), lambda b, i: (b, 0, 0))],
        out_specs=pl.BlockSpec((1, tq, hw), lambda b, i: (b, i, 0)),
        out_shape=jax.ShapeDtypeStruct((bsz, t, hw), BF16),
        compiler_params=_cparams(("parallel", "parallel")),
        name="xattn",
    )(hn, w_qx, mem_kv)


def _block_diag_q(q, n_vh, scale):
    bsz, n_q, _ = q.shape
    qh = (q.astype(F32) * scale).reshape(bsz, n_q, n_vh, DH)
    eye = jnp.eye(n_vh, dtype=F32)
    out = jnp.einsum("bqhd,hg->bhqgd", qh, eye)
    return out.reshape(bsz, n_vh * n_q, n_vh * DH).astype(BF16)


def _map_diag_q(q, n_h, scale):
    bsz, n_q, _ = q.shape
    qh = (q.astype(F32) * scale).reshape(bsz, n_q, n_h, 2, DH)
    eye = jnp.eye(2, dtype=F32)
    out = jnp.einsum("bqhjd,jg->bhjqgd", qh, eye)
    return out.reshape(bsz, n_h * 2 * n_q, 2 * DH).astype(BF16)


def kernel(x_prompt, x_sample, mem_prompt, cache_a_kv, cache_a_idx_k, cache_c_kv, cache_d_kv,
           cache_d_logf, cache_mem_kv, state_s5_re, state_s5_im, state_ffn_conv, page_table,
           rel_bias, g_mix_pre, w_in, s5_a_re, s5_a_im, s5_log_dt, s5_b_re, s5_b_im, s5_c_re,
           s5_c_im, s5_d, s5_w_glu, lam_q1, lam_k1, lam_q2, lam_k2, diff_sub_g, fox_b_f, w_out,
           g_mix_post, g_x_pre, w_qx, w_kvx, w_ox, g_x_post, g_ffn_pre, w_gate, w_up, conv_w,
           conv_b, w_down, g_ffn_post):
    bp, t, d = x_prompt.shape
    bs, ts, _ = x_sample.shape
    depth = w_in.shape[0]
    n_pool = cache_a_kv.shape[1]
    n_pages = page_table.shape[1]
    past = n_pages * PAGE
    h_a = cache_a_kv.shape[4]
    h_c = cache_c_kv.shape[4]
    h_d = cache_d_kv.shape[4]
    h_x = cache_mem_kv.shape[4]
    n_mem = cache_mem_kv.shape[2]
    s5_g, s5_p = s5_a_re.shape[1:]
    s5_ch = s5_g * S5_GROUP
    n_state = s5_g * s5_p
    d_ff = w_gate.shape[2]
    wa, wc, wd = h_a * DH, h_c * 2 * DH, h_d * DH
    wqi = H_IDX * D_IDX
    assert wa == wc == wd == s5_ch and ts == SUBLANES and past % (PAGES_PER_STEP * PAGE) == 0
    pp = PAGES_PER_STEP
    ns = n_pages // pp
    topk_p = min(TOPK_MAX, t // 4)
    topk_s = min(TOPK_MAX, (past + ts) // 4)

    sizes = (wa, wa, wa, wqi, D_IDX, H_IDX, s5_ch, wc, wc, wc, wd, wd, wd, h_d)
    offs = np.concatenate([[0], np.cumsum(sizes)]).tolist()
    (o_aq, o_ak, o_av, o_aqi, o_aki, o_awi, o_bu, o_cq, o_ck, o_cv, o_dq, o_dk, o_dv, o_df) = offs[:-1]
    col_wi, col_df = D_IDX, D_IDX + H_IDX

    tq = min(t, ATT_BLK)
    bias_p = jnp.stack([_bias_tile(rel_bias, tq, tq, tq), _bias_tile(rel_bias, 0, tq, tq)], axis=1)
    bias_s_prev = _bias_tile(rel_bias, PAGE, ts, PAGE)
    bias_s_new = _bias_tile(rel_bias, 0, ts, PAGE)
    far = rel_bias[NUM_BUCKETS - 1]
    causal_new = jnp.where(jnp.arange(PAGE)[None, :] <= jnp.arange(ts)[:, None], 0.0, NEG).astype(F32)

    def sample_tiles(h0, n_h, rep):
        hs = [h0 + i // rep for i in range(n_h * rep)]
        farc = jnp.stack([jnp.full((ts, 1), 1.0, F32) * far[h] for h in hs]).reshape(-1, 1)
        prev = jnp.stack([bias_s_prev[h] for h in hs]).reshape(-1, PAGE) - farc
        new = jnp.stack([bias_s_new[h] + causal_new for h in hs]).reshape(-1, PAGE) - farc
        return jnp.broadcast_to(farc[None], (bs,) + farc.shape), prev, new

    rowc_a, bprev_a, bnew_a = sample_tiles(0, h_a, 1)
    rowc_c, bprev_c, bnew_c = sample_tiles(h_a, h_c, 2)
    bprev_d = jnp.zeros((h_d * ts, PAGE), F32)
    bnew_d = jnp.tile(causal_new, (h_d, 1))

    key_minor = lambda c: jnp.moveaxis(c, 2, -1)
    pool_a = key_minor(cache_a_kv).reshape(depth, n_pool, 2 * wa, PAGE)
    pool_d = key_minor(cache_d_kv).reshape(depth, n_pool, 2 * wd, PAGE)
    pool_c = cache_c_kv.reshape(depth, n_pool, PAGE * 2 * h_c, 2 * DH)
    pool_ki = key_minor(cache_a_idx_k)
    pool_lf = key_minor(cache_d_logf)

    xp = x_prompt.reshape(bp * t, d)
    xs = x_sample.reshape(bs * ts, d)
    mem_bf = mem_prompt.reshape(bp * n_mem, d).astype(BF16)
    hn_p = _norm_cast(xp, g_mix_pre[0])
    hn_s = _norm_cast(xs, g_mix_pre[0])

    outs = {k: [] for k in ("a_kv_p", "a_kv_s", "a_ki_p", "a_ki_s", "c_kv_p", "c_kv_s", "d_kv_p",
                            "d_kv_s", "lf_p", "lf_s", "mem_kv", "s5r_p", "s5r_s", "s5i_p", "s5i_s",
                            "cs_p", "cs_s")}

    for l in range(depth):
        wl = w_in[l]
        w_akv = wl[:, o_ak:o_aqi].astype(BF16)
        w_ckv = wl[:, o_ck:o_dq].astype(BF16)
        w_dkv = wl[:, o_dk:o_df].astype(BF16)
        w_q = jnp.concatenate([wl[:, o_aqi:o_aki], wl[:, o_aq:o_ak], wl[:, o_cq:o_ck],
                               wl[:, o_dq:o_dk]], axis=1).astype(BF16)
        w_bu = wl[:, o_bu:o_cq].astype(BF16)
        w_small = jnp.concatenate([wl[:, o_aki:o_bu], wl[:, o_df:],
                                   jnp.zeros((d, LANES - D_IDX - H_IDX - h_d), F32)], axis=1).astype(BF16)
        qblk_a, qblk_c, qblk_d = wqi // wa, wqi // wa + 1, wqi // wa + 2
        lam_init = 0.8 - 0.6 * math.exp(-0.3 * l)
        lam = (jnp.exp(jnp.sum(lam_q1[l] * lam_k1[l])) - jnp.exp(jnp.sum(lam_q2[l] * lam_k2[l]))
               + lam_init).reshape(1)
        bvec = jnp.zeros((1, LANES), F32).at[0, col_df:col_df + h_d].set(fox_b_f[l])
        s5p = _s5_params(s5_a_re[l], s5_a_im[l], s5_log_dt[l], s5_b_re[l], s5_b_im[l],
                         s5_c_re[l], s5_c_im[l], s5_d[l], s5_w_glu[l])
        w_out_b = w_out[l].astype(BF16)
        w_qx_b = w_qx[l].astype(BF16)
        w_ox_b = w_ox[l].astype(BF16)
        w_gate_b = w_gate[l].astype(BF16)
        w_up_b = w_up[l].astype(BF16)
        w_down_b = w_down[l].astype(BF16)
        g_next = g_mix_pre[l + 1] if l + 1 < depth else None

        mkv_p = _matmul(mem_bf, w_kvx[l].astype(BF16), F32).reshape(bp, n_mem, 2 * h_x * DH_X)
        outs["mem_kv"].append(mkv_p)

        def mixer_inputs(hn):
            return (_matmul(hn, w_akv, F32), _matmul(hn, w_ckv, F32), _matmul(hn, w_dkv, F32),
                    _matmul(hn, w_q, BF16), _matmul(hn, w_bu, F32), _matmul(hn, w_small, F32))

        a_kv, c_kv, d_kv, q_all, b_u, small = mixer_inputs(hn_p)
        a_kv3 = a_kv.reshape(bp, t, 2 * wa)
        c_kv3 = c_kv.reshape(bp, t, 2 * wc)
        d_kv3 = d_kv.reshape(bp, t, 2 * wd)
        q3 = q_all.reshape(bp, t, -1)
        small3 = small.reshape(bp, t, LANES)
        lf3, c3 = _logsig_cumsum(small3, bvec)
        negck = -jnp.swapaxes(c3[:, :, col_df:col_df + h_d], 1, 2)
        o_a = _dsa_prompt(q3, qblk_a, small3, a_kv3, bias_p[:h_a], far[:h_a], h_a, topk_p, col_wi)
        o_b, s5r, s5i = _s5_mixer(b_u.reshape(bp, t, s5_ch), jnp.zeros((bp, 1, n_state), F32),
                                  jnp.zeros((bp, 1, n_state), F32), s5p)
        o_c = _diff_prompt(q3, qblk_c, c_kv3, bias_p[h_a:], far[h_a:], lam, diff_sub_g[l], h_c,
                           1.0 - lam_init)
        o_d = _fox_prompt(q3, qblk_d, d_kv3, c3, negck, h_d, col_df)
        mix = [o.reshape(bp * t, -1) for o in (o_a, o_b, o_c, o_d)]
        xp, hn = _proj_res(mix, w_out_b, xp, g_mix_post[l], g_x_pre[l])
        ox = _xattn(hn.reshape(bp, t, d), w_qx_b, mkv_p, h_x)
        xp, hn = _proj_res([ox.reshape(bp * t, -1)], w_ox_b, xp, g_x_post[l], g_ffn_pre[l])
        hid, cs = _ffn_hidden(hn, w_gate_b, w_up_b, conv_w[l], conv_b[l],
                              jnp.zeros((bp, CONV_W - 1, d_ff), F32), t)
        xp, hn_p = _proj_res_ktiled(hid, w_down_b, xp, g_ffn_post[l], g_next)
        outs["a_kv_p"].append(a_kv3)
        outs["a_ki_p"].append(small3[:, :, 0:D_IDX])
        outs["c_kv_p"].append(c_kv3)
        outs["d_kv_p"].append(d_kv3)
        outs["lf_p"].append(lf3[:, :, col_df:col_df + h_d])
        outs["s5r_p"].append(s5r)
        outs["s5i_p"].append(s5i)
        outs["cs_p"].append(cs)

        a_kv, c_kv, d_kv, q_all, b_u, small = mixer_inputs(hn_s)
        a_kv3 = a_kv.reshape(bs, ts, 2 * wa)
        c_kv3 = c_kv.reshape(bs, ts, 2 * wc)
        d_kv3 = d_kv.reshape(bs, ts, 2 * wd)
        q3 = q_all.reshape(bs, ts, -1)
        small3 = small.reshape(bs, ts, LANES)
        lf3, c3 = _logsig_cumsum(small3, bvec)
        keys_on_lanes = lambda x: jnp.pad(jnp.swapaxes(x, 1, 2), ((0, 0), (0, 0), (0, PAGE - ts)))
        keys_on_rows = lambda x: jnp.pad(x, ((0, 0), (0, PAGE - ts), (0, 0)))

        qi_rows = jnp.swapaxes(q3[:, :, 0:wqi].reshape(bs, ts, H_IDX, D_IDX), 1, 2)
        qi_rows = qi_rows.reshape(bs, H_IDX * ts, D_IDX)
        w_rows = jnp.swapaxes(small3[:, :, col_wi:col_wi + H_IDX], 1, 2).reshape(bs, H_IDX * ts, 1)
        w_rows = w_rows * (1.0 / (math.sqrt(H_IDX) * math.sqrt(D_IDX)))
        amask = _dsa_index_sample(qi_rows, w_rows, pool_ki, l, page_table,
                                  keys_on_lanes(small3[:, :, 0:D_IDX]), topk_s, ts)
        o_a = _paged_attn("dsa", _block_diag_q(q3[:, :, wqi:wqi + wa], h_a, 0.125), pool_a, l,
                          page_table, keys_on_lanes(a_kv3), amask, amask[:, ns, :, 0:PAGE],
                          rowc_a, bprev_a, bnew_a, ts, h_a, DH)
        o_b, s5r, s5i = _s5_mixer(b_u.reshape(bs, ts, s5_ch),
                                  state_s5_re[l].reshape(bs, 1, n_state),
                                  state_s5_im[l].reshape(bs, 1, n_state), s5p)
        o_c = _paged_attn("diff", _map_diag_q(q3[:, :, wqi + wa:wqi + wa + wc], h_c, 0.125),
                          pool_c, l, page_table,
                          keys_on_rows(c_kv3).reshape(bs, PAGE * 2 * h_c, 2 * DH), None, None,
                          rowc_c, bprev_c, bnew_c,
                          ts, h_c, 2 * DH, lam=lam, gsub=diff_sub_g[l], out_scale=1.0 - lam_init)
        c_new = c3[:, :, col_df:col_df + h_d]
        nck_past = _page_suffix(pool_lf, l, page_table)
        nck_new = -keys_on_lanes(c_new)
        rowc_d = jnp.swapaxes(c_new, 1, 2).reshape(bs, h_d * ts, 1)
        o_d = _paged_attn("fox", _block_diag_q(q3[:, :, wqi + wa + wc:], h_d, 0.125), pool_d, l,
                          page_table, keys_on_lanes(d_kv3), nck_past, nck_new,
                          rowc_d, bprev_d, bnew_d, ts, h_d, DH)
        mix = [o.reshape(bs * ts, -1) for o in (o_a, o_b, o_c, o_d)]
        xs, hn = _proj_res(mix, w_out_b, xs, g_mix_post[l], g_x_pre[l])
        mkv_s = cache_mem_kv[l].reshape(bs, n_mem, 2 * h_x * DH_X)
        ox = _xattn(hn.reshape(bs, ts, d), w_qx_b, mkv_s, h_x)
        xs, hn = _proj_res([ox.reshape(bs * ts, -1)], w_ox_b, xs, g_x_post[l], g_ffn_pre[l])
        hid, cs = _ffn_hidden(hn, w_gate_b, w_up_b, conv_w[l], conv_b[l], state_ffn_conv[l], ts)
        xs, hn_s = _proj_res_ktiled(hid, w_down_b, xs, g_ffn_post[l], g_next)
        outs["a_kv_s"].append(a_kv3)
        outs["a_ki_s"].append(small3[:, :, 0:D_IDX])
        outs["c_kv_s"].append(c_kv3)
        outs["d_kv_s"].append(d_kv3)
        outs["lf_s"].append(lf3[:, :, col_df:col_df + h_d])
        outs["s5r_s"].append(s5r)
        outs["s5i_s"].append(s5i)
        outs["cs_s"].append(cs)

    st = {k: jnp.stack(v) for k, v in outs.items()}
    return (xp.reshape(bp, t, d), xs.reshape(bs, ts, d),
            st["a_kv_p"].reshape(depth, bp, t, 2, h_a, DH),
            st["a_kv_s"].reshape(depth, bs, ts, 2, h_a, DH),
            st["a_ki_p"], st["a_ki_s"],
            st["c_kv_p"].reshape(depth, bp, t, 2, h_c, 2 * DH),
            st["c_kv_s"].reshape(depth, bs, ts, 2, h_c, 2 * DH),
            st["d_kv_p"].reshape(depth, bp, t, 2, h_d, DH),
            st["d_kv_s"].reshape(depth, bs, ts, 2, h_d, DH),
            st["lf_p"], st["lf_s"],
            st["mem_kv"].reshape(depth, bp, n_mem, 2, h_x, DH_X),
            st["s5r_p"].reshape(depth, bp, s5_g, s5_p), st["s5r_s"].reshape(depth, bs, s5_g, s5_p),
            st["s5i_p"].reshape(depth, bp, s5_g, s5_p), st["s5i_s"].reshape(depth, bs, s5_g, s5_p),
            st["cs_p"], st["cs_s"])
```

```python
import functools
import math

import numpy as np
import jax
import jax.numpy as jnp
from jax import lax
from jax.experimental import pallas as pl
from jax.experimental.pallas import tpu as pltpu

F32 = jnp.float32
BF16 = jnp.bfloat16
I32 = jnp.int32

EPS = 1e-6
NEG = -1e30
M_INIT = -3e38
INT_MIN = -2 ** 31
BIG_IDX = 2 ** 30

DH = 64
H_IDX = 16
D_IDX = 64
TOPK_MAX = 256
S5_GROUP = 16
S5_P = 64
DH_X = 128
NUM_BUCKETS = 32
MAX_DISTANCE = 128
CONV_W = 3
PAGE = 128

LANES = 128
SUBLANES = 8
VMEM_LIMIT = 52 * 1024 * 1024

ATT_BLK = 256
PAGES_PER_STEP = 8


def _t5_thresholds():
    exact = NUM_BUCKETS // 2
    n = np.arange(exact, MAX_DISTANCE + 1).astype(np.float64)
    large = exact + np.floor(np.log(n / exact) / math.log(MAX_DISTANCE / exact)
                             * (NUM_BUCKETS - exact)).astype(np.int64)
    bucket = np.minimum(large, NUM_BUCKETS - 1)
    return [int(n[np.argmax(bucket >= b)]) for b in range(exact + 1, NUM_BUCKETS)]


_T5_THR = _t5_thresholds()


def _cparams(sem):
    return pltpu.CompilerParams(dimension_semantics=sem, vmem_limit_bytes=VMEM_LIMIT)


def _dot(a, b):
    return jnp.dot(a, b, preferred_element_type=F32)


def _dot_nt(a, b):
    return lax.dot_general(a, b, (((1,), (1,)), ((), ())), preferred_element_type=F32)


def _rms(x, g):
    y = x * lax.rsqrt(jnp.mean(x * x, axis=-1, keepdims=True) + EPS)
    return y * g


def _gelu(x):
    c = math.sqrt(2.0 / math.pi)
    return 0.5 * x * (1.0 + jnp.tanh(c * (x + 0.044715 * (x * x * x))))


def _sigmoid(x):
    return 1.0 / (1.0 + jnp.exp(-x))


def _log_sigmoid(x):
    return jnp.minimum(x, 0.0) - jnp.log(1.0 + jnp.exp(-jnp.abs(x)))


def _norm_cast_kernel(x_ref, g_ref, o_ref):
    o_ref[...] = _rms(x_ref[...], g_ref[...]).astype(o_ref.dtype)


def _norm_cast(x, g):
    m, d = x.shape
    tm = min(m, 512)
    return pl.pallas_call(
        _norm_cast_kernel,
        grid=(m // tm,),
        in_specs=[pl.BlockSpec((tm, d), lambda i: (i, 0)),
                  pl.BlockSpec((1, d), lambda i: (0, 0))],
        out_specs=pl.BlockSpec((tm, d), lambda i: (i, 0)),
        out_shape=jax.ShapeDtypeStruct((m, d), BF16),
        compiler_params=_cparams(("parallel",)),
        name="norm_cast",
    )(x, g.reshape(1, d))


def _mm_kernel(a_ref, w_ref, o_ref):
    o_ref[...] = _dot(a_ref[...], w_ref[...]).astype(o_ref.dtype)


def _matmul(a, w, out_dtype):
    m, k = a.shape
    n = w.shape[1]
    tm = min(m, 1024)
    tn = min(n, 512)
    return pl.pallas_call(
        _mm_kernel,
        grid=(m // tm, n // tn),
        in_specs=[pl.BlockSpec((tm, k), lambda i, j: (i, 0)),
                  pl.BlockSpec((k, tn), lambda i, j: (0, j))],
        out_specs=pl.BlockSpec((tm, tn), lambda i, j: (i, j)),
        out_shape=jax.ShapeDtypeStruct((m, n), out_dtype),
        compiler_params=_cparams(("parallel", "parallel")),
        name="matmul",
    )(a, w)


def _mm_nt_kernel(w_ref, a_ref, *o_refs, plain, blocked):
    acc = _dot_nt(w_ref[...], a_ref[0])
    k = 0
    if plain:
        o_refs[k][0] = acc.astype(o_refs[k].dtype)
        k += 1
    if blocked:
        ob = o_refs[k]
        tk = ob.shape[3]
        for i in range(ob.shape[1]):
            ob[0, i] = acc[:, i * tk:(i + 1) * tk].astype(ob.dtype)


def _matmul_nt(w_t, a3, plain_dtype, blocked):
    n, k = w_t.shape
    bsz, t, _ = a3.shape
    tm = min(t, 1024)
    tn = min(n, 512)
    tk = min(t, ATT_BLK)
    out_specs, out_shape = [], []
    if plain_dtype is not None:
        out_specs.append(pl.BlockSpec((1, tn, tm), lambda b, i, j: (b, j, i)))
        out_shape.append(jax.ShapeDtypeStruct((bsz, n, t), plain_dtype))
    if blocked:
        out_specs.append(pl.BlockSpec((1, tm // tk, tn, tk), lambda b, i, j: (b, i, j, 0)))
        out_shape.append(jax.ShapeDtypeStruct((bsz, t // tk, n, tk), BF16))
    return pl.pallas_call(
        functools.partial(_mm_nt_kernel, plain=plain_dtype is not None, blocked=blocked),
        grid=(bsz, t // tm, n // tn),
        in_specs=[pl.BlockSpec((tn, k), lambda b, i, j: (j, 0)),
                  pl.BlockSpec((1, tm, k), lambda b, i, j: (b, i, 0))],
        out_specs=out_specs,
        out_shape=out_shape,
        compiler_params=_cparams(("parallel", "parallel", "parallel")),
        name="matmul_nt",
    )(w_t, a3)


def _proj_res_kernel(*refs, n_in, emit_next):
    a_refs = refs[:n_in]
    w_ref, x_ref, gp_ref, gn_ref = refs[n_in:n_in + 4]
    outs = refs[n_in + 4:]
    off = 0
    acc = None
    for a_ref in a_refs:
        kk = a_ref.shape[1]
        part = _dot(a_ref[...], w_ref[off:off + kk, :])
        acc = part if acc is None else acc + part
        off += kk
    xn = x_ref[...] + _rms(acc, gp_ref[...])
    outs[0][...] = xn
    if emit_next:
        outs[1][...] = _rms(xn, gn_ref[...]).astype(BF16)


def _proj_res(a_list, w, x, g_post, g_next):
    m, d = x.shape
    k = w.shape[0]
    tm = min(m, 256)
    emit_next = g_next is not None
    gn = g_next if emit_next else g_post
    in_specs = [pl.BlockSpec((tm, a.shape[1]), lambda i: (i, 0)) for a in a_list]
    in_specs += [pl.BlockSpec((k, d), lambda i: (0, 0)),
                 pl.BlockSpec((tm, d), lambda i: (i, 0)),
                 pl.BlockSpec((1, d), lambda i: (0, 0)),
                 pl.BlockSpec((1, d), lambda i: (0, 0))]
    out_specs = [pl.BlockSpec((tm, d), lambda i: (i, 0))]
    out_shape = [jax.ShapeDtypeStruct((m, d), F32)]
    if emit_next:
        out_specs.append(pl.BlockSpec((tm, d), lambda i: (i, 0)))
        out_shape.append(jax.ShapeDtypeStruct((m, d), BF16))
    res = pl.pallas_call(
        functools.partial(_proj_res_kernel, n_in=len(a_list), emit_next=emit_next),
        grid=(m // tm,),
        in_specs=in_specs,
        out_specs=out_specs,
        out_shape=out_shape,
        compiler_params=_cparams(("parallel",)),
        name="proj_res",
    )(*a_list, w, x, g_post.reshape(1, d), gn.reshape(1, d))
    return res[0], (res[1] if emit_next else None)


def _proj_res_kt_kernel(a_ref, w_ref, x_ref, gp_ref, gn_ref, *rest, emit_next):
    if emit_next:
        xo_ref, hn_ref, acc_ref = rest
    else:
        xo_ref, acc_ref = rest
    kk = pl.program_id(1)

    @pl.when(kk == 0)
    def _():
        acc_ref[...] = jnp.zeros_like(acc_ref)

    acc_ref[...] += _dot(a_ref[...], w_ref[...])

    @pl.when(kk == pl.num_programs(1) - 1)
    def _():
        xn = x_ref[...] + _rms(acc_ref[...], gp_ref[...])
        xo_ref[...] = xn
        if emit_next:
            hn_ref[...] = _rms(xn, gn_ref[...]).astype(BF16)


def _proj_res_ktiled(a, w, x, g_post, g_next):
    m, d = x.shape
    k = w.shape[0]
    tm = min(m, 512)
    tk = 512
    emit_next = g_next is not None
    gn = g_next if emit_next else g_post
    out_specs = [pl.BlockSpec((tm, d), lambda i, j: (i, 0))]
    out_shape = [jax.ShapeDtypeStruct((m, d), F32)]
    if emit_next:
        out_specs.append(pl.BlockSpec((tm, d), lambda i, j: (i, 0)))
        out_shape.append(jax.ShapeDtypeStruct((m, d), BF16))
    res = pl.pallas_call(
        functools.partial(_proj_res_kt_kernel, emit_next=emit_next),
        grid=(m // tm, k // tk),
        in_specs=[pl.BlockSpec((tm, tk), lambda i, j: (i, j)),
                  pl.BlockSpec((tk, d), lambda i, j: (j, 0)),
                  pl.BlockSpec((tm, d), lambda i, j: (i, 0)),
                  pl.BlockSpec((1, d), lambda i, j: (0, 0)),
                  pl.BlockSpec((1, d), lambda i, j: (0, 0))],
        out_specs=out_specs,
        out_shape=out_shape,
        scratch_shapes=[pltpu.VMEM((tm, d), F32)],
        compiler_params=_cparams(("parallel", "arbitrary")),
        name="proj_res_ktiled",
    )(a, w, x, g_post.reshape(1, d), gn.reshape(1, d))
    return res[0], (res[1] if emit_next else None)


def _ffn_hidden_kernel(hn_ref, wg_ref, wu_ref, cw_ref, cb_ref, hb1_ref, hb2_ref,
                       h_ref, cs_ref, *, seq_len, n_seq):
    hn = hn_ref[...]
    g = _dot(hn, wg_ref[...])
    u = _dot(hn, wu_ref[...])
    w0 = cw_ref[0:1, :]
    w1 = cw_ref[1:2, :]
    w2 = cw_ref[2:3, :]
    cb = cb_ref[...]
    if seq_len > SUBLANES:
        gc = cb + w0 * pltpu.roll(g, 2, 0) + w1 * pltpu.roll(g, 1, 0) + w2 * g
        h_ref[...] = (_gelu(gc) * u).astype(h_ref.dtype)
    row = lax.broadcasted_iota(I32, (SUBLANES, 1), 0)
    for s in range(n_seq):
        r0 = s * seq_len
        g8 = g[r0:r0 + SUBLANES, :]
        p1 = jnp.where(row < 1, hb1_ref[s], pltpu.roll(g8, 1, 0))
        p2 = jnp.where(row < 2, hb2_ref[s], pltpu.roll(g8, 2, 0))
        gc8 = cb + w0 * p2 + w1 * p1 + w2 * g8
        h_ref[r0:r0 + SUBLANES, :] = (_gelu(gc8) * u[r0:r0 + SUBLANES, :]).astype(h_ref.dtype)
        cs_ref[s] = g[r0 + seq_len - 2:r0 + seq_len, :]


def _ffn_hidden(hn, w_gate, w_up, conv_w, conv_b, buf, seq_len):
    m, d = hn.shape
    f = w_gate.shape[1]
    bsz = m // seq_len
    n_seq = 1 if seq_len > SUBLANES else bsz
    tt = seq_len * n_seq
    tf = 256
    zeros = jnp.zeros((bsz, SUBLANES - 2, f), F32)
    hb1 = jnp.concatenate([buf[:, 1:2], jnp.zeros((bsz, 1, f), F32), zeros], axis=1)
    hb2 = jnp.concatenate([buf, zeros], axis=1)
    h, cs = pl.pallas_call(
        functools.partial(_ffn_hidden_kernel, seq_len=seq_len, n_seq=n_seq),
        grid=(m // tt, f // tf),
        in_specs=[pl.BlockSpec((tt, d), lambda i, j: (i, 0)),
                  pl.BlockSpec((d, tf), lambda i, j: (0, j)),
                  pl.BlockSpec((d, tf), lambda i, j: (0, j)),
                  pl.BlockSpec((CONV_W, tf), lambda i, j: (0, j)),
                  pl.BlockSpec((1, tf), lambda i, j: (0, j)),
                  pl.BlockSpec((n_seq, SUBLANES, tf), lambda i, j: (i, 0, j)),
                  pl.BlockSpec((n_seq, SUBLANES, tf), lambda i, j: (i, 0, j))],
        out_specs=[pl.BlockSpec((tt, tf), lambda i, j: (i, j)),
                   pl.BlockSpec((n_seq, CONV_W - 1, tf), lambda i, j: (i, 0, j))],
        out_shape=[jax.ShapeDtypeStruct((m, f), BF16),
                   jax.ShapeDtypeStruct((bsz, CONV_W - 1, f), F32)],
        compiler_params=_cparams(("parallel", "parallel")),
        name="ffn_hidden",
    )(hn, w_gate, w_up, conv_w, conv_b.reshape(1, f), hb1, hb2)
    return h, cs


def _bias_tile_kernel(tab_ref, o_ref, *, off, keys_on_rows):
    nh, r, c = o_ref.shape
    i = lax.broadcasted_iota(I32, (r, c), 0)
    j = lax.broadcasted_iota(I32, (r, c), 1)
    n = jnp.maximum(off + (j - i if keys_on_rows else i - j), 0)
    large = jnp.full((r, c), NUM_BUCKETS // 2, I32)
    for thr in _T5_THR:
        large = large + jnp.where(n >= thr, 1, 0)
    bucket = jnp.where(n < NUM_BUCKETS // 2, n, large)

    def head(h, carry):
        val = jnp.full((r, c), tab_ref[NUM_BUCKETS - 1, h], F32)
        for b in range(NUM_BUCKETS - 2, -1, -1):
            val = jnp.where(bucket == b, tab_ref[b, h], val)
        o_ref[h] = val
        return carry
    lax.fori_loop(0, nh, head, 0)


def _bias_tile(rel_bias, off, r, c, keys_on_rows=False):
    nh = rel_bias.shape[1]
    return pl.pallas_call(
        functools.partial(_bias_tile_kernel, off=off, keys_on_rows=keys_on_rows),
        in_specs=[pl.BlockSpec(memory_space=pltpu.SMEM)],
        out_specs=pl.BlockSpec(memory_space=pltpu.VMEM),
        out_shape=jax.ShapeDtypeStruct((nh, r, c), F32),
        compiler_params=pltpu.CompilerParams(vmem_limit_bytes=VMEM_LIMIT),
        name="t5_bias_tile",
    )(rel_bias)


def _logsig_cumsum_kernel(x_ref, b_ref, lf_ref, c_ref):
    t = x_ref.shape[1]
    lf = _log_sigmoid(x_ref[0] + b_ref[...])
    lf_ref[0] = lf
    row = lax.broadcasted_iota(I32, (t, 1), 0)
    c = lf
    s = 1
    while s < t:
        c = c + jnp.where(row >= s, pltpu.roll(c, s, 0), 0.0)
        s *= 2
    c_ref[0] = c


def _logsig_cumsum(x, bvec):
    bsz, t, w = x.shape
    return pl.pallas_call(
        _logsig_cumsum_kernel,
        grid=(bsz,),
        in_specs=[pl.BlockSpec((1, t, w), lambda b: (b, 0, 0)),
                  pl.BlockSpec((1, w), lambda b: (0, 0))],
        out_specs=[pl.BlockSpec((1, t, w), lambda b: (b, 0, 0)),
                   pl.BlockSpec((1, t, w), lambda b: (b, 0, 0))],
        out_shape=[jax.ShapeDtypeStruct((bsz, t, w), F32),
                   jax.ShapeDtypeStruct((bsz, t, w), F32)],
        compiler_params=_cparams(("parallel",)),
        name="logsig_cumsum",
    )(x, bvec)


def _logsig_cumsum_t_kernel(x_ref, b_ref, lf_ref, c_ref, *, row0):
    n_h, t = lf_ref.shape[1:]
    lf = _log_sigmoid(x_ref[0, row0:row0 + n_h, :] + b_ref[...])
    lf_ref[0] = lf
    lane = lax.broadcasted_iota(I32, (1, t), 1)
    c = lf
    s = 1
    while s < t:
        c = c + jnp.where(lane >= s, pltpu.roll(c, s, 1), 0.0)
        s *= 2
    c_ref[0] = c


def _logsig_cumsum_t(x_t, bcol, row0):
    bsz, w, t = x_t.shape
    n_h = bcol.shape[0]
    return pl.pallas_call(
        functools.partial(_logsig_cumsum_t_kernel, row0=row0),
        grid=(bsz,),
        in_specs=[pl.BlockSpec((1, w, t), lambda b: (b, 0, 0)),
                  pl.BlockSpec((n_h, 1), lambda b: (0, 0))],
        out_specs=[pl.BlockSpec((1, n_h, t), lambda b: (b, 0, 0)),
                   pl.BlockSpec((1, n_h, t), lambda b: (b, 0, 0))],
        out_shape=[jax.ShapeDtypeStruct((bsz, n_h, t), F32),
                   jax.ShapeDtypeStruct((bsz, n_h, t), F32)],
        compiler_params=_cparams(("parallel",)),
        name="logsig_cumsum_t",
    )(x_t, bcol)


def _page_suffix_kernel(pt_ref, *refs):
    del pt_ref
    pp = PAGES_PER_STEP
    page_refs = refs[:pp]
    o_ref, carry_ref = refs[pp:]
    s = pl.program_id(1)

    @pl.when(s == 0)
    def _():
        carry_ref[...] = jnp.zeros_like(carry_ref)

    x0 = jnp.concatenate([page_refs[p][0, 0] for p in range(pp)], axis=1)
    w = x0.shape[1]
    lane = lax.broadcasted_iota(I32, (1, w), 1) % PAGE
    x = x0
    sh = 1
    while sh < PAGE:
        x = x + jnp.where(lane + sh < PAGE, pltpu.roll(x, w - sh, 1), 0.0)
        sh *= 2
    run = carry_ref[...]
    pieces = [None] * pp
    for p in reversed(range(pp)):
        sl = slice(p * PAGE, (p + 1) * PAGE)
        pieces[p] = (x[:, sl] - x0[:, sl]) + run
        run = run + jnp.sum(x0[:, sl], axis=1, keepdims=True)
    o_ref[0, 0] = jnp.concatenate(pieces, axis=1)
    carry_ref[...] = run


def _page_suffix(lf_pool_t, layer, page_table):
    n_h = lf_pool_t.shape[2]
    bsz, n_pages = page_table.shape
    pp = PAGES_PER_STEP
    ns = n_pages // pp

    def page_map(p):
        return lambda b, s, pt: (layer, pt[b, (ns - 1 - s) * pp + p], 0, 0)

    return pl.pallas_call(
        _page_suffix_kernel,
        grid_spec=pltpu.PrefetchScalarGridSpec(
            num_scalar_prefetch=1,
            grid=(bsz, ns),
            in_specs=[pl.BlockSpec((1, 1, n_h, PAGE), page_map(p)) for p in range(pp)],
            out_specs=pl.BlockSpec((1, 1, n_h, pp * PAGE), lambda b, s, pt: (b, ns - 1 - s, 0, 0)),
            scratch_shapes=[pltpu.VMEM((n_h, 1), F32)]),
        out_shape=jax.ShapeDtypeStruct((bsz, ns, n_h, pp * PAGE), F32),
        compiler_params=_cparams(("parallel", "arbitrary")),
        name="page_suffix",
    )(page_table, *([lf_pool_t] * pp))


def _init_flash(m_ref, l_ref, acc_ref):
    m_ref[...] = jnp.full(m_ref.shape, M_INIT, F32)
    l_ref[...] = jnp.zeros_like(l_ref)
    acc_ref[...] = jnp.zeros_like(acc_ref)


def _sortable(x):
    bits = lax.bitcast_convert_type(x + 0.0, I32)
    return jnp.where(bits < 0, bits ^ 0x7FFFFFFF, bits)


def _lane_fold(x):
    n = x.shape[1] // LANES
    acc = x[:, 0:LANES]
    for i in range(1, n):
        acc = acc + x[:, i * LANES:(i + 1) * LANES]
    return acc


def _topk_select(count_fn, shape, k, n_idx_bits):
    kf = float(k)

    def bit_body(i, ans):
        cand = ans + lax.shift_left(jnp.int32(1), 31 - i)
        cnt = count_fn(lambda key, idx: key >= cand)
        return jnp.where(cnt >= kf, cand, ans)

    thr = lax.fori_loop(0, 32, bit_body, jnp.full(shape, INT_MIN, I32))
    n_ge = count_fn(lambda key, idx: key >= thr)
    n_gt = count_fn(lambda key, idx: key > thr)
    need = kf - n_gt

    def tie_search():
        def idx_body(i, c):
            cand = c + lax.shift_left(jnp.int32(1), n_idx_bits - 1 - i)
            cnt = count_fn(lambda key, idx: (key == thr) & (idx < cand))
            return jnp.where(cnt < need, cand, c)
        return lax.fori_loop(0, n_idx_bits, idx_body, jnp.zeros(shape, I32))

    excess = jnp.max(n_ge - kf) > 0.0
    cut = lax.cond(excess, tie_search, lambda: jnp.full(shape, BIG_IDX, I32))
    cut = jnp.where(n_ge > kf, cut, BIG_IDX)
    return thr, cut


def _causal_blocks(step_fn, qi):
    def far_body(j, c):
        step_fn(j, "far")
        return c
    lax.fori_loop(0, qi - 1, far_body, 0)

    @pl.when(qi >= 1)
    def _():
        step_fn(jnp.maximum(qi - 1, 0), "prev")

    step_fn(qi, "diag")


def _causal_t(tk, tq):
    key = lax.broadcasted_iota(I32, (tk, tq), 0)
    qry = lax.broadcasted_iota(I32, (tk, tq), 1)
    return key <= qry


def _mask_queries(qt_ref, qm_ref):
    rowi = lax.broadcasted_iota(I32, (LANES, 1), 0)
    for v in range(qm_ref.shape[0]):
        pair, half = divmod(v, 2)
        qp = qt_ref[0, pair * LANES:(pair + 1) * LANES, :]
        keep = (rowi >= half * DH) & (rowi < (half + 1) * DH)
        qm_ref[v] = jnp.where(keep, qp * 0.125, 0.0).astype(BF16)


def _flash_update_t(st, v_t, m_ref, l_ref, acc_ref, idx):
    m_old = m_ref[idx]
    m_new = jnp.maximum(m_old, jnp.max(st, axis=0, keepdims=True))
    alpha = jnp.exp(m_old - m_new)
    p = jnp.exp(st - m_new)
    l_ref[idx] = alpha * l_ref[idx] + jnp.sum(p, axis=0, keepdims=True)
    acc_ref[idx] = alpha * acc_ref[idx] + _dot(v_t, p.astype(BF16))
    m_ref[idx] = m_new


def _store_head_pairs(o_ref, m_ref, l_ref, acc_ref, n_h):
    del m_ref
    for p in range(n_h // 2):
        o2 = jnp.concatenate([acc_ref[2 * p] / l_ref[2 * p], acc_ref[2 * p + 1] / l_ref[2 * p + 1]], axis=0)
        o_ref[0, :, p * LANES:(p + 1) * LANES] = o2.T.astype(o_ref.dtype)


def _fox_prompt_kernel(qt_ref, kk_ref, vt_ref, c_ref, ct_ref, o_ref, qm_ref, m_ref, l_ref, acc_ref,
                       *, n_h, col0):
    tq = qt_ref.shape[2]
    tk = tq
    qi = pl.program_id(1)
    _init_flash(m_ref, l_ref, acc_ref)
    _mask_queries(qt_ref, qm_ref)
    ct = ct_ref[0]
    causal = _causal_t(tk, tq)

    def step(j, kind):
        rows = pl.ds(pl.multiple_of(j * tk, tk), tk)
        kblk = kk_ref[0, rows, :]
        cblk = c_ref[0, rows, :]
        for h in range(n_h):
            pair = h // 2
            st = _dot(kblk[:, pair * LANES:(pair + 1) * LANES], qm_ref[h])
            st = st + ct[h:h + 1, :] - cblk[:, col0 + h:col0 + h + 1]
            if kind == "diag":
                st = jnp.where(causal, st, NEG)
            _flash_update_t(st, vt_ref[0, j, h * DH:(h + 1) * DH, :], m_ref, l_ref, acc_ref, h)

    _causal_blocks(step, qi)
    _store_head_pairs(o_ref, m_ref, l_ref, acc_ref, n_h)


def _fox_prompt(q_t, q_blk, kk, k_blk, vt_b, c_all, c_t, n_h, col0):
    bsz, t, _ = kk.shape
    tq = min(t, ATT_BLK)
    nb = t // tq
    hw = n_h * DH
    return pl.pallas_call(
        functools.partial(_fox_prompt_kernel, n_h=n_h, col0=col0),
        grid=(bsz, nb),
        in_specs=[pl.BlockSpec((1, hw, tq), lambda b, i: (b, q_blk, i)),
                  pl.BlockSpec((1, t, hw), lambda b, i: (b, 0, k_blk)),
                  pl.BlockSpec((1, nb, hw, tq), lambda b, i: (b, 0, 1, 0)),
                  pl.BlockSpec((1, t, LANES), lambda b, i: (b, 0, 0)),
                  pl.BlockSpec((1, n_h, tq), lambda b, i: (b, 0, i))],
        out_specs=pl.BlockSpec((1, tq, hw), lambda b, i: (b, i, 0)),
        out_shape=jax.ShapeDtypeStruct((bsz, t, hw), BF16),
        scratch_shapes=[pltpu.VMEM((n_h, LANES, tq), BF16),
                        pltpu.VMEM((n_h, 1, tq), F32), pltpu.VMEM((n_h, 1, tq), F32),
                        pltpu.VMEM((n_h, DH, tq), F32)],
        compiler_params=_cparams(("parallel", "parallel")),
        name="fox_prompt",
    )(q_t, kk, vt_b, c_all, c_t)


def _diff_finish(a0, l0, a1, l1, lam, gsub, scale):
    o = a0 / l0 - lam * (a1 / l1)
    return _rms(o, gsub) * scale


def _diff_prompt_kernel(far_ref, lam_ref, qt_ref, kk_ref, vt_ref, bias_ref, gs_ref, o_ref,
                        qm_ref, m_ref, l_ref, acc_ref, *, n_h, out_scale):
    tq = qt_ref.shape[2]
    tk = tq
    qi = pl.program_id(1)
    _init_flash(m_ref, l_ref, acc_ref)
    _mask_queries(qt_ref, qm_ref)
    causal = _causal_t(tk, tq)
    dv = 2 * DH

    def step(j, kind):
        kblk = kk_ref[0, pl.ds(pl.multiple_of(j * tk, tk), tk), :]
        for h in range(n_h):
            v_t = vt_ref[0, j, h * dv:(h + 1) * dv, :]
            if kind == "far":
                bias = far_ref[h]
            else:
                bias = bias_ref[h, 0 if kind == "prev" else 1]
            for jj in range(2):
                vh = 2 * h + jj
                st = _dot(kblk[:, h * dv:(h + 1) * dv], qm_ref[vh]) + bias
                if kind == "diag":
                    st = jnp.where(causal, st, NEG)
                _flash_update_t(st, v_t, m_ref, l_ref, acc_ref, vh)

    _causal_blocks(step, qi)
    lam = lam_ref[0]
    for h in range(n_h):
        o_t = acc_ref[2 * h] / l_ref[2 * h] - lam * (acc_ref[2 * h + 1] / l_ref[2 * h + 1])
        o = _rms(o_t.T, gs_ref[...]) * out_scale
        o_ref[0, :, h * dv:(h + 1) * dv] = o.astype(o_ref.dtype)


def _diff_prompt(q_t, q_blk, kk, k_blk, vt_b, bias_t, far, lam, gsub, n_h, out_scale):
    bsz, t, _ = kk.shape
    tq = min(t, ATT_BLK)
    nb = t // tq
    hw = n_h * 2 * DH
    smem = pl.BlockSpec(memory_space=pltpu.SMEM)
    return pl.pallas_call(
        functools.partial(_diff_prompt_kernel, n_h=n_h, out_scale=out_scale),
        grid=(bsz, nb),
        in_specs=[smem, smem,
                  pl.BlockSpec((1, hw, tq), lambda b, i: (b, q_blk, i)),
                  pl.BlockSpec((1, t, hw), lambda b, i: (b, 0, k_blk)),
                  pl.BlockSpec((1, nb, hw, tq), lambda b, i: (b, 0, 0, 0)),
                  pl.BlockSpec((n_h, 2, tq, tq), lambda b, i: (0, 0, 0, 0)),
                  pl.BlockSpec((1, 2 * DH), lambda b, i: (0, 0))],
        out_specs=pl.BlockSpec((1, tq, hw), lambda b, i: (b, i, 0)),
        out_shape=jax.ShapeDtypeStruct((bsz, t, hw), BF16),
        scratch_shapes=[pltpu.VMEM((2 * n_h, LANES, tq), BF16),
                        pltpu.VMEM((2 * n_h, 1, tq), F32), pltpu.VMEM((2 * n_h, 1, tq), F32),
                        pltpu.VMEM((2 * n_h, 2 * DH, tq), F32)],
        compiler_params=_cparams(("parallel", "parallel")),
        name="diff_prompt",
    )(far, lam, q_t, kk, vt_b, bias_t, gsub.reshape(1, 2 * DH))


def _dsa_prompt_kernel(far_ref, qt_ref, qit_ref, wt_ref, ki_ref, kk_ref, vt_ref, bias_ref, o_ref,
                       key_ref, qm_ref, m_ref, l_ref, acc_ref, *, n_h, topk, wi_row0):
    tq = qt_ref.shape[2]
    tk = tq
    nb = key_ref.shape[0]
    qi = pl.program_id(1)
    _init_flash(m_ref, l_ref, acc_ref)
    _mask_queries(qt_ref, qm_ref)
    causal = _causal_t(tk, tq)

    w_t = wt_ref[0, wi_row0:wi_row0 + H_IDX, :] * (1.0 / (math.sqrt(H_IDX) * math.sqrt(D_IDX)))

    def score_block(j, diag):
        kib = ki_ref[0, pl.ds(pl.multiple_of(j * tk, tk), tk), :][:, 0:D_IDX].astype(BF16)
        sc = jnp.zeros((tk, tq), F32)
        for h in range(H_IDX):
            d = _dot(kib, qit_ref[0, h * D_IDX:(h + 1) * D_IDX, :])
            sc = sc + jnp.maximum(d, 0.0) * w_t[h:h + 1, :]
        if diag:
            sc = jnp.where(causal, sc, -jnp.inf)
        key_ref[j] = _sortable(sc)

    def score_body(j, c):
        score_block(j, False)
        return c
    lax.fori_loop(0, qi, score_body, 0)
    score_block(qi, True)

    krow = lax.broadcasted_iota(I32, (tk, tq), 0)

    def count_fn(pred):
        def body(j, part):
            hit = jnp.where(pred(key_ref[j], krow + j * tk), 1.0, 0.0)
            return part + jnp.sum(hit.reshape(tk // SUBLANES, SUBLANES, tq), axis=0)
        part = lax.fori_loop(0, qi + 1, body, jnp.zeros((SUBLANES, tq), F32))
        return jnp.sum(part, axis=0, keepdims=True)

    n_bits = max(1, int(math.ceil(math.log2(nb * tk))))
    thr, cut = _topk_select(count_fn, (1, tq), topk, n_bits)

    def step(j, kind):
        kblk = kk_ref[0, pl.ds(pl.multiple_of(j * tk, tk), tk), :]
        key = key_ref[j]
        sel = (key > thr) | ((key == thr) & (krow + j * tk <= cut))
        if kind == "diag":
            sel = sel & causal
        for h in range(n_h):
            pair = h // 2
            if kind == "far":
                bias = far_ref[h]
            else:
                bias = bias_ref[h, 0 if kind == "prev" else 1]
            st = _dot(kblk[:, pair * LANES:(pair + 1) * LANES], qm_ref[h]) + bias
            st = jnp.where(sel, st, NEG)
            _flash_update_t(st, vt_ref[0, j, h * DH:(h + 1) * DH, :], m_ref, l_ref, acc_ref, h)

    _causal_blocks(step, qi)
    _store_head_pairs(o_ref, m_ref, l_ref, acc_ref, n_h)


def _dsa_prompt(q_t, q_blk, small_t, small, kk, k_blk, vt_b, bias_t, far, n_h, topk, wi_row0):
    bsz, t, _ = kk.shape
    tq = min(t, ATT_BLK)
    nb = t // tq
    hw = n_h * DH
    wqi = H_IDX * D_IDX
    smem = pl.BlockSpec(memory_space=pltpu.SMEM)
    return pl.pallas_call(
        functools.partial(_dsa_prompt_kernel, n_h=n_h, topk=topk, wi_row0=wi_row0),
        grid=(bsz, nb),
        in_specs=[smem,
                  pl.BlockSpec((1, hw, tq), lambda b, i: (b, q_blk, i)),
                  pl.BlockSpec((1, wqi, tq), lambda b, i: (b, 0, i)),
                  pl.BlockSpec((1, LANES, tq), lambda b, i: (b, 0, i)),
                  pl.BlockSpec((1, t, LANES), lambda b, i: (b, 0, 0)),
                  pl.BlockSpec((1, t, hw), lambda b, i: (b, 0, k_blk)),
                  pl.BlockSpec((1, nb, hw, tq), lambda b, i: (b, 0, 1, 0)),
                  pl.BlockSpec((n_h, 2, tq, tq), lambda b, i: (0, 0, 0, 0))],
        out_specs=pl.BlockSpec((1, tq, hw), lambda b, i: (b, i, 0)),
        out_shape=jax.ShapeDtypeStruct((bsz, t, hw), BF16),
        scratch_shapes=[pltpu.VMEM((nb, tq, tq), I32),
                        pltpu.VMEM((n_h, LANES, tq), BF16),
                        pltpu.VMEM((n_h, 1, tq), F32), pltpu.VMEM((n_h, 1, tq), F32),
                        pltpu.VMEM((n_h, DH, tq), F32)],
        compiler_params=_cparams(("parallel", "parallel")),
        name="dsa_prompt",
    )(far, q_t, q_t, small_t, small, kk, vt_b, bias_t)


def _dsa_index_sample_kernel(pt_ref, qi_ref, w_ref, *refs, topk, n_q):
    del pt_ref
    pp = PAGES_PER_STEP
    page_refs = refs[:pp]
    kinew_ref, o_ref = refs[pp:]
    s = pl.program_id(1)
    ns = pl.num_programs(1) - 1
    nblk = o_ref.shape[1]
    wblk = o_ref.shape[3]
    qrows = qi_ref[0]
    w = w_ref[0]

    def scores(ki_t):
        d = _dot(qrows, ki_t.astype(BF16))
        r = jnp.maximum(d, 0.0) * w
        sc = r[0:n_q, :]
        for h in range(1, H_IDX):
            sc = sc + r[h * n_q:(h + 1) * n_q, :]
        return sc

    @pl.when(s < ns)
    def _():
        o_ref[0, s] = jnp.concatenate([scores(page_refs[p][0, 0]) for p in range(pp)], axis=1)

    @pl.when(s == ns)
    def _():
        row = lax.broadcasted_iota(I32, (n_q, PAGE), 0)
        colp = lax.broadcasted_iota(I32, (n_q, PAGE), 1)
        sc = jnp.where(colp <= row, scores(kinew_ref[0]), -jnp.inf)
        pad = jnp.full((n_q, wblk - PAGE), -jnp.inf, F32)
        o_ref[0, ns] = jnp.concatenate([sc, pad], axis=1)

        col = lax.broadcasted_iota(I32, (n_q, wblk), 1)

        def count_fn(pred):
            def body(j, part):
                hit = pred(_sortable(o_ref[0, j]), col + j * wblk)
                return part + _lane_fold(jnp.where(hit, 1.0, 0.0))
            part = lax.fori_loop(0, nblk, body, jnp.zeros((n_q, LANES), F32))
            return jnp.sum(part, axis=1, keepdims=True)

        n_bits = max(1, int(math.ceil(math.log2(nblk * wblk))))
        thr, cut = _topk_select(count_fn, (n_q, 1), topk, n_bits)

        def mask_body(j, c):
            key = _sortable(o_ref[0, j])
            sel = (key > thr) | ((key == thr) & (col + j * wblk <= cut))
            valid = o_ref[0, j] > -jnp.inf
            o_ref[0, j] = jnp.where(sel & valid, 0.0, NEG)
            return c
        lax.fori_loop(0, nblk, mask_body, 0)


def _dsa_index_sample(qi_rows, w_rows, ki_pool, layer, page_table, ki_new, topk, n_q):
    bsz, n_pages = page_table.shape
    pp = PAGES_PER_STEP
    ns = n_pages // pp
    wblk = pp * PAGE

    def page_map(p):
        return lambda b, s, pt: (layer, pt[b, jnp.minimum(s, ns - 1) * pp + p], 0, 0)

    rows = qi_rows.shape[1]
    return pl.pallas_call(
        functools.partial(_dsa_index_sample_kernel, topk=topk, n_q=n_q),
        grid_spec=pltpu.PrefetchScalarGridSpec(
            num_scalar_prefetch=1,
            grid=(bsz, ns + 1),
            in_specs=[pl.BlockSpec((1, rows, D_IDX), lambda b, s, pt: (b, 0, 0)),
                      pl.BlockSpec((1, rows, 1), lambda b, s, pt: (b, 0, 0))]
                     + [pl.BlockSpec((1, 1, D_IDX, PAGE), page_map(p)) for p in range(pp)]
                     + [pl.BlockSpec((1, D_IDX, PAGE), lambda b, s, pt: (b, 0, 0))],
            out_specs=pl.BlockSpec((1, ns + 1, n_q, wblk), lambda b, s, pt: (b, 0, 0, 0))),
        out_shape=jax.ShapeDtypeStruct((bsz, ns + 1, n_q, wblk), F32),
        compiler_params=_cparams(("parallel", "arbitrary")),
        name="dsa_index_sample",
    )(page_table, qi_rows, w_rows, *([ki_pool] * pp), ki_new)


def _paged_attn_kernel(pt_ref, *refs, mode, n_q, n_h, dv, out_scale):
    del pt_ref
    pp = PAGES_PER_STEP
    it = iter(refs)
    lam_ref = next(it) if mode == "diff" else None
    q_ref = next(it)
    page_refs = [next(it) for _ in range(pp)]
    kvnew_ref = next(it)
    add_ref = next(it) if mode in ("dsa", "fox") else None
    addnew_ref = next(it) if mode in ("dsa", "fox") else None
    rowc_ref = next(it)
    bprev_ref = next(it)
    bnew_ref = next(it)
    gs_ref = next(it) if mode == "diff" else None
    o_ref, m_ref, l_ref, acc_ref = next(it), next(it), next(it), next(it)

    s = pl.program_id(1)
    ns = pl.num_programs(1) - 1
    rows = q_ref.shape[1]
    n_vh = rows // n_q

    @pl.when(s == 0)
    def _():
        _init_flash(m_ref, l_ref, acc_ref)

    q = q_ref[0]
    rowc = rowc_ref[0]

    if mode == "diff":
        rph = rows // n_h
        stride = 2 * n_h

        def logits(get):
            return jnp.concatenate(
                [_dot_nt(q[h * rph:(h + 1) * rph], get(pl.ds(h, PAGE, stride=stride)).astype(BF16))
                 for h in range(n_h)], axis=0)

        def weighted_values(p, get):
            return jnp.concatenate(
                [_dot(p[h * rph:(h + 1) * rph], get(pl.ds(n_h + h, PAGE, stride=stride)).astype(BF16))
                 for h in range(n_h)], axis=0)
    else:
        kw = q.shape[1]

        def logits(get):
            return _dot(q, get(slice(0, kw)).astype(BF16))

        def weighted_values(p, get):
            return _dot_nt(p, get(slice(kw, 2 * kw)).astype(BF16))

    def row_add(blk):
        if mode == "dsa":
            return jnp.concatenate([blk] * n_vh, axis=0)
        if mode == "fox":
            return jnp.concatenate(
                [jnp.broadcast_to(blk[h:h + 1, :], (n_q, blk.shape[1])) for h in range(n_vh)], axis=0)
        return None

    def update(x, values_fn):
        m_old = m_ref[0]
        m_new = jnp.maximum(m_old, jnp.max(x, axis=-1, keepdims=True))
        alpha = jnp.exp(m_old - m_new)
        p32 = jnp.exp(x - m_new)
        l_ref[0] = alpha * l_ref[0] + jnp.sum(p32, axis=-1, keepdims=True)
        acc_ref[0] = alpha * acc_ref[0] + values_fn(p32.astype(BF16))
        m_ref[0] = m_new

    @pl.when(s < ns)
    def _():
        gets = [(lambda idx, r=page_refs[p]: r[0, 0, idx, :]) for p in range(pp)]
        x = jnp.concatenate([logits(g) for g in gets], axis=1)
        x = x + rowc
        if add_ref is not None:
            x = x + row_add(add_ref[0, 0])
        last = jnp.where(s == ns - 1, bprev_ref[...], 0.0)
        x = jnp.concatenate([x[:, 0:(pp - 1) * PAGE], x[:, (pp - 1) * PAGE:] + last], axis=1)

        def values_fn(p):
            pv = weighted_values(p[:, 0:PAGE], gets[0])
            for i in range(1, pp):
                pv = pv + weighted_values(p[:, i * PAGE:(i + 1) * PAGE], gets[i])
            return pv
        update(x, values_fn)

    @pl.when(s == ns)
    def _():
        get = lambda idx: kvnew_ref[0, idx, :]
        x = logits(get) + rowc + bnew_ref[...]
        if add_ref is not None:
            x = x + row_add(addnew_ref[0])
        update(x, lambda p: weighted_values(p, get))
        acc = acc_ref[0]
        l = l_ref[0]
        if mode == "diff":
            lam = lam_ref[0]
            outs = []
            for h in range(n_h):
                r0 = h * rph
                r1 = r0 + n_q
                outs.append(_diff_finish(acc[r0:r0 + n_q], l[r0:r0 + n_q], acc[r1:r1 + n_q], l[r1:r1 + n_q],
                                         lam, gs_ref[...], out_scale))
        else:
            outs = [acc[h * n_q:(h + 1) * n_q, h * dv:(h + 1) * dv] / l[h * n_q:(h + 1) * n_q]
                    for h in range(n_h)]
        o_ref[0] = jnp.concatenate(outs, axis=1).astype(o_ref.dtype)


def _paged_attn(mode, q, pool, layer, page_table, kv_new, add, add_new, rowc, bprev, bnew,
                n_q, n_h, dv, lam=None, gsub=None, out_scale=1.0):
    bsz, n_pages = page_table.shape
    pp = PAGES_PER_STEP
    ns = n_pages // pp
    rows, qw = q.shape[1:]
    prow, pcol = pool.shape[2:]
    acc_w = dv if mode == "diff" else qw

    def page_map(p):
        return lambda b, s, pt: (layer, pt[b, jnp.minimum(s, ns - 1) * pp + p], 0, 0)

    in_specs = []
    args = []
    if mode == "diff":
        in_specs.append(pl.BlockSpec(memory_space=pltpu.SMEM))
        args.append(lam)
    in_specs.append(pl.BlockSpec((1, rows, qw), lambda b, s, pt: (b, 0, 0)))
    args.append(q)
    in_specs += [pl.BlockSpec((1, 1, prow, pcol), page_map(p)) for p in range(pp)]
    args += [pool] * pp
    in_specs.append(pl.BlockSpec((1, prow, pcol), lambda b, s, pt: (b, 0, 0)))
    args.append(kv_new)
    if add is not None:
        in_specs.append(pl.BlockSpec((1, 1, add.shape[2], add.shape[3]),
                                     lambda b, s, pt: (b, jnp.minimum(s, ns - 1), 0, 0)))
        args.append(add)
        in_specs.append(pl.BlockSpec((1, add_new.shape[1], PAGE), lambda b, s, pt: (b, 0, 0)))
        args.append(add_new)
    in_specs.append(pl.BlockSpec((1, rows, 1), lambda b, s, pt: (b, 0, 0)))
    args.append(rowc)
    in_specs.append(pl.BlockSpec((rows, PAGE), lambda b, s, pt: (0, 0)))
    args.append(bprev)
    in_specs.append(pl.BlockSpec((rows, PAGE), lambda b, s, pt: (0, 0)))
    args.append(bnew)
    if mode == "diff":
        in_specs.append(pl.BlockSpec((1, dv), lambda b, s, pt: (0, 0)))
        args.append(gsub.reshape(1, dv))
    return pl.pallas_call(
        functools.partial(_paged_attn_kernel, mode=mode, n_q=n_q, n_h=n_h, dv=dv, out_scale=out_scale),
        grid_spec=pltpu.PrefetchScalarGridSpec(
            num_scalar_prefetch=1,
            grid=(bsz, ns + 1),
            in_specs=in_specs,
            out_specs=pl.BlockSpec((1, n_q, n_h * dv), lambda b, s, pt: (b, 0, 0)),
            scratch_shapes=[pltpu.VMEM((1, rows, 1), F32), pltpu.VMEM((1, rows, 1), F32),
                            pltpu.VMEM((1, rows, acc_w), F32)]),
        out_shape=jax.ShapeDtypeStruct((bsz, n_q, n_h * dv), BF16),
        compiler_params=_cparams(("parallel", "arbitrary")),
        name="paged_attn_" + mode,
    )(page_table, *args)


def _s5_kernel(u_ref, h0r_ref, h0i_ref, bre_ref, bim_ref, cre_ref, cim_ref, alr_ref, ali_ref,
               acr_ref, aci_ref, d_ref, wglu_ref, o_ref, sr_ref, si_ref,
               hr_ref, hi_ref, cr_ref, ci_ref):
    t = pl.program_id(1)
    ln = u_ref.shape[1]

    @pl.when(t == 0)
    def _():
        cr_ref[...] = h0r_ref[0]
        ci_ref[...] = h0i_ref[0]

    u = u_ref[0]
    ub = u.astype(BF16)
    xr = _dot(ub, bre_ref[...])
    xi = _dot(ub, bim_ref[...])
    row = lax.broadcasted_iota(I32, (ln, 1), 0) % SUBLANES
    for kk, sh in enumerate((1, 2, 4)):
        ar = alr_ref[kk:kk + 1, :]
        ai = ali_ref[kk:kk + 1, :]
        pr = pltpu.roll(xr, sh, 0)
        pi = pltpu.roll(xi, sh, 0)
        keep = row >= sh
        xr, xi = (xr + jnp.where(keep, ar * pr - ai * pi, 0.0),
                  xi + jnp.where(keep, ar * pi + ai * pr, 0.0))
    hr_ref[...] = xr
    hi_ref[...] = xi
    acr = acr_ref[...]
    aci = aci_ref[...]

    def group(r, carry):
        cr, ci = carry
        sl = pl.ds(pl.multiple_of(r * SUBLANES, SUBLANES), SUBLANES)
        br = hr_ref[sl, :] + acr * cr - aci * ci
        bi = hi_ref[sl, :] + acr * ci + aci * cr
        hr_ref[sl, :] = br
        hi_ref[sl, :] = bi
        return br[SUBLANES - 1:SUBLANES, :], bi[SUBLANES - 1:SUBLANES, :]

    cr, ci = lax.fori_loop(0, ln // SUBLANES, group, (cr_ref[...], ci_ref[...]))
    cr_ref[...] = cr
    ci_ref[...] = ci
    y = _dot(hr_ref[...].astype(BF16), cre_ref[...]) - _dot(hi_ref[...].astype(BF16), cim_ref[...])
    y = _gelu(y + d_ref[...] * u)
    z = _dot(y.astype(BF16), wglu_ref[...])
    o_ref[0] = (y * _sigmoid(z)).astype(o_ref.dtype)
    sr_ref[0] = cr
    si_ref[0] = ci


def _s5_mixer(u, h0r, h0i, prm):
    bsz, t, ch = u.shape
    n_state = h0r.shape[-1]
    ln = min(t, 256)
    const = lambda shape: pl.BlockSpec(shape, lambda b, i: (0, 0))
    state = pl.BlockSpec((1, 1, n_state), lambda b, i: (b, 0, 0))
    return pl.pallas_call(
        _s5_kernel,
        grid=(bsz, t // ln),
        in_specs=[pl.BlockSpec((1, ln, ch), lambda b, i: (b, i, 0)), state, state,
                  const((ch, n_state)), const((ch, n_state)),
                  const((n_state, ch)), const((n_state, ch)),
                  const((SUBLANES, n_state)), const((SUBLANES, n_state)),
                  const((SUBLANES, n_state)), const((SUBLANES, n_state)),
                  const((1, ch)), const((ch, ch))],
        out_specs=[pl.BlockSpec((1, ln, ch), lambda b, i: (b, i, 0)), state, state],
        out_shape=[jax.ShapeDtypeStruct((bsz, t, ch), BF16),
                   jax.ShapeDtypeStruct((bsz, 1, n_state), F32),
                   jax.ShapeDtypeStruct((bsz, 1, n_state), F32)],
        scratch_shapes=[pltpu.VMEM((ln, n_state), F32), pltpu.VMEM((ln, n_state), F32),
                        pltpu.VMEM((1, n_state), F32), pltpu.VMEM((1, n_state), F32)],
        compiler_params=_cparams(("parallel", "arbitrary")),
        name="s5_mixer",
    )(u, h0r, h0i, prm["bre"], prm["bim"], prm["cre"], prm["cim"],
      prm["alr"], prm["ali"], prm["acr"], prm["aci"], prm["d"], prm["wglu"])


def _s5_params(a_re, a_im, log_dt, b_re, b_im, c_re, c_im, d, w_glu):
    g, p = a_re.shape
    c = b_re.shape[-1]
    dt = jnp.exp(log_dt)[:, None]
    mag = jnp.exp(a_re * dt)
    ar = mag * jnp.cos(a_im * dt)
    ai = mag * jnp.sin(a_im * dt)
    den = a_re * a_re + a_im * a_im
    fr = ((ar - 1.0) * a_re + ai * a_im) / den
    fi = (ai * a_re - (ar - 1.0) * a_im) / den
    bbr = fr[..., None] * b_re - fi[..., None] * b_im
    bbi = fr[..., None] * b_im + fi[..., None] * b_re
    eye = jnp.eye(g, dtype=F32)

    def in_proj(x):
        return jnp.einsum("gpc,gh->gchp", x, eye).reshape(g * c, g * p).astype(BF16)

    def out_proj(x):
        return jnp.einsum("gcp,gh->gphc", x, eye).reshape(g * p, g * c).astype(BF16)

    def powers(n_list):
        rs, is_ = [], []
        for n in n_list:
            m = jnp.exp(a_re * dt * n)
            rs.append((m * jnp.cos(a_im * dt * n)).reshape(1, g * p))
            is_.append((m * jnp.sin(a_im * dt * n)).reshape(1, g * p))
        pad = SUBLANES - len(n_list)
        if pad:
            rs += [jnp.zeros((pad, g * p), F32)]
            is_ += [jnp.zeros((pad, g * p), F32)]
        return jnp.concatenate(rs, axis=0), jnp.concatenate(is_, axis=0)

    alr, ali = powers([1, 2, 4])
    acr, aci = powers(list(range(1, SUBLANES + 1)))
    return {"bre": in_proj(bbr), "bim": in_proj(bbi), "cre": out_proj(c_re), "cim": out_proj(c_im),
            "alr": alr, "ali": ali, "acr": acr, "aci": aci,
            "d": d.reshape(1, g * c), "wglu": w_glu.astype(BF16)}


def _xattn_kernel(hn_ref, wq_ref, mkv_ref, o_ref, *, n_h):
    qx = _dot(hn_ref[0], wq_ref[...])
    mkv = mkv_ref[0]
    hw = n_h * DH_X
    scale = DH_X ** -0.5
    for h in range(n_h):
        qh = qx[:, h * DH_X:(h + 1) * DH_X].astype(BF16)
        kh = mkv[:, h * DH_X:(h + 1) * DH_X].astype(BF16)
        vh = mkv[:, hw + h * DH_X:hw + (h + 1) * DH_X].astype(BF16)
        s = _dot_nt(qh, kh) * scale
        m = jnp.max(s, axis=-1, keepdims=True)
        p = jnp.exp(s - m)
        l = jnp.sum(p, axis=-1, keepdims=True)
        o = _dot(p.astype(BF16), vh) / l
        o_ref[0, :, h * DH_X:(h + 1) * DH_X] = o.astype(o_ref.dtype)


def _xattn(hn, w_qx, mem_kv, n_h):
    bsz, t, d = hn.shape
    n_mem = mem_kv.shape[1]
    hw = n_h * DH_X
    tq = min(t, 256)
    return pl.pallas_call(
        functools.partial(_xattn_kernel, n_h=n_h),
        grid=(bsz, t // tq),
        in_specs=[pl.BlockSpec((1, tq, d), lambda b, i: (b, i, 0)),
                  pl.BlockSpec((d, hw), lambda b, i: (0, 0)),
                  pl.BlockSpec((1, n_mem, 2 * hw), lambda b, i: (b, 0, 0))],
        out_specs=pl.BlockSpec((1, tq, hw), lambda b, i: (b, i, 0)),
        out_shape=jax.ShapeDtypeStruct((bsz, t, hw), BF16),
        compiler_params=_cparams(("parallel", "parallel")),
        name="xattn",
    )(hn, w_qx, mem_kv)


def _block_diag_q(q, n_vh, scale):
    bsz, n_q, _ = q.shape
    qh = (q.astype(F32) * scale).reshape(bsz, n_q, n_vh, DH)
    eye = jnp.eye(n_vh, dtype=F32)
    out = jnp.einsum("bqhd,hg->bhqgd", qh, eye)
    return out.reshape(bsz, n_vh * n_q, n_vh * DH).astype(BF16)


def _map_diag_q(q, n_h, scale):
    bsz, n_q, _ = q.shape
    qh = (q.astype(F32) * scale).reshape(bsz, n_q, n_h, 2, DH)
    eye = jnp.eye(2, dtype=F32)
    out = jnp.einsum("bqhjd,jg->bhjqgd", qh, eye)
    return out.reshape(bsz, n_h * 2 * n_q, 2 * DH).astype(BF16)


def kernel(x_prompt, x_sample, mem_prompt, cache_a_kv, cache_a_idx_k, cache_c_kv, cache_d_kv,
           cache_d_logf, cache_mem_kv, state_s5_re, state_s5_im, state_ffn_conv, page_table,
           rel_bias, g_mix_pre, w_in, s5_a_re, s5_a_im, s5_log_dt, s5_b_re, s5_b_im, s5_c_re,
           s5_c_im, s5_d, s5_w_glu, lam_q1, lam_k1, lam_q2, lam_k2, diff_sub_g, fox_b_f, w_out,
           g_mix_post, g_x_pre, w_qx, w_kvx, w_ox, g_x_post, g_ffn_pre, w_gate, w_up, conv_w,
           conv_b, w_down, g_ffn_post):
    bp, t, d = x_prompt.shape
    bs, ts, _ = x_sample.shape
    depth = w_in.shape[0]
    n_pool = cache_a_kv.shape[1]
    n_pages = page_table.shape[1]
    past = n_pages * PAGE
    h_a = cache_a_kv.shape[4]
    h_c = cache_c_kv.shape[4]
    h_d = cache_d_kv.shape[4]
    h_x = cache_mem_kv.shape[4]
    n_mem = cache_mem_kv.shape[2]
    s5_g, s5_p = s5_a_re.shape[1:]
    s5_ch = s5_g * S5_GROUP
    n_state = s5_g * s5_p
    d_ff = w_gate.shape[2]
    wa, wc, wd = h_a * DH, h_c * 2 * DH, h_d * DH
    wqi = H_IDX * D_IDX
    assert wa == wc == wd == s5_ch and ts == SUBLANES and past % (PAGES_PER_STEP * PAGE) == 0
    pp = PAGES_PER_STEP
    ns = n_pages // pp
    topk_p = min(TOPK_MAX, t // 4)
    topk_s = min(TOPK_MAX, (past + ts) // 4)

    sizes = (wa, wa, wa, wqi, D_IDX, H_IDX, s5_ch, wc, wc, wc, wd, wd, wd, h_d)
    offs = np.concatenate([[0], np.cumsum(sizes)]).tolist()
    (o_aq, o_ak, o_av, o_aqi, o_aki, o_awi, o_bu, o_cq, o_ck, o_cv, o_dq, o_dk, o_dv, o_df) = offs[:-1]
    col_wi, col_df = D_IDX, D_IDX + H_IDX

    tq = min(t, ATT_BLK)
    bias_p = jnp.stack([_bias_tile(rel_bias, tq, tq, tq, keys_on_rows=True),
                        _bias_tile(rel_bias, 0, tq, tq, keys_on_rows=True)], axis=1)
    bias_s_prev = _bias_tile(rel_bias, PAGE, ts, PAGE)
    bias_s_new = _bias_tile(rel_bias, 0, ts, PAGE)
    far = rel_bias[NUM_BUCKETS - 1]
    causal_new = jnp.where(jnp.arange(PAGE)[None, :] <= jnp.arange(ts)[:, None], 0.0, NEG).astype(F32)

    def sample_tiles(h0, n_h, rep):
        hs = [h0 + i // rep for i in range(n_h * rep)]
        farc = jnp.stack([jnp.full((ts, 1), 1.0, F32) * far[h] for h in hs]).reshape(-1, 1)
        prev = jnp.stack([bias_s_prev[h] for h in hs]).reshape(-1, PAGE) - farc
        new = jnp.stack([bias_s_new[h] + causal_new for h in hs]).reshape(-1, PAGE) - farc
        return jnp.broadcast_to(farc[None], (bs,) + farc.shape), prev, new

    rowc_a, bprev_a, bnew_a = sample_tiles(0, h_a, 1)
    rowc_c, bprev_c, bnew_c = sample_tiles(h_a, h_c, 2)
    bprev_d = jnp.zeros((h_d * ts, PAGE), F32)
    bnew_d = jnp.tile(causal_new, (h_d, 1))

    key_minor = lambda c: jnp.moveaxis(c, 2, -1)
    pool_a = key_minor(cache_a_kv).reshape(depth, n_pool, 2 * wa, PAGE)
    pool_d = key_minor(cache_d_kv).reshape(depth, n_pool, 2 * wd, PAGE)
    pool_c = cache_c_kv.reshape(depth, n_pool, PAGE * 2 * h_c, 2 * DH)
    pool_ki = key_minor(cache_a_idx_k)
    pool_lf = key_minor(cache_d_logf)

    xp = x_prompt.reshape(bp * t, d)
    xs = x_sample.reshape(bs * ts, d)
    mem_bf = mem_prompt.reshape(bp * n_mem, d).astype(BF16)
    hn_p = _norm_cast(xp, g_mix_pre[0])
    hn_s = _norm_cast(xs, g_mix_pre[0])

    outs = {k: [] for k in ("a_kv_p", "a_kv_s", "a_ki_p", "a_ki_s", "c_kv_p", "c_kv_s", "d_kv_p",
                            "d_kv_s", "lf_p", "lf_s", "mem_kv", "s5r_p", "s5r_s", "s5i_p", "s5i_s",
                            "cs_p", "cs_s")}

    for l in range(depth):
        wl = w_in[l]
        w_akv = wl[:, o_ak:o_aqi].astype(BF16)
        w_ckv = wl[:, o_ck:o_dq].astype(BF16)
        w_dkv = wl[:, o_dk:o_df].astype(BF16)
        w_q = jnp.concatenate([wl[:, o_aqi:o_aki], wl[:, o_aq:o_ak], wl[:, o_cq:o_ck],
                               wl[:, o_dq:o_dk]], axis=1).astype(BF16)
        w_bu = wl[:, o_bu:o_cq].astype(BF16)
        w_small = jnp.concatenate([wl[:, o_aki:o_bu], wl[:, o_df:],
                                   jnp.zeros((d, LANES - D_IDX - H_IDX - h_d), F32)], axis=1).astype(BF16)
        qblk_a, qblk_c, qblk_d = wqi // wa, wqi // wa + 1, wqi // wa + 2
        wt = jnp.swapaxes(wl, 0, 1)
        wt_akv = wt[o_ak:o_aqi].astype(BF16)
        wt_dkv = wt[o_dk:o_df].astype(BF16)
        wt_cv = wt[o_cv:o_dq].astype(BF16)
        wt_q = jnp.concatenate([wt[o_aqi:o_aki], wt[o_aq:o_ak], wt[o_cq:o_ck], wt[o_dq:o_dk]],
                               axis=0).astype(BF16)
        wt_small = jnp.concatenate([wt[o_aki:o_bu], wt[o_df:],
                                    jnp.zeros((LANES - D_IDX - H_IDX - h_d, d), F32)], axis=0).astype(BF16)
        w_keys = jnp.concatenate([wl[:, o_ak:o_av], wl[:, o_ck:o_cv], wl[:, o_dk:o_dv]], axis=1).astype(BF16)
        lam_init = 0.8 - 0.6 * math.exp(-0.3 * l)
        lam = (jnp.exp(jnp.sum(lam_q1[l] * lam_k1[l])) - jnp.exp(jnp.sum(lam_q2[l] * lam_k2[l]))
               + lam_init).reshape(1)
        bvec = jnp.zeros((1, LANES), F32).at[0, col_df:col_df + h_d].set(fox_b_f[l])
        s5p = _s5_params(s5_a_re[l], s5_a_im[l], s5_log_dt[l], s5_b_re[l], s5_b_im[l],
                         s5_c_re[l], s5_c_im[l], s5_d[l], s5_w_glu[l])
        w_out_b = w_out[l].astype(BF16)
        w_qx_b = w_qx[l].astype(BF16)
        w_ox_b = w_ox[l].astype(BF16)
        w_gate_b = w_gate[l].astype(BF16)
        w_up_b = w_up[l].astype(BF16)
        w_down_b = w_down[l].astype(BF16)
        g_next = g_mix_pre[l + 1] if l + 1 < depth else None

        mkv_p = _matmul(mem_bf, w_kvx[l].astype(BF16), F32).reshape(bp, n_mem, 2 * h_x * DH_X)
        outs["mem_kv"].append(mkv_p)

        def mixer_inputs(hn):
            return (_matmul(hn, w_akv, F32), _matmul(hn, w_ckv, F32), _matmul(hn, w_dkv, F32),
                    _matmul(hn, w_q, BF16), _matmul(hn, w_bu, F32), _matmul(hn, w_small, F32))

        hn3 = hn_p.reshape(bp, t, d)
        a_kv_t, a_vt_b = _matmul_nt(wt_akv, hn3, F32, True)
        d_kv_t, d_vt_b = _matmul_nt(wt_dkv, hn3, F32, True)
        (c_vt_b,) = _matmul_nt(wt_cv, hn3, None, True)
        (q_t,) = _matmul_nt(wt_q, hn3, BF16, False)
        (small_t,) = _matmul_nt(wt_small, hn3, F32, False)
        kk = _matmul(hn_p, w_keys, BF16).reshape(bp, t, -1)
        c_kv3 = _matmul(hn_p, w_ckv, F32).reshape(bp, t, 2 * wc)
        b_u = _matmul(hn_p, w_bu, F32)
        small3 = _matmul(hn_p, w_small, F32).reshape(bp, t, LANES)
        _, c3 = _logsig_cumsum(small3, bvec)
        lf_t, c_t = _logsig_cumsum_t(small_t, fox_b_f[l].reshape(h_d, 1), col_df)
        o_a = _dsa_prompt(q_t, qblk_a, small_t, small3, kk, 0, a_vt_b, bias_p[:h_a], far[:h_a],
                          h_a, topk_p, col_wi)
        o_b, s5r, s5i = _s5_mixer(b_u.reshape(bp, t, s5_ch), jnp.zeros((bp, 1, n_state), F32),
                                  jnp.zeros((bp, 1, n_state), F32), s5p)
        o_c = _diff_prompt(q_t, qblk_c, kk, 1, c_vt_b, bias_p[h_a:], far[h_a:], lam, diff_sub_g[l], h_c,
                           1.0 - lam_init)
        o_d = _fox_prompt(q_t, qblk_d, kk, 2, d_vt_b, c3, c_t, h_d, col_df)
        mix = [o.reshape(bp * t, -1) for o in (o_a, o_b, o_c, o_d)]
        xp, hn = _proj_res(mix, w_out_b, xp, g_mix_post[l], g_x_pre[l])
        ox = _xattn(hn.reshape(bp, t, d), w_qx_b, mkv_p, h_x)
        xp, hn = _proj_res([ox.reshape(bp * t, -1)], w_ox_b, xp, g_x_post[l], g_ffn_pre[l])
        hid, cs = _ffn_hidden(hn, w_gate_b, w_up_b, conv_w[l], conv_b[l],
                              jnp.zeros((bp, CONV_W - 1, d_ff), F32), t)
        xp, hn_p = _proj_res_ktiled(hid, w_down_b, xp, g_ffn_post[l], g_next)
        outs["a_kv_p"].append(a_kv_t)
        outs["a_ki_p"].append(small_t[:, 0:D_IDX, :])
        outs["c_kv_p"].append(c_kv3)
        outs["d_kv_p"].append(d_kv_t)
        outs["lf_p"].append(lf_t)
        outs["s5r_p"].append(s5r)
        outs["s5i_p"].append(s5i)
        outs["cs_p"].append(cs)

        a_kv, c_kv, d_kv, q_all, b_u, small = mixer_inputs(hn_s)
        a_kv3 = a_kv.reshape(bs, ts, 2 * wa)
        c_kv3 = c_kv.reshape(bs, ts, 2 * wc)
        d_kv3 = d_kv.reshape(bs, ts, 2 * wd)
        q3 = q_all.reshape(bs, ts, -1)
        small3 = small.reshape(bs, ts, LANES)
        lf3, c3 = _logsig_cumsum(small3, bvec)
        keys_on_lanes = lambda x: jnp.pad(jnp.swapaxes(x, 1, 2), ((0, 0), (0, 0), (0, PAGE - ts)))
        keys_on_rows = lambda x: jnp.pad(x, ((0, 0), (0, PAGE - ts), (0, 0)))

        qi_rows = jnp.swapaxes(q3[:, :, 0:wqi].reshape(bs, ts, H_IDX, D_IDX), 1, 2)
        qi_rows = qi_rows.reshape(bs, H_IDX * ts, D_IDX)
        w_rows = jnp.swapaxes(small3[:, :, col_wi:col_wi + H_IDX], 1, 2).reshape(bs, H_IDX * ts, 1)
        w_rows = w_rows * (1.0 / (math.sqrt(H_IDX) * math.sqrt(D_IDX)))
        amask = _dsa_index_sample(qi_rows, w_rows, pool_ki, l, page_table,
                                  keys_on_lanes(small3[:, :, 0:D_IDX]), topk_s, ts)
        o_a = _paged_attn("dsa", _block_diag_q(q3[:, :, wqi:wqi + wa], h_a, 0.125), pool_a, l,
                          page_table, keys_on_lanes(a_kv3), amask, amask[:, ns, :, 0:PAGE],
                          rowc_a, bprev_a, bnew_a, ts, h_a, DH)
        o_b, s5r, s5i = _s5_mixer(b_u.reshape(bs, ts, s5_ch),
                                  state_s5_re[l].reshape(bs, 1, n_state),
                                  state_s5_im[l].reshape(bs, 1, n_state), s5p)
        o_c = _paged_attn("diff", _map_diag_q(q3[:, :, wqi + wa:wqi + wa + wc], h_c, 0.125),
                          pool_c, l, page_table,
                          keys_on_rows(c_kv3).reshape(bs, PAGE * 2 * h_c, 2 * DH), None, None,
                          rowc_c, bprev_c, bnew_c,
                          ts, h_c, 2 * DH, lam=lam, gsub=diff_sub_g[l], out_scale=1.0 - lam_init)
        c_new = c3[:, :, col_df:col_df + h_d]
        nck_past = _page_suffix(pool_lf, l, page_table)
        nck_new = -keys_on_lanes(c_new)
        rowc_d = jnp.swapaxes(c_new, 1, 2).reshape(bs, h_d * ts, 1)
        o_d = _paged_attn("fox", _block_diag_q(q3[:, :, wqi + wa + wc:], h_d, 0.125), pool_d, l,
                          page_table, keys_on_lanes(d_kv3), nck_past, nck_new,
                          rowc_d, bprev_d, bnew_d, ts, h_d, DH)
        mix = [o.reshape(bs * ts, -1) for o in (o_a, o_b, o_c, o_d)]
        xs, hn = _proj_res(mix, w_out_b, xs, g_mix_post[l], g_x_pre[l])
        mkv_s = cache_mem_kv[l].reshape(bs, n_mem, 2 * h_x * DH_X)
        ox = _xattn(hn.reshape(bs, ts, d), w_qx_b, mkv_s, h_x)
        xs, hn = _proj_res([ox.reshape(bs * ts, -1)], w_ox_b, xs, g_x_post[l], g_ffn_pre[l])
        hid, cs = _ffn_hidden(hn, w_gate_b, w_up_b, conv_w[l], conv_b[l], state_ffn_conv[l], ts)
        xs, hn_s = _proj_res_ktiled(hid, w_down_b, xs, g_ffn_post[l], g_next)
        outs["a_kv_s"].append(a_kv3)
        outs["a_ki_s"].append(small3[:, :, 0:D_IDX])
        outs["c_kv_s"].append(c_kv3)
        outs["d_kv_s"].append(d_kv3)
        outs["lf_s"].append(lf3[:, :, col_df:col_df + h_d])
        outs["s5r_s"].append(s5r)
        outs["s5i_s"].append(s5i)
        outs["cs_s"].append(cs)

    st = {k: jnp.stack(v) for k, v in outs.items()}
    token_major = lambda x: jnp.moveaxis(x, -1, 2)
    return (xp.reshape(bp, t, d), xs.reshape(bs, ts, d),
            token_major(st["a_kv_p"].reshape(depth, bp, 2, h_a, DH, t)),
            st["a_kv_s"].reshape(depth, bs, ts, 2, h_a, DH),
            token_major(st["a_ki_p"]), st["a_ki_s"],
            st["c_kv_p"].reshape(depth, bp, t, 2, h_c, 2 * DH),
            st["c_kv_s"].reshape(depth, bs, ts, 2, h_c, 2 * DH),
            token_major(st["d_kv_p"].reshape(depth, bp, 2, h_d, DH, t)),
            st["d_kv_s"].reshape(depth, bs, ts, 2, h_d, DH),
            token_major(st["lf_p"]), st["lf_s"],
            st["mem_kv"].reshape(depth, bp, n_mem, 2, h_x, DH_X),
            st["s5r_p"].reshape(depth, bp, s5_g, s5_p), st["s5r_s"].reshape(depth, bs, s5_g, s5_p),
            st["s5i_p"].reshape(depth, bp, s5_g, s5_p), st["s5i_s"].reshape(depth, bs, s5_g, s5_p),
            st["cs_p"], st["cs_s"])
```

```python
import functools
import math

import numpy as np
import jax
import jax.numpy as jnp
from jax import lax
from jax.experimental import pallas as pl
from jax.experimental.pallas import tpu as pltpu

F32 = jnp.float32
BF16 = jnp.bfloat16
I32 = jnp.int32

EPS = 1e-6
NEG = -1e30
M_INIT = -3e38
INT_MIN = -2 ** 31
BIG_IDX = 2 ** 30

DH = 64
H_IDX = 16
D_IDX = 64
TOPK_MAX = 256
S5_GROUP = 16
S5_P = 64
DH_X = 128
NUM_BUCKETS = 32
MAX_DISTANCE = 128
CONV_W = 3
PAGE = 128

LANES = 128
SUBLANES = 8
VMEM_LIMIT = 52 * 1024 * 1024

ATT_BLK = 256
PAGES_PER_STEP = 16
KTILE_BYTES = 6 * 1024 * 1024


def _t5_thresholds():
    exact = NUM_BUCKETS // 2
    n = np.arange(exact, MAX_DISTANCE + 1).astype(np.float64)
    large = exact + np.floor(np.log(n / exact) / math.log(MAX_DISTANCE / exact)
                             * (NUM_BUCKETS - exact)).astype(np.int64)
    bucket = np.minimum(large, NUM_BUCKETS - 1)
    return [int(n[np.argmax(bucket >= b)]) for b in range(exact + 1, NUM_BUCKETS)]


_T5_THR = _t5_thresholds()


def _cparams(sem):
    return pltpu.CompilerParams(dimension_semantics=sem, vmem_limit_bytes=VMEM_LIMIT)


def _dot(a, b):
    return jnp.dot(a, b, preferred_element_type=F32)


def _dot_nt(a, b):
    return lax.dot_general(a, b, (((1,), (1,)), ((), ())), preferred_element_type=F32)


def _rms(x, g):
    y = x * lax.rsqrt(jnp.mean(x * x, axis=-1, keepdims=True) + EPS)
    return y * g


def _gelu(x):
    c = math.sqrt(2.0 / math.pi)
    return 0.5 * x * (1.0 + jnp.tanh(c * (x + 0.044715 * (x * x * x))))


def _sigmoid(x):
    return 1.0 / (1.0 + jnp.exp(-x))


def _log_sigmoid(x):
    return jnp.minimum(x, 0.0) - jnp.log(1.0 + jnp.exp(-jnp.abs(x)))


def _norm_cast_kernel(x_ref, g_ref, o_ref):
    o_ref[...] = _rms(x_ref[...], g_ref[...]).astype(o_ref.dtype)


def _norm_cast(x, g):
    m, d = x.shape
    tm = min(m, 512)
    return pl.pallas_call(
        _norm_cast_kernel,
        grid=(m // tm,),
        in_specs=[pl.BlockSpec((tm, d), lambda i: (i, 0)),
                  pl.BlockSpec((1, d), lambda i: (0, 0))],
        out_specs=pl.BlockSpec((tm, d), lambda i: (i, 0)),
        out_shape=jax.ShapeDtypeStruct((m, d), BF16),
        compiler_params=_cparams(("parallel",)),
        name="norm_cast",
    )(x, g.reshape(1, d))


def _mm_kernel(a_ref, w_ref, o_ref):
    o_ref[...] = _dot(a_ref[...], w_ref[...]).astype(o_ref.dtype)


def _matmul(a, w, out_dtype):
    m, k = a.shape
    n = w.shape[1]
    tm = min(m, 1024)
    tn = min(n, 512)
    return pl.pallas_call(
        _mm_kernel,
        grid=(m // tm, n // tn),
        in_specs=[pl.BlockSpec((tm, k), lambda i, j: (i, 0)),
                  pl.BlockSpec((k, tn), lambda i, j: (0, j))],
        out_specs=pl.BlockSpec((tm, tn), lambda i, j: (i, j)),
        out_shape=jax.ShapeDtypeStruct((m, n), out_dtype),
        compiler_params=_cparams(("parallel", "parallel")),
        name="matmul",
    )(a, w)


def _mm_nt_kernel(w_ref, a_ref, *o_refs, plain, blocked):
    acc = _dot_nt(w_ref[...], a_ref[0])
    k = 0
    if plain:
        o_refs[k][0] = acc.astype(o_refs[k].dtype)
        k += 1
    if blocked:
        ob = o_refs[k]
        tk = ob.shape[3]
        for i in range(ob.shape[1]):
            ob[0, i] = acc[:, i * tk:(i + 1) * tk].astype(ob.dtype)


def _matmul_nt(w_t, a3, plain_dtype, blocked):
    n, k = w_t.shape
    bsz, t, _ = a3.shape
    tm = min(t, 1024)
    tn = min(n, 512)
    tk = min(t, ATT_BLK)
    out_specs, out_shape = [], []
    if plain_dtype is not None:
        out_specs.append(pl.BlockSpec((1, tn, tm), lambda b, i, j: (b, j, i)))
        out_shape.append(jax.ShapeDtypeStruct((bsz, n, t), plain_dtype))
    if blocked:
        out_specs.append(pl.BlockSpec((1, tm // tk, tn, tk), lambda b, i, j: (b, i, j, 0)))
        out_shape.append(jax.ShapeDtypeStruct((bsz, t // tk, n, tk), BF16))
    return pl.pallas_call(
        functools.partial(_mm_nt_kernel, plain=plain_dtype is not None, blocked=blocked),
        grid=(bsz, t // tm, n // tn),
        in_specs=[pl.BlockSpec((tn, k), lambda b, i, j: (j, 0)),
                  pl.BlockSpec((1, tm, k), lambda b, i, j: (b, i, 0))],
        out_specs=out_specs,
        out_shape=out_shape,
        compiler_params=_cparams(("parallel", "parallel", "parallel")),
        name="matmul_nt",
    )(w_t, a3)


def _proj_res_kernel(*refs, n_in, emit_next):
    a_refs = refs[:n_in]
    w_ref, x_ref, gp_ref, gn_ref = refs[n_in:n_in + 4]
    outs = refs[n_in + 4:]
    off = 0
    acc = None
    for a_ref in a_refs:
        kk = a_ref.shape[1]
        part = _dot(a_ref[...], w_ref[off:off + kk, :])
        acc = part if acc is None else acc + part
        off += kk
    xn = x_ref[...] + _rms(acc, gp_ref[...])
    outs[0][...] = xn
    if emit_next:
        outs[1][...] = _rms(xn, gn_ref[...]).astype(BF16)


def _proj_res(a_list, w, x, g_post, g_next):
    m, d = x.shape
    k = w.shape[0]
    tm = min(m, 256)
    emit_next = g_next is not None
    gn = g_next if emit_next else g_post
    in_specs = [pl.BlockSpec((tm, a.shape[1]), lambda i: (i, 0)) for a in a_list]
    in_specs += [pl.BlockSpec((k, d), lambda i: (0, 0)),
                 pl.BlockSpec((tm, d), lambda i: (i, 0)),
                 pl.BlockSpec((1, d), lambda i: (0, 0)),
                 pl.BlockSpec((1, d), lambda i: (0, 0))]
    out_specs = [pl.BlockSpec((tm, d), lambda i: (i, 0))]
    out_shape = [jax.ShapeDtypeStruct((m, d), F32)]
    if emit_next:
        out_specs.append(pl.BlockSpec((tm, d), lambda i: (i, 0)))
        out_shape.append(jax.ShapeDtypeStruct((m, d), BF16))
    res = pl.pallas_call(
        functools.partial(_proj_res_kernel, n_in=len(a_list), emit_next=emit_next),
        grid=(m // tm,),
        in_specs=in_specs,
        out_specs=out_specs,
        out_shape=out_shape,
        compiler_params=_cparams(("parallel",)),
        name="proj_res",
    )(*a_list, w, x, g_post.reshape(1, d), gn.reshape(1, d))
    return res[0], (res[1] if emit_next else None)


def _proj_res_kt_kernel(a_ref, w_ref, x_ref, gp_ref, gn_ref, *rest, emit_next):
    if emit_next:
        xo_ref, hn_ref, acc_ref = rest
    else:
        xo_ref, acc_ref = rest
    kk = pl.program_id(1)

    @pl.when(kk == 0)
    def _():
        acc_ref[...] = jnp.zeros_like(acc_ref)

    acc_ref[...] += _dot(a_ref[...], w_ref[...])

    @pl.when(kk == pl.num_programs(1) - 1)
    def _():
        xn = x_ref[...] + _rms(acc_ref[...], gp_ref[...])
        xo_ref[...] = xn
        if emit_next:
            hn_ref[...] = _rms(xn, gn_ref[...]).astype(BF16)


def _proj_res_ktiled(a, w, x, g_post, g_next):
    m, d = x.shape
    k = w.shape[0]
    tm = min(m, 512)
    tk = max(c for c in range(LANES, k + 1, LANES) if k % c == 0 and c * d * 2 <= KTILE_BYTES)
    emit_next = g_next is not None
    gn = g_next if emit_next else g_post
    out_specs = [pl.BlockSpec((tm, d), lambda i, j: (i, 0))]
    out_shape = [jax.ShapeDtypeStruct((m, d), F32)]
    if emit_next:
        out_specs.append(pl.BlockSpec((tm, d), lambda i, j: (i, 0)))
        out_shape.append(jax.ShapeDtypeStruct((m, d), BF16))
    res = pl.pallas_call(
        functools.partial(_proj_res_kt_kernel, emit_next=emit_next),
        grid=(m // tm, k // tk),
        in_specs=[pl.BlockSpec((tm, tk), lambda i, j: (i, j)),
                  pl.BlockSpec((tk, d), lambda i, j: (j, 0)),
                  pl.BlockSpec((tm, d), lambda i, j: (i, 0)),
                  pl.BlockSpec((1, d), lambda i, j: (0, 0)),
                  pl.BlockSpec((1, d), lambda i, j: (0, 0))],
        out_specs=out_specs,
        out_shape=out_shape,
        scratch_shapes=[pltpu.VMEM((tm, d), F32)],
        compiler_params=_cparams(("parallel", "arbitrary")),
        name="proj_res_ktiled",
    )(a, w, x, g_post.reshape(1, d), gn.reshape(1, d))
    return res[0], (res[1] if emit_next else None)


def _ffn_hidden_kernel(hn_ref, wg_ref, wu_ref, cw_ref, cb_ref, hb1_ref, hb2_ref,
                       h_ref, cs_ref, *, seq_len, n_seq):
    hn = hn_ref[...]
    g = _dot(hn, wg_ref[...])
    u = _dot(hn, wu_ref[...])
    w0 = cw_ref[0:1, :]
    w1 = cw_ref[1:2, :]
    w2 = cw_ref[2:3, :]
    cb = cb_ref[...]
    if seq_len > SUBLANES:
        gc = cb + w0 * pltpu.roll(g, 2, 0) + w1 * pltpu.roll(g, 1, 0) + w2 * g
        h_ref[...] = (_gelu(gc) * u).astype(h_ref.dtype)
    row = lax.broadcasted_iota(I32, (SUBLANES, 1), 0)
    for s in range(n_seq):
        r0 = s * seq_len
        g8 = g[r0:r0 + SUBLANES, :]
        p1 = jnp.where(row < 1, hb1_ref[s], pltpu.roll(g8, 1, 0))
        p2 = jnp.where(row < 2, hb2_ref[s], pltpu.roll(g8, 2, 0))
        gc8 = cb + w0 * p2 + w1 * p1 + w2 * g8
        h_ref[r0:r0 + SUBLANES, :] = (_gelu(gc8) * u[r0:r0 + SUBLANES, :]).astype(h_ref.dtype)
        cs_ref[s] = g[r0 + seq_len - 2:r0 + seq_len, :]


def _ffn_hidden(hn, w_gate, w_up, conv_w, conv_b, buf, seq_len):
    m, d = hn.shape
    f = w_gate.shape[1]
    bsz = m // seq_len
    n_seq = 1 if seq_len > SUBLANES else bsz
    tt = seq_len * n_seq
    tf = 256
    zeros = jnp.zeros((bsz, SUBLANES - 2, f), F32)
    hb1 = jnp.concatenate([buf[:, 1:2], jnp.zeros((bsz, 1, f), F32), zeros], axis=1)
    hb2 = jnp.concatenate([buf, zeros], axis=1)
    h, cs = pl.pallas_call(
        functools.partial(_ffn_hidden_kernel, seq_len=seq_len, n_seq=n_seq),
        grid=(m // tt, f // tf),
        in_specs=[pl.BlockSpec((tt, d), lambda i, j: (i, 0)),
                  pl.BlockSpec((d, tf), lambda i, j: (0, j)),
                  pl.BlockSpec((d, tf), lambda i, j: (0, j)),
                  pl.BlockSpec((CONV_W, tf), lambda i, j: (0, j)),
                  pl.BlockSpec((1, tf), lambda i, j: (0, j)),
                  pl.BlockSpec((n_seq, SUBLANES, tf), lambda i, j: (i, 0, j)),
                  pl.BlockSpec((n_seq, SUBLANES, tf), lambda i, j: (i, 0, j))],
        out_specs=[pl.BlockSpec((tt, tf), lambda i, j: (i, j)),
                   pl.BlockSpec((n_seq, CONV_W - 1, tf), lambda i, j: (i, 0, j))],
        out_shape=[jax.ShapeDtypeStruct((m, f), BF16),
                   jax.ShapeDtypeStruct((bsz, CONV_W - 1, f), F32)],
        compiler_params=_cparams(("parallel", "parallel")),
        name="ffn_hidden",
    )(hn, w_gate, w_up, conv_w, conv_b.reshape(1, f), hb1, hb2)
    return h, cs


def _bias_tile_kernel(tab_ref, o_ref, *, off, keys_on_rows):
    nh, r, c = o_ref.shape
    i = lax.broadcasted_iota(I32, (r, c), 0)
    j = lax.broadcasted_iota(I32, (r, c), 1)
    n = jnp.maximum(off + (j - i if keys_on_rows else i - j), 0)
    large = jnp.full((r, c), NUM_BUCKETS // 2, I32)
    for thr in _T5_THR:
        large = large + jnp.where(n >= thr, 1, 0)
    bucket = jnp.where(n < NUM_BUCKETS // 2, n, large)

    def head(h, carry):
        val = jnp.full((r, c), tab_ref[NUM_BUCKETS - 1, h], F32)
        for b in range(NUM_BUCKETS - 2, -1, -1):
            val = jnp.where(bucket == b, tab_ref[b, h], val)
        o_ref[h] = val
        return carry
    lax.fori_loop(0, nh, head, 0)


def _bias_tile(rel_bias, off, r, c, keys_on_rows=False):
    nh = rel_bias.shape[1]
    return pl.pallas_call(
        functools.partial(_bias_tile_kernel, off=off, keys_on_rows=keys_on_rows),
        in_specs=[pl.BlockSpec(memory_space=pltpu.SMEM)],
        out_specs=pl.BlockSpec(memory_space=pltpu.VMEM),
        out_shape=jax.ShapeDtypeStruct((nh, r, c), F32),
        compiler_params=pltpu.CompilerParams(vmem_limit_bytes=VMEM_LIMIT),
        name="t5_bias_tile",
    )(rel_bias)


def _logsig_cumsum_kernel(x_ref, b_ref, lf_ref, c_ref):
    t = x_ref.shape[1]
    lf = _log_sigmoid(x_ref[0] + b_ref[...])
    lf_ref[0] = lf
    row = lax.broadcasted_iota(I32, (t, 1), 0)
    c = lf
    s = 1
    while s < t:
        c = c + jnp.where(row >= s, pltpu.roll(c, s, 0), 0.0)
        s *= 2
    c_ref[0] = c


def _logsig_cumsum(x, bvec):
    bsz, t, w = x.shape
    return pl.pallas_call(
        _logsig_cumsum_kernel,
        grid=(bsz,),
        in_specs=[pl.BlockSpec((1, t, w), lambda b: (b, 0, 0)),
                  pl.BlockSpec((1, w), lambda b: (0, 0))],
        out_specs=[pl.BlockSpec((1, t, w), lambda b: (b, 0, 0)),
                   pl.BlockSpec((1, t, w), lambda b: (b, 0, 0))],
        out_shape=[jax.ShapeDtypeStruct((bsz, t, w), F32),
                   jax.ShapeDtypeStruct((bsz, t, w), F32)],
        compiler_params=_cparams(("parallel",)),
        name="logsig_cumsum",
    )(x, bvec)


def _logsig_cumsum_t_kernel(x_ref, b_ref, lf_ref, c_ref, *, row0):
    n_h, t = lf_ref.shape[1:]
    lf = _log_sigmoid(x_ref[0, row0:row0 + n_h, :] + b_ref[...])
    lf_ref[0] = lf
    lane = lax.broadcasted_iota(I32, (1, t), 1)
    c = lf
    s = 1
    while s < t:
        c = c + jnp.where(lane >= s, pltpu.roll(c, s, 1), 0.0)
        s *= 2
    c_ref[0] = c


def _logsig_cumsum_t(x_t, bcol, row0):
    bsz, w, t = x_t.shape
    n_h = bcol.shape[0]
    return pl.pallas_call(
        functools.partial(_logsig_cumsum_t_kernel, row0=row0),
        grid=(bsz,),
        in_specs=[pl.BlockSpec((1, w, t), lambda b: (b, 0, 0)),
                  pl.BlockSpec((n_h, 1), lambda b: (0, 0))],
        out_specs=[pl.BlockSpec((1, n_h, t), lambda b: (b, 0, 0)),
                   pl.BlockSpec((1, n_h, t), lambda b: (b, 0, 0))],
        out_shape=[jax.ShapeDtypeStruct((bsz, n_h, t), F32),
                   jax.ShapeDtypeStruct((bsz, n_h, t), F32)],
        compiler_params=_cparams(("parallel",)),
        name="logsig_cumsum_t",
    )(x_t, bcol)


def _page_suffix_kernel(pt_ref, *refs):
    del pt_ref
    pp = PAGES_PER_STEP
    page_refs = refs[:pp]
    o_ref, carry_ref = refs[pp:]
    s = pl.program_id(1)

    @pl.when(s == 0)
    def _():
        carry_ref[...] = jnp.zeros_like(carry_ref)

    x0 = jnp.concatenate([page_refs[p][0, 0] for p in range(pp)], axis=1)
    w = x0.shape[1]
    lane = lax.broadcasted_iota(I32, (1, w), 1) % PAGE
    x = x0
    sh = 1
    while sh < PAGE:
        x = x + jnp.where(lane + sh < PAGE, pltpu.roll(x, w - sh, 1), 0.0)
        sh *= 2
    run = carry_ref[...]
    pieces = [None] * pp
    for p in reversed(range(pp)):
        sl = slice(p * PAGE, (p + 1) * PAGE)
        pieces[p] = (x[:, sl] - x0[:, sl]) + run
        run = run + jnp.sum(x0[:, sl], axis=1, keepdims=True)
    o_ref[0, 0] = jnp.concatenate(pieces, axis=1)
    carry_ref[...] = run


def _page_suffix(lf_pool_t, layer, page_table):
    n_h = lf_pool_t.shape[2]
    bsz, n_pages = page_table.shape
    pp = PAGES_PER_STEP
    ns = n_pages // pp

    def page_map(p):
        return lambda b, s, pt: (layer, pt[b, (ns - 1 - s) * pp + p], 0, 0)

    return pl.pallas_call(
        _page_suffix_kernel,
        grid_spec=pltpu.PrefetchScalarGridSpec(
            num_scalar_prefetch=1,
            grid=(bsz, ns),
            in_specs=[pl.BlockSpec((1, 1, n_h, PAGE), page_map(p)) for p in range(pp)],
            out_specs=pl.BlockSpec((1, 1, n_h, pp * PAGE), lambda b, s, pt: (b, ns - 1 - s, 0, 0)),
            scratch_shapes=[pltpu.VMEM((n_h, 1), F32)]),
        out_shape=jax.ShapeDtypeStruct((bsz, ns, n_h, pp * PAGE), F32),
        compiler_params=_cparams(("parallel", "arbitrary")),
        name="page_suffix",
    )(page_table, *([lf_pool_t] * pp))


def _init_flash(m_ref, l_ref, acc_ref):
    m_ref[...] = jnp.full(m_ref.shape, M_INIT, F32)
    l_ref[...] = jnp.zeros_like(l_ref)
    acc_ref[...] = jnp.zeros_like(acc_ref)


def _sortable(x):
    bits = lax.bitcast_convert_type(x + 0.0, I32)
    return jnp.where(bits < 0, bits ^ 0x7FFFFFFF, bits)


def _lane_fold(x):
    n = x.shape[1] // LANES
    acc = x[:, 0:LANES]
    for i in range(1, n):
        acc = acc + x[:, i * LANES:(i + 1) * LANES]
    return acc


def _topk_select(count_fn, shape, k, n_idx_bits):
    kf = float(k)

    def bit_body(i, ans):
        cand = ans + lax.shift_left(jnp.int32(1), 31 - i)
        cnt = count_fn(lambda key, idx: key >= cand)
        return jnp.where(cnt >= kf, cand, ans)

    thr = lax.fori_loop(0, 32, bit_body, jnp.full(shape, INT_MIN, I32))
    n_ge = count_fn(lambda key, idx: key >= thr)
    n_gt = count_fn(lambda key, idx: key > thr)
    need = kf - n_gt

    def tie_search():
        def idx_body(i, c):
            cand = c + lax.shift_left(jnp.int32(1), n_idx_bits - 1 - i)
            cnt = count_fn(lambda key, idx: (key == thr) & (idx < cand))
            return jnp.where(cnt < need, cand, c)
        return lax.fori_loop(0, n_idx_bits, idx_body, jnp.zeros(shape, I32))

    excess = jnp.max(n_ge - kf) > 0.0
    cut = lax.cond(excess, tie_search, lambda: jnp.full(shape, BIG_IDX, I32))
    cut = jnp.where(n_ge > kf, cut, BIG_IDX)
    return thr, cut


def _causal_blocks(step_fn, qi):
    def far_body(j, c):
        step_fn(j, "far")
        return c
    lax.fori_loop(0, qi - 1, far_body, 0)

    @pl.when(qi >= 1)
    def _():
        step_fn(jnp.maximum(qi - 1, 0), "prev")

    step_fn(qi, "diag")


def _causal_t(tk, tq):
    key = lax.broadcasted_iota(I32, (tk, tq), 0)
    qry = lax.broadcasted_iota(I32, (tk, tq), 1)
    return key <= qry


def _mask_queries(qt_ref, qm_ref):
    n_feat, tq = qt_ref.shape[1:]
    rowi = lax.broadcasted_iota(I32, (n_feat, 1), 0)
    qt = qt_ref[0] * 0.125
    for v in range(qm_ref.shape[1] // tq):
        keep = (rowi >= v * DH) & (rowi < (v + 1) * DH)
        qm_ref[:, v * tq:(v + 1) * tq] = jnp.where(keep, qt, 0.0).astype(BF16)


def _flash_update_t(st, v_t, m_ref, l_ref, acc_ref, idx):
    m_old = m_ref[idx]
    m_new = jnp.maximum(m_old, jnp.max(st, axis=0, keepdims=True))
    alpha = jnp.exp(m_old - m_new)
    p = jnp.exp(st - m_new)
    l_ref[idx] = alpha * l_ref[idx] + jnp.sum(p, axis=0, keepdims=True)
    acc_ref[idx] = alpha * acc_ref[idx] + _dot(v_t, p.astype(BF16))
    m_ref[idx] = m_new


def _store_head_pairs(o_ref, m_ref, l_ref, acc_ref, n_h):
    del m_ref
    for p in range(n_h // 2):
        o2 = jnp.concatenate([acc_ref[2 * p] / l_ref[2 * p], acc_ref[2 * p + 1] / l_ref[2 * p + 1]], axis=0)
        o_ref[0, :, p * LANES:(p + 1) * LANES] = o2.T.astype(o_ref.dtype)


def _fox_prompt_kernel(qt_ref, kk_ref, vt_ref, c_ref, ct_ref, o_ref, qm_ref, m_ref, l_ref, acc_ref,
                       *, n_h, col0):
    tq = qt_ref.shape[2]
    tk = tq
    qi = pl.program_id(1)
    _init_flash(m_ref, l_ref, acc_ref)
    _mask_queries(qt_ref, qm_ref)
    ct = ct_ref[0]
    causal = _causal_t(tk, tq)

    def step(j, kind):
        rows = pl.ds(pl.multiple_of(j * tk, tk), tk)
        kblk = kk_ref[0, rows, :]
        cblk = c_ref[0, rows, :]
        sts = _dot(kblk, qm_ref[...])
        for h in range(n_h):
            st = sts[:, h * tq:(h + 1) * tq] + ct[h:h + 1, :] - cblk[:, col0 + h:col0 + h + 1]
            if kind == "diag":
                st = jnp.where(causal, st, NEG)
            _flash_update_t(st, vt_ref[0, j, h * DH:(h + 1) * DH, :], m_ref, l_ref, acc_ref, h)

    _causal_blocks(step, qi)
    _store_head_pairs(o_ref, m_ref, l_ref, acc_ref, n_h)


def _fox_prompt(q_t, q_blk, kk, k_blk, vt_b, c_all, c_t, n_h, col0):
    bsz, t, _ = kk.shape
    tq = min(t, ATT_BLK)
    nb = t // tq
    hw = n_h * DH
    return pl.pallas_call(
        functools.partial(_fox_prompt_kernel, n_h=n_h, col0=col0),
        grid=(bsz, nb),
        in_specs=[pl.BlockSpec((1, hw, tq), lambda b, i: (b, q_blk, i)),
                  pl.BlockSpec((1, t, hw), lambda b, i: (b, 0, k_blk)),
                  pl.BlockSpec((1, nb, hw, tq), lambda b, i: (b, 0, 1, 0)),
                  pl.BlockSpec((1, t, LANES), lambda b, i: (b, 0, 0)),
                  pl.BlockSpec((1, n_h, tq), lambda b, i: (b, 0, i))],
        out_specs=pl.BlockSpec((1, tq, hw), lambda b, i: (b, i, 0)),
        out_shape=jax.ShapeDtypeStruct((bsz, t, hw), BF16),
        scratch_shapes=[pltpu.VMEM((hw, n_h * tq), BF16),
                        pltpu.VMEM((n_h, 1, tq), F32), pltpu.VMEM((n_h, 1, tq), F32),
                        pltpu.VMEM((n_h, DH, tq), F32)],
        compiler_params=_cparams(("parallel", "parallel")),
        name="fox_prompt",
    )(q_t, kk, vt_b, c_all, c_t)


def _diff_finish(a0, l0, a1, l1, lam, gsub, scale):
    o = a0 / l0 - lam * (a1 / l1)
    return _rms(o, gsub) * scale


def _diff_prompt_kernel(far_ref, lam_ref, qt_ref, kk_ref, vt_ref, bias_ref, gs_ref, o_ref,
                        qm_ref, m_ref, l_ref, acc_ref, *, n_h, out_scale):
    tq = qt_ref.shape[2]
    tk = tq
    qi = pl.program_id(1)
    _init_flash(m_ref, l_ref, acc_ref)
    _mask_queries(qt_ref, qm_ref)
    causal = _causal_t(tk, tq)
    dv = 2 * DH

    def step(j, kind):
        kblk = kk_ref[0, pl.ds(pl.multiple_of(j * tk, tk), tk), :]
        sts = _dot(kblk, qm_ref[...])
        for h in range(n_h):
            v_t = vt_ref[0, j, h * dv:(h + 1) * dv, :]
            if kind == "far":
                bias = far_ref[h]
            else:
                bias = bias_ref[h, 0 if kind == "prev" else 1]
            for jj in range(2):
                vh = 2 * h + jj
                st = sts[:, vh * tq:(vh + 1) * tq] + bias
                if kind == "diag":
                    st = jnp.where(causal, st, NEG)
                _flash_update_t(st, v_t, m_ref, l_ref, acc_ref, vh)

    _causal_blocks(step, qi)
    lam = lam_ref[0]
    for h in range(n_h):
        o_t = acc_ref[2 * h] / l_ref[2 * h] - lam * (acc_ref[2 * h + 1] / l_ref[2 * h + 1])
        o = _rms(o_t.T, gs_ref[...]) * out_scale
        o_ref[0, :, h * dv:(h + 1) * dv] = o.astype(o_ref.dtype)


def _diff_prompt(q_t, q_blk, kk, k_blk, vt_b, bias_t, far, lam, gsub, n_h, out_scale):
    bsz, t, _ = kk.shape
    tq = min(t, ATT_BLK)
    nb = t // tq
    hw = n_h * 2 * DH
    smem = pl.BlockSpec(memory_space=pltpu.SMEM)
    return pl.pallas_call(
        functools.partial(_diff_prompt_kernel, n_h=n_h, out_scale=out_scale),
        grid=(bsz, nb),
        in_specs=[smem, smem,
                  pl.BlockSpec((1, hw, tq), lambda b, i: (b, q_blk, i)),
                  pl.BlockSpec((1, t, hw), lambda b, i: (b, 0, k_blk)),
                  pl.BlockSpec((1, nb, hw, tq), lambda b, i: (b, 0, 0, 0)),
                  pl.BlockSpec((n_h, 2, tq, tq), lambda b, i: (0, 0, 0, 0)),
                  pl.BlockSpec((1, 2 * DH), lambda b, i: (0, 0))],
        out_specs=pl.BlockSpec((1, tq, hw), lambda b, i: (b, i, 0)),
        out_shape=jax.ShapeDtypeStruct((bsz, t, hw), BF16),
        scratch_shapes=[pltpu.VMEM((hw, 2 * n_h * tq), BF16),
                        pltpu.VMEM((2 * n_h, 1, tq), F32), pltpu.VMEM((2 * n_h, 1, tq), F32),
                        pltpu.VMEM((2 * n_h, 2 * DH, tq), F32)],
        compiler_params=_cparams(("parallel", "parallel")),
        name="diff_prompt",
    )(far, lam, q_t, kk, vt_b, bias_t, gsub.reshape(1, 2 * DH))


def _dsa_prompt_kernel(far_ref, qt_ref, qit_ref, wt_ref, ki_ref, kk_ref, vt_ref, bias_ref, o_ref,
                       key_ref, qm_ref, m_ref, l_ref, acc_ref, *, n_h, topk, wi_row0):
    tq = qt_ref.shape[2]
    tk = tq
    nb = key_ref.shape[0]
    qi = pl.program_id(1)
    _init_flash(m_ref, l_ref, acc_ref)
    _mask_queries(qt_ref, qm_ref)
    causal = _causal_t(tk, tq)

    w_t = wt_ref[0, wi_row0:wi_row0 + H_IDX, :] * (1.0 / (math.sqrt(H_IDX) * math.sqrt(D_IDX)))

    def score_block(j, diag):
        kib = ki_ref[0, pl.ds(pl.multiple_of(j * tk, tk), tk), :][:, 0:D_IDX].astype(BF16)
        sc = jnp.zeros((tk, tq), F32)
        for h in range(H_IDX):
            d = _dot(kib, qit_ref[0, h * D_IDX:(h + 1) * D_IDX, :])
            sc = sc + jnp.maximum(d, 0.0) * w_t[h:h + 1, :]
        if diag:
            sc = jnp.where(causal, sc, -jnp.inf)
        key_ref[j] = _sortable(sc)

    def score_body(j, c):
        score_block(j, False)
        return c
    lax.fori_loop(0, qi, score_body, 0)
    score_block(qi, True)

    krow = lax.broadcasted_iota(I32, (tk, tq), 0)

    def count_fn(pred):
        def body(j, part):
            hit = jnp.where(pred(key_ref[j], krow + j * tk), 1.0, 0.0)
            return part + jnp.sum(hit.reshape(tk // SUBLANES, SUBLANES, tq), axis=0)
        part = lax.fori_loop(0, qi + 1, body, jnp.zeros((SUBLANES, tq), F32))
        return jnp.sum(part, axis=0, keepdims=True)

    n_bits = max(1, int(math.ceil(math.log2(nb * tk))))
    thr, cut = _topk_select(count_fn, (1, tq), topk, n_bits)

    def step(j, kind):
        kblk = kk_ref[0, pl.ds(pl.multiple_of(j * tk, tk), tk), :]
        key = key_ref[j]
        sel = (key > thr) | ((key == thr) & (krow + j * tk <= cut))
        if kind == "diag":
            sel = sel & causal
        sts = _dot(kblk, qm_ref[...])
        for h in range(n_h):
            if kind == "far":
                bias = far_ref[h]
            else:
                bias = bias_ref[h, 0 if kind == "prev" else 1]
            st = jnp.where(sel, sts[:, h * tq:(h + 1) * tq] + bias, NEG)
            _flash_update_t(st, vt_ref[0, j, h * DH:(h + 1) * DH, :], m_ref, l_ref, acc_ref, h)

    _causal_blocks(step, qi)
    _store_head_pairs(o_ref, m_ref, l_ref, acc_ref, n_h)


def _dsa_prompt(q_t, q_blk, small_t, small, kk, k_blk, vt_b, bias_t, far, n_h, topk, wi_row0):
    bsz, t, _ = kk.shape
    tq = min(t, ATT_BLK)
    nb = t // tq
    hw = n_h * DH
    wqi = H_IDX * D_IDX
    smem = pl.BlockSpec(memory_space=pltpu.SMEM)
    return pl.pallas_call(
        functools.partial(_dsa_prompt_kernel, n_h=n_h, topk=topk, wi_row0=wi_row0),
        grid=(bsz, nb),
        in_specs=[smem,
                  pl.BlockSpec((1, hw, tq), lambda b, i: (b, q_blk, i)),
                  pl.BlockSpec((1, wqi, tq), lambda b, i: (b, 0, i)),
                  pl.BlockSpec((1, LANES, tq), lambda b, i: (b, 0, i)),
                  pl.BlockSpec((1, t, LANES), lambda b, i: (b, 0, 0)),
                  pl.BlockSpec((1, t, hw), lambda b, i: (b, 0, k_blk)),
                  pl.BlockSpec((1, nb, hw, tq), lambda b, i: (b, 0, 1, 0)),
                  pl.BlockSpec((n_h, 2, tq, tq), lambda b, i: (0, 0, 0, 0))],
        out_specs=pl.BlockSpec((1, tq, hw), lambda b, i: (b, i, 0)),
        out_shape=jax.ShapeDtypeStruct((bsz, t, hw), BF16),
        scratch_shapes=[pltpu.VMEM((nb, tq, tq), I32),
                        pltpu.VMEM((hw, n_h * tq), BF16),
                        pltpu.VMEM((n_h, 1, tq), F32), pltpu.VMEM((n_h, 1, tq), F32),
                        pltpu.VMEM((n_h, DH, tq), F32)],
        compiler_params=_cparams(("parallel", "parallel")),
        name="dsa_prompt",
    )(far, q_t, q_t, small_t, small, kk, vt_b, bias_t)


def _dsa_index_sample_kernel(pt_ref, qi_ref, w_ref, *refs, topk, n_q):
    del pt_ref
    pp = PAGES_PER_STEP
    page_refs = refs[:pp]
    kinew_ref, o_ref = refs[pp:]
    s = pl.program_id(1)
    ns = pl.num_programs(1) - 1
    nblk = o_ref.shape[1]
    wblk = o_ref.shape[3]
    qrows = qi_ref[0]
    w = w_ref[0]

    def scores(ki_t):
        d = _dot(qrows, ki_t.astype(BF16))
        r = jnp.maximum(d, 0.0) * w
        sc = r[0:n_q, :]
        for h in range(1, H_IDX):
            sc = sc + r[h * n_q:(h + 1) * n_q, :]
        return sc

    @pl.when(s < ns)
    def _():
        o_ref[0, s] = jnp.concatenate([scores(page_refs[p][0, 0]) for p in range(pp)], axis=1)

    @pl.when(s == ns)
    def _():
        row = lax.broadcasted_iota(I32, (n_q, PAGE), 0)
        colp = lax.broadcasted_iota(I32, (n_q, PAGE), 1)
        sc = jnp.where(colp <= row, scores(kinew_ref[0]), -jnp.inf)
        pad = jnp.full((n_q, wblk - PAGE), -jnp.inf, F32)
        o_ref[0, ns] = jnp.concatenate([sc, pad], axis=1)

        col = lax.broadcasted_iota(I32, (n_q, wblk), 1)

        def count_fn(pred):
            def body(j, part):
                hit = pred(_sortable(o_ref[0, j]), col + j * wblk)
                return part + _lane_fold(jnp.where(hit, 1.0, 0.0))
            part = lax.fori_loop(0, nblk, body, jnp.zeros((n_q, LANES), F32))
            return jnp.sum(part, axis=1, keepdims=True)

        n_bits = max(1, int(math.ceil(math.log2(nblk * wblk))))
        thr, cut = _topk_select(count_fn, (n_q, 1), topk, n_bits)

        def mask_body(j, c):
            key = _sortable(o_ref[0, j])
            sel = (key > thr) | ((key == thr) & (col + j * wblk <= cut))
            valid = o_ref[0, j] > -jnp.inf
            o_ref[0, j] = jnp.where(sel & valid, 0.0, NEG)
            return c
        lax.fori_loop(0, nblk, mask_body, 0)


def _dsa_index_sample(qi_rows, w_rows, ki_pool, layer, page_table, ki_new, topk, n_q):
    bsz, n_pages = page_table.shape
    pp = PAGES_PER_STEP
    ns = n_pages // pp
    wblk = pp * PAGE

    def page_map(p):
        return lambda b, s, pt: (layer, pt[b, jnp.minimum(s, ns - 1) * pp + p], 0, 0)

    rows = qi_rows.shape[1]
    return pl.pallas_call(
        functools.partial(_dsa_index_sample_kernel, topk=topk, n_q=n_q),
        grid_spec=pltpu.PrefetchScalarGridSpec(
            num_scalar_prefetch=1,
            grid=(bsz, ns + 1),
            in_specs=[pl.BlockSpec((1, rows, D_IDX), lambda b, s, pt: (b, 0, 0)),
                      pl.BlockSpec((1, rows, 1), lambda b, s, pt: (b, 0, 0))]
                     + [pl.BlockSpec((1, 1, D_IDX, PAGE), page_map(p)) for p in range(pp)]
                     + [pl.BlockSpec((1, D_IDX, PAGE), lambda b, s, pt: (b, 0, 0))],
            out_specs=pl.BlockSpec((1, ns + 1, n_q, wblk), lambda b, s, pt: (b, 0, 0, 0))),
        out_shape=jax.ShapeDtypeStruct((bsz, ns + 1, n_q, wblk), F32),
        compiler_params=_cparams(("parallel", "arbitrary")),
        name="dsa_index_sample",
    )(page_table, qi_rows, w_rows, *([ki_pool] * pp), ki_new)


def _paged_attn_kernel(pt_ref, *refs, mode, n_q, n_h, dv, out_scale):
    del pt_ref
    pp = PAGES_PER_STEP
    it = iter(refs)
    lam_ref = next(it) if mode == "diff" else None
    q_ref = next(it)
    page_refs = [next(it) for _ in range(pp)]
    kvnew_ref = next(it)
    add_ref = next(it) if mode in ("dsa", "fox") else None
    addnew_ref = next(it) if mode in ("dsa", "fox") else None
    rowc_ref = next(it)
    bprev_ref = next(it)
    bnew_ref = next(it)
    gs_ref = next(it) if mode == "diff" else None
    o_ref, m_ref, l_ref, acc_ref = next(it), next(it), next(it), next(it)

    s = pl.program_id(1)
    ns = pl.num_programs(1) - 1
    rows = q_ref.shape[1]
    n_vh = rows // n_q

    @pl.when(s == 0)
    def _():
        _init_flash(m_ref, l_ref, acc_ref)

    q = q_ref[0]
    rowc = rowc_ref[0]

    if mode == "diff":
        rph = rows // n_h
        stride = 2 * n_h

        def logits(get):
            return jnp.concatenate(
                [_dot_nt(q[h * rph:(h + 1) * rph], get(pl.ds(h, PAGE, stride=stride)).astype(BF16))
                 for h in range(n_h)], axis=0)

        def weighted_values(p, get):
            return jnp.concatenate(
                [_dot(p[h * rph:(h + 1) * rph], get(pl.ds(n_h + h, PAGE, stride=stride)).astype(BF16))
                 for h in range(n_h)], axis=0)
    else:
        kw = q.shape[1]

        def logits(get):
            return _dot(q, get(slice(0, kw)).astype(BF16))

        def weighted_values(p, get):
            return _dot_nt(p, get(slice(kw, 2 * kw)).astype(BF16))

    def row_add(blk):
        if mode == "dsa":
            return jnp.concatenate([blk] * n_vh, axis=0)
        if mode == "fox":
            return jnp.concatenate(
                [jnp.broadcast_to(blk[h:h + 1, :], (n_q, blk.shape[1])) for h in range(n_vh)], axis=0)
        return None

    def update(x, values_fn):
        m_old = m_ref[0]
        m_new = jnp.maximum(m_old, jnp.max(x, axis=-1, keepdims=True))
        alpha = jnp.exp(m_old - m_new)
        p32 = jnp.exp(x - m_new)
        l_ref[0] = alpha * l_ref[0] + jnp.sum(p32, axis=-1, keepdims=True)
        acc_ref[0] = alpha * acc_ref[0] + values_fn(p32.astype(BF16))
        m_ref[0] = m_new

    @pl.when(s < ns)
    def _():
        gets = [(lambda idx, r=page_refs[p]: r[0, 0, idx, :]) for p in range(pp)]
        x = jnp.concatenate([logits(g) for g in gets], axis=1)
        x = x + rowc
        if add_ref is not None:
            x = x + row_add(add_ref[0, 0])
        last = jnp.where(s == ns - 1, bprev_ref[...], 0.0)
        x = jnp.concatenate([x[:, 0:(pp - 1) * PAGE], x[:, (pp - 1) * PAGE:] + last], axis=1)

        def values_fn(p):
            pv = weighted_values(p[:, 0:PAGE], gets[0])
            for i in range(1, pp):
                pv = pv + weighted_values(p[:, i * PAGE:(i + 1) * PAGE], gets[i])
            return pv
        update(x, values_fn)

    @pl.when(s == ns)
    def _():
        get = lambda idx: kvnew_ref[0, idx, :]
        x = logits(get) + rowc + bnew_ref[...]
        if add_ref is not None:
            x = x + row_add(addnew_ref[0])
        update(x, lambda p: weighted_values(p, get))
        acc = acc_ref[0]
        l = l_ref[0]
        if mode == "diff":
            lam = lam_ref[0]
            outs = []
            for h in range(n_h):
                r0 = h * rph
                r1 = r0 + n_q
                outs.append(_diff_finish(acc[r0:r0 + n_q], l[r0:r0 + n_q], acc[r1:r1 + n_q], l[r1:r1 + n_q],
                                         lam, gs_ref[...], out_scale))
        else:
            outs = [acc[h * n_q:(h + 1) * n_q, h * dv:(h + 1) * dv] / l[h * n_q:(h + 1) * n_q]
                    for h in range(n_h)]
        o_ref[0] = jnp.concatenate(outs, axis=1).astype(o_ref.dtype)


def _paged_attn(mode, q, pool, layer, page_table, kv_new, add, add_new, rowc, bprev, bnew,
                n_q, n_h, dv, lam=None, gsub=None, out_scale=1.0):
    bsz, n_pages = page_table.shape
    pp = PAGES_PER_STEP
    ns = n_pages // pp
    rows, qw = q.shape[1:]
    prow, pcol = pool.shape[2:]
    acc_w = dv if mode == "diff" else qw

    def page_map(p):
        return lambda b, s, pt: (layer, pt[b, jnp.minimum(s, ns - 1) * pp + p], 0, 0)

    in_specs = []
    args = []
    if mode == "diff":
        in_specs.append(pl.BlockSpec(memory_space=pltpu.SMEM))
        args.append(lam)
    in_specs.append(pl.BlockSpec((1, rows, qw), lambda b, s, pt: (b, 0, 0)))
    args.append(q)
    in_specs += [pl.BlockSpec((1, 1, prow, pcol), page_map(p)) for p in range(pp)]
    args += [pool] * pp
    in_specs.append(pl.BlockSpec((1, prow, pcol), lambda b, s, pt: (b, 0, 0)))
    args.append(kv_new)
    if add is not None:
        in_specs.append(pl.BlockSpec((1, 1, add.shape[2], add.shape[3]),
                                     lambda b, s, pt: (b, jnp.minimum(s, ns - 1), 0, 0)))
        args.append(add)
        in_specs.append(pl.BlockSpec((1, add_new.shape[1], PAGE), lambda b, s, pt: (b, 0, 0)))
        args.append(add_new)
    in_specs.append(pl.BlockSpec((1, rows, 1), lambda b, s, pt: (b, 0, 0)))
    args.append(rowc)
    in_specs.append(pl.BlockSpec((rows, PAGE), lambda b, s, pt: (0, 0)))
    args.append(bprev)
    in_specs.append(pl.BlockSpec((rows, PAGE), lambda b, s, pt: (0, 0)))
    args.append(bnew)
    if mode == "diff":
        in_specs.append(pl.BlockSpec((1, dv), lambda b, s, pt: (0, 0)))
        args.append(gsub.reshape(1, dv))
    return pl.pallas_call(
        functools.partial(_paged_attn_kernel, mode=mode, n_q=n_q, n_h=n_h, dv=dv, out_scale=out_scale),
        grid_spec=pltpu.PrefetchScalarGridSpec(
            num_scalar_prefetch=1,
            grid=(bsz, ns + 1),
            in_specs=in_specs,
            out_specs=pl.BlockSpec((1, n_q, n_h * dv), lambda b, s, pt: (b, 0, 0)),
            scratch_shapes=[pltpu.VMEM((1, rows, 1), F32), pltpu.VMEM((1, rows, 1), F32),
                            pltpu.VMEM((1, rows, acc_w), F32)]),
        out_shape=jax.ShapeDtypeStruct((bsz, n_q, n_h * dv), BF16),
        compiler_params=_cparams(("parallel", "arbitrary")),
        name="paged_attn_" + mode,
    )(page_table, *args)


def _s5_kernel(u_ref, h0r_ref, h0i_ref, bre_ref, bim_ref, cre_ref, cim_ref, alr_ref, ali_ref,
               acr_ref, aci_ref, d_ref, wglu_ref, o_ref, sr_ref, si_ref,
               hr_ref, hi_ref, cr_ref, ci_ref):
    t = pl.program_id(1)
    ln = u_ref.shape[1]

    @pl.when(t == 0)
    def _():
        cr_ref[...] = h0r_ref[0]
        ci_ref[...] = h0i_ref[0]

    u = u_ref[0]
    ub = u.astype(BF16)
    xr = _dot(ub, bre_ref[...])
    xi = _dot(ub, bim_ref[...])
    row = lax.broadcasted_iota(I32, (ln, 1), 0) % SUBLANES
    for kk, sh in enumerate((1, 2, 4)):
        ar = alr_ref[kk:kk + 1, :]
        ai = ali_ref[kk:kk + 1, :]
        pr = pltpu.roll(xr, sh, 0)
        pi = pltpu.roll(xi, sh, 0)
        keep = row >= sh
        xr, xi = (xr + jnp.where(keep, ar * pr - ai * pi, 0.0),
                  xi + jnp.where(keep, ar * pi + ai * pr, 0.0))
    hr_ref[...] = xr
    hi_ref[...] = xi
    acr = acr_ref[...]
    aci = aci_ref[...]

    def group(r, carry):
        cr, ci = carry
        sl = pl.ds(pl.multiple_of(r * SUBLANES, SUBLANES), SUBLANES)
        br = hr_ref[sl, :] + acr * cr - aci * ci
        bi = hi_ref[sl, :] + acr * ci + aci * cr
        hr_ref[sl, :] = br
        hi_ref[sl, :] = bi
        return br[SUBLANES - 1:SUBLANES, :], bi[SUBLANES - 1:SUBLANES, :]

    cr, ci = lax.fori_loop(0, ln // SUBLANES, group, (cr_ref[...], ci_ref[...]))
    cr_ref[...] = cr
    ci_ref[...] = ci
    y = _dot(hr_ref[...].astype(BF16), cre_ref[...]) - _dot(hi_ref[...].astype(BF16), cim_ref[...])
    y = _gelu(y + d_ref[...] * u)
    z = _dot(y.astype(BF16), wglu_ref[...])
    o_ref[0] = (y * _sigmoid(z)).astype(o_ref.dtype)
    sr_ref[0] = cr
    si_ref[0] = ci


def _s5_mixer(u, h0r, h0i, prm):
    bsz, t, ch = u.shape
    n_state = h0r.shape[-1]
    ln = min(t, 256)
    const = lambda shape: pl.BlockSpec(shape, lambda b, i: (0, 0))
    state = pl.BlockSpec((1, 1, n_state), lambda b, i: (b, 0, 0))
    return pl.pallas_call(
        _s5_kernel,
        grid=(bsz, t // ln),
        in_specs=[pl.BlockSpec((1, ln, ch), lambda b, i: (b, i, 0)), state, state,
                  const((ch, n_state)), const((ch, n_state)),
                  const((n_state, ch)), const((n_state, ch)),
                  const((SUBLANES, n_state)), const((SUBLANES, n_state)),
                  const((SUBLANES, n_state)), const((SUBLANES, n_state)),
                  const((1, ch)), const((ch, ch))],
        out_specs=[pl.BlockSpec((1, ln, ch), lambda b, i: (b, i, 0)), state, state],
        out_shape=[jax.ShapeDtypeStruct((bsz, t, ch), BF16),
                   jax.ShapeDtypeStruct((bsz, 1, n_state), F32),
                   jax.ShapeDtypeStruct((bsz, 1, n_state), F32)],
        scratch_shapes=[pltpu.VMEM((ln, n_state), F32), pltpu.VMEM((ln, n_state), F32),
                        pltpu.VMEM((1, n_state), F32), pltpu.VMEM((1, n_state), F32)],
        compiler_params=_cparams(("parallel", "arbitrary")),
        name="s5_mixer",
    )(u, h0r, h0i, prm["bre"], prm["bim"], prm["cre"], prm["cim"],
      prm["alr"], prm["ali"], prm["acr"], prm["aci"], prm["d"], prm["wglu"])


def _s5_params(a_re, a_im, log_dt, b_re, b_im, c_re, c_im, d, w_glu):
    g, p = a_re.shape
    c = b_re.shape[-1]
    dt = jnp.exp(log_dt)[:, None]
    mag = jnp.exp(a_re * dt)
    ar = mag * jnp.cos(a_im * dt)
    ai = mag * jnp.sin(a_im * dt)
    den = a_re * a_re + a_im * a_im
    fr = ((ar - 1.0) * a_re + ai * a_im) / den
    fi = (ai * a_re - (ar - 1.0) * a_im) / den
    bbr = fr[..., None] * b_re - fi[..., None] * b_im
    bbi = fr[..., None] * b_im + fi[..., None] * b_re
    eye = jnp.eye(g, dtype=F32)

    def in_proj(x):
        return jnp.einsum("gpc,gh->gchp", x, eye).reshape(g * c, g * p).astype(BF16)

    def out_proj(x):
        return jnp.einsum("gcp,gh->gphc", x, eye).reshape(g * p, g * c).astype(BF16)

    def powers(n_list):
        rs, is_ = [], []
        for n in n_list:
            m = jnp.exp(a_re * dt * n)
            rs.append((m * jnp.cos(a_im * dt * n)).reshape(1, g * p))
            is_.append((m * jnp.sin(a_im * dt * n)).reshape(1, g * p))
        pad = SUBLANES - len(n_list)
        if pad:
            rs += [jnp.zeros((pad, g * p), F32)]
            is_ += [jnp.zeros((pad, g * p), F32)]
        return jnp.concatenate(rs, axis=0), jnp.concatenate(is_, axis=0)

    alr, ali = powers([1, 2, 4])
    acr, aci = powers(list(range(1, SUBLANES + 1)))
    return {"bre": in_proj(bbr), "bim": in_proj(bbi), "cre": out_proj(c_re), "cim": out_proj(c_im),
            "alr": alr, "ali": ali, "acr": acr, "aci": aci,
            "d": d.reshape(1, g * c), "wglu": w_glu.astype(BF16)}


def _xattn_kernel(hn_ref, wq_ref, mkv_ref, o_ref, *, n_h):
    qx = _dot(hn_ref[0], wq_ref[...])
    mkv = mkv_ref[0]
    hw = n_h * DH_X
    scale = DH_X ** -0.5
    for h in range(n_h):
        qh = qx[:, h * DH_X:(h + 1) * DH_X].astype(BF16)
        kh = mkv[:, h * DH_X:(h + 1) * DH_X].astype(BF16)
        vh = mkv[:, hw + h * DH_X:hw + (h + 1) * DH_X].astype(BF16)
        s = _dot_nt(qh, kh) * scale
        m = jnp.max(s, axis=-1, keepdims=True)
        p = jnp.exp(s - m)
        l = jnp.sum(p, axis=-1, keepdims=True)
        o = _dot(p.astype(BF16), vh) / l
        o_ref[0, :, h * DH_X:(h + 1) * DH_X] = o.astype(o_ref.dtype)


def _xattn(hn, w_qx, mem_kv, n_h):
    bsz, t, d = hn.shape
    n_mem = mem_kv.shape[1]
    hw = n_h * DH_X
    tq = min(t, 256)
    return pl.pallas_call(
        functools.partial(_xattn_kernel, n_h=n_h),
        grid=(bsz, t // tq),
        in_specs=[pl.BlockSpec((1, tq, d), lambda b, i: (b, i, 0)),
                  pl.BlockSpec((d, hw), lambda b, i: (0, 0)),
                  pl.BlockSpec((1, n_mem, 2 * hw), lambda b, i: (b, 0, 0))],
        out_specs=pl.BlockSpec((1, tq, hw), lambda b, i: (b, i, 0)),
        out_shape=jax.ShapeDtypeStruct((bsz, t, hw), BF16),
        compiler_params=_cparams(("parallel", "parallel")),
        name="xattn",
    )(hn, w_qx, mem_kv)


def _block_diag_q(q, n_vh, scale):
    bsz, n_q, _ = q.shape
    qh = (q.astype(F32) * scale).reshape(bsz, n_q, n_vh, DH)
    eye = jnp.eye(n_vh, dtype=F32)
    out = jnp.einsum("bqhd,hg->bhqgd", qh, eye)
    return out.reshape(bsz, n_vh * n_q, n_vh * DH).astype(BF16)


def _map_diag_q(q, n_h, scale):
    bsz, n_q, _ = q.shape
    qh = (q.astype(F32) * scale).reshape(bsz, n_q, n_h, 2, DH)
    eye = jnp.eye(2, dtype=F32)
    out = jnp.einsum("bqhjd,jg->bhjqgd", qh, eye)
    return out.reshape(bsz, n_h * 2 * n_q, 2 * DH).astype(BF16)


def kernel(x_prompt, x_sample, mem_prompt, cache_a_kv, cache_a_idx_k, cache_c_kv, cache_d_kv,
           cache_d_logf, cache_mem_kv, state_s5_re, state_s5_im, state_ffn_conv, page_table,
           rel_bias, g_mix_pre, w_in, s5_a_re, s5_a_im, s5_log_dt, s5_b_re, s5_b_im, s5_c_re,
           s5_c_im, s5_d, s5_w_glu, lam_q1, lam_k1, lam_q2, lam_k2, diff_sub_g, fox_b_f, w_out,
           g_mix_post, g_x_pre, w_qx, w_kvx, w_ox, g_x_post, g_ffn_pre, w_gate, w_up, conv_w,
           conv_b, w_down, g_ffn_post):
    bp, t, d = x_prompt.shape
    bs, ts, _ = x_sample.shape
    depth = w_in.shape[0]
    n_pool = cache_a_kv.shape[1]
    n_pages = page_table.shape[1]
    past = n_pages * PAGE
    h_a = cache_a_kv.shape[4]
    h_c = cache_c_kv.shape[4]
    h_d = cache_d_kv.shape[4]
    h_x = cache_mem_kv.shape[4]
    n_mem = cache_mem_kv.shape[2]
    s5_g, s5_p = s5_a_re.shape[1:]
    s5_ch = s5_g * S5_GROUP
    n_state = s5_g * s5_p
    d_ff = w_gate.shape[2]
    wa, wc, wd = h_a * DH, h_c * 2 * DH, h_d * DH
    wqi = H_IDX * D_IDX
    assert wa == wc == wd == s5_ch and ts == SUBLANES and past % (PAGES_PER_STEP * PAGE) == 0
    pp = PAGES_PER_STEP
    ns = n_pages // pp
    topk_p = min(TOPK_MAX, t // 4)
    topk_s = min(TOPK_MAX, (past + ts) // 4)

    sizes = (wa, wa, wa, wqi, D_IDX, H_IDX, s5_ch, wc, wc, wc, wd, wd, wd, h_d)
    offs = np.concatenate([[0], np.cumsum(sizes)]).tolist()
    (o_aq, o_ak, o_av, o_aqi, o_aki, o_awi, o_bu, o_cq, o_ck, o_cv, o_dq, o_dk, o_dv, o_df) = offs[:-1]
    col_wi, col_df = D_IDX, D_IDX + H_IDX

    tq = min(t, ATT_BLK)
    bias_p = jnp.stack([_bias_tile(rel_bias, tq, tq, tq, keys_on_rows=True),
                        _bias_tile(rel_bias, 0, tq, tq, keys_on_rows=True)], axis=1)
    bias_s_prev = _bias_tile(rel_bias, PAGE, ts, PAGE)
    bias_s_new = _bias_tile(rel_bias, 0, ts, PAGE)
    far = rel_bias[NUM_BUCKETS - 1]
    causal_new = jnp.where(jnp.arange(PAGE)[None, :] <= jnp.arange(ts)[:, None], 0.0, NEG).astype(F32)

    def sample_tiles(h0, n_h, rep):
        hs = [h0 + i // rep for i in range(n_h * rep)]
        farc = jnp.stack([jnp.full((ts, 1), 1.0, F32) * far[h] for h in hs]).reshape(-1, 1)
        prev = jnp.stack([bias_s_prev[h] for h in hs]).reshape(-1, PAGE) - farc
        new = jnp.stack([bias_s_new[h] + causal_new for h in hs]).reshape(-1, PAGE) - farc
        return jnp.broadcast_to(farc[None], (bs,) + farc.shape), prev, new

    rowc_a, bprev_a, bnew_a = sample_tiles(0, h_a, 1)
    rowc_c, bprev_c, bnew_c = sample_tiles(h_a, h_c, 2)
    bprev_d = jnp.zeros((h_d * ts, PAGE), F32)
    bnew_d = jnp.tile(causal_new, (h_d, 1))

    key_minor = lambda c: jnp.moveaxis(c, 2, -1)
    pool_a = key_minor(cache_a_kv).reshape(depth, n_pool, 2 * wa, PAGE)
    pool_d = key_minor(cache_d_kv).reshape(depth, n_pool, 2 * wd, PAGE)
    pool_c = cache_c_kv.reshape(depth, n_pool, PAGE * 2 * h_c, 2 * DH)
    pool_ki = key_minor(cache_a_idx_k)
    pool_lf = key_minor(cache_d_logf)

    xp = x_prompt.reshape(bp * t, d)
    xs = x_sample.reshape(bs * ts, d)
    mem_bf = mem_prompt.reshape(bp * n_mem, d).astype(BF16)
    hn_p = _norm_cast(xp, g_mix_pre[0])
    hn_s = _norm_cast(xs, g_mix_pre[0])

    outs = {k: [] for k in ("a_kv_p", "a_kv_s", "a_ki_p", "a_ki_s", "c_kv_p", "c_kv_s", "d_kv_p",
                            "d_kv_s", "lf_p", "lf_s", "mem_kv", "s5r_p", "s5r_s", "s5i_p", "s5i_s",
                            "cs_p", "cs_s")}

    for l in range(depth):
        wl = w_in[l]
        w_akv = wl[:, o_ak:o_aqi].astype(BF16)
        w_ckv = wl[:, o_ck:o_dq].astype(BF16)
        w_dkv = wl[:, o_dk:o_df].astype(BF16)
        w_q = jnp.concatenate([wl[:, o_aqi:o_aki], wl[:, o_aq:o_ak], wl[:, o_cq:o_ck],
                               wl[:, o_dq:o_dk]], axis=1).astype(BF16)
        w_bu = wl[:, o_bu:o_cq].astype(BF16)
        w_small = jnp.concatenate([wl[:, o_aki:o_bu], wl[:, o_df:],
                                   jnp.zeros((d, LANES - D_IDX - H_IDX - h_d), F32)], axis=1).astype(BF16)
        qblk_a, qblk_c, qblk_d = wqi // wa, wqi // wa + 1, wqi // wa + 2
        wt = jnp.swapaxes(wl, 0, 1)
        wt_akv = wt[o_ak:o_aqi].astype(BF16)
        wt_dkv = wt[o_dk:o_df].astype(BF16)
        wt_cv = wt[o_cv:o_dq].astype(BF16)
        wt_q = jnp.concatenate([wt[o_aqi:o_aki], wt[o_aq:o_ak], wt[o_cq:o_ck], wt[o_dq:o_dk]],
                               axis=0).astype(BF16)
        wt_small = jnp.concatenate([wt[o_aki:o_bu], wt[o_df:],
                                    jnp.zeros((LANES - D_IDX - H_IDX - h_d, d), F32)], axis=0).astype(BF16)
        w_keys = jnp.concatenate([wl[:, o_ak:o_av], wl[:, o_ck:o_cv], wl[:, o_dk:o_dv]], axis=1).astype(BF16)
        lam_init = 0.8 - 0.6 * math.exp(-0.3 * l)
        lam = (jnp.exp(jnp.sum(lam_q1[l] * lam_k1[l])) - jnp.exp(jnp.sum(lam_q2[l] * lam_k2[l]))
               + lam_init).reshape(1)
        bvec = jnp.zeros((1, LANES), F32).at[0, col_df:col_df + h_d].set(fox_b_f[l])
        s5p = _s5_params(s5_a_re[l], s5_a_im[l], s5_log_dt[l], s5_b_re[l], s5_b_im[l],
                         s5_c_re[l], s5_c_im[l], s5_d[l], s5_w_glu[l])
        w_out_b = w_out[l].astype(BF16)
        w_qx_b = w_qx[l].astype(BF16)
        w_ox_b = w_ox[l].astype(BF16)
        w_gate_b = w_gate[l].astype(BF16)
        w_up_b = w_up[l].astype(BF16)
        w_down_b = w_down[l].astype(BF16)
        g_next = g_mix_pre[l + 1] if l + 1 < depth else None

        mkv_p = _matmul(mem_bf, w_kvx[l].astype(BF16), F32).reshape(bp, n_mem, 2 * h_x * DH_X)
        outs["mem_kv"].append(mkv_p)

        def mixer_inputs(hn):
            return (_matmul(hn, w_akv, F32), _matmul(hn, w_ckv, F32), _matmul(hn, w_dkv, F32),
                    _matmul(hn, w_q, BF16), _matmul(hn, w_bu, F32), _matmul(hn, w_small, F32))

        hn3 = hn_p.reshape(bp, t, d)
        a_kv_t, a_vt_b = _matmul_nt(wt_akv, hn3, F32, True)
        d_kv_t, d_vt_b = _matmul_nt(wt_dkv, hn3, F32, True)
        (c_vt_b,) = _matmul_nt(wt_cv, hn3, None, True)
        (q_t,) = _matmul_nt(wt_q, hn3, BF16, False)
        (small_t,) = _matmul_nt(wt_small, hn3, F32, False)
        kk = _matmul(hn_p, w_keys, BF16).reshape(bp, t, -1)
        c_kv3 = _matmul(hn_p, w_ckv, F32).reshape(bp, t, 2 * wc)
        b_u = _matmul(hn_p, w_bu, F32)
        small3 = _matmul(hn_p, w_small, F32).reshape(bp, t, LANES)
        _, c3 = _logsig_cumsum(small3, bvec)
        lf_t, c_t = _logsig_cumsum_t(small_t, fox_b_f[l].reshape(h_d, 1), col_df)
        o_a = _dsa_prompt(q_t, qblk_a, small_t, small3, kk, 0, a_vt_b, bias_p[:h_a], far[:h_a],
                          h_a, topk_p, col_wi)
        o_b, s5r, s5i = _s5_mixer(b_u.reshape(bp, t, s5_ch), jnp.zeros((bp, 1, n_state), F32),
                                  jnp.zeros((bp, 1, n_state), F32), s5p)
        o_c = _diff_prompt(q_t, qblk_c, kk, 1, c_vt_b, bias_p[h_a:], far[h_a:], lam, diff_sub_g[l], h_c,
                           1.0 - lam_init)
        o_d = _fox_prompt(q_t, qblk_d, kk, 2, d_vt_b, c3, c_t, h_d, col_df)
        mix = [o.reshape(bp * t, -1) for o in (o_a, o_b, o_c, o_d)]
        xp, hn = _proj_res(mix, w_out_b, xp, g_mix_post[l], g_x_pre[l])
        ox = _xattn(hn.reshape(bp, t, d), w_qx_b, mkv_p, h_x)
        xp, hn = _proj_res([ox.reshape(bp * t, -1)], w_ox_b, xp, g_x_post[l], g_ffn_pre[l])
        hid, cs = _ffn_hidden(hn, w_gate_b, w_up_b, conv_w[l], conv_b[l],
                              jnp.zeros((bp, CONV_W - 1, d_ff), F32), t)
        xp, hn_p = _proj_res_ktiled(hid, w_down_b, xp, g_ffn_post[l], g_next)
        outs["a_kv_p"].append(a_kv_t)
        outs["a_ki_p"].append(small_t[:, 0:D_IDX, :])
        outs["c_kv_p"].append(c_kv3)
        outs["d_kv_p"].append(d_kv_t)
        outs["lf_p"].append(lf_t)
        outs["s5r_p"].append(s5r)
        outs["s5i_p"].append(s5i)
        outs["cs_p"].append(cs)

        a_kv, c_kv, d_kv, q_all, b_u, small = mixer_inputs(hn_s)
        a_kv3 = a_kv.reshape(bs, ts, 2 * wa)
        c_kv3 = c_kv.reshape(bs, ts, 2 * wc)
        d_kv3 = d_kv.reshape(bs, ts, 2 * wd)
        q3 = q_all.reshape(bs, ts, -1)
        small3 = small.reshape(bs, ts, LANES)
        lf3, c3 = _logsig_cumsum(small3, bvec)
        keys_on_lanes = lambda x: jnp.pad(jnp.swapaxes(x, 1, 2), ((0, 0), (0, 0), (0, PAGE - ts)))
        keys_on_rows = lambda x: jnp.pad(x, ((0, 0), (0, PAGE - ts), (0, 0)))

        qi_rows = jnp.swapaxes(q3[:, :, 0:wqi].reshape(bs, ts, H_IDX, D_IDX), 1, 2)
        qi_rows = qi_rows.reshape(bs, H_IDX * ts, D_IDX)
        w_rows = jnp.swapaxes(small3[:, :, col_wi:col_wi + H_IDX], 1, 2).reshape(bs, H_IDX * ts, 1)
        w_rows = w_rows * (1.0 / (math.sqrt(H_IDX) * math.sqrt(D_IDX)))
        amask = _dsa_index_sample(qi_rows, w_rows, pool_ki, l, page_table,
                                  keys_on_lanes(small3[:, :, 0:D_IDX]), topk_s, ts)
        o_a = _paged_attn("dsa", _block_diag_q(q3[:, :, wqi:wqi + wa], h_a, 0.125), pool_a, l,
                          page_table, keys_on_lanes(a_kv3), amask, amask[:, ns, :, 0:PAGE],
                          rowc_a, bprev_a, bnew_a, ts, h_a, DH)
        o_b, s5r, s5i = _s5_mixer(b_u.reshape(bs, ts, s5_ch),
                                  state_s5_re[l].reshape(bs, 1, n_state),
                                  state_s5_im[l].reshape(bs, 1, n_state), s5p)
        o_c = _paged_attn("diff", _map_diag_q(q3[:, :, wqi + wa:wqi + wa + wc], h_c, 0.125),
                          pool_c, l, page_table,
                          keys_on_rows(c_kv3).reshape(bs, PAGE * 2 * h_c, 2 * DH), None, None,
                          rowc_c, bprev_c, bnew_c,
                          ts, h_c, 2 * DH, lam=lam, gsub=diff_sub_g[l], out_scale=1.0 - lam_init)
        c_new = c3[:, :, col_df:col_df + h_d]
        nck_past = _page_suffix(pool_lf, l, page_table)
        nck_new = -keys_on_lanes(c_new)
        rowc_d = jnp.swapaxes(c_new, 1, 2).reshape(bs, h_d * ts, 1)
        o_d = _paged_attn("fox", _block_diag_q(q3[:, :, wqi + wa + wc:], h_d, 0.125), pool_d, l,
                          page_table, keys_on_lanes(d_kv3), nck_past, nck_new,
                          rowc_d, bprev_d, bnew_d, ts, h_d, DH)
        mix = [o.reshape(bs * ts, -1) for o in (o_a, o_b, o_c, o_d)]
        xs, hn = _proj_res(mix, w_out_b, xs, g_mix_post[l], g_x_pre[l])
        mkv_s = cache_mem_kv[l].reshape(bs, n_mem, 2 * h_x * DH_X)
        ox = _xattn(hn.reshape(bs, ts, d), w_qx_b, mkv_s, h_x)
        xs, hn = _proj_res([ox.reshape(bs * ts, -1)], w_ox_b, xs, g_x_post[l], g_ffn_pre[l])
        hid, cs = _ffn_hidden(hn, w_gate_b, w_up_b, conv_w[l], conv_b[l], state_ffn_conv[l], ts)
        xs, hn_s = _proj_res_ktiled(hid, w_down_b, xs, g_ffn_post[l], g_next)
        outs["a_kv_s"].append(a_kv3)
        outs["a_ki_s"].append(small3[:, :, 0:D_IDX])
        outs["c_kv_s"].append(c_kv3)
        outs["d_kv_s"].append(d_kv3)
        outs["lf_s"].append(lf3[:, :, col_df:col_df + h_d])
        outs["s5r_s"].append(s5r)
        outs["s5i_s"].append(s5i)
        outs["cs_s"].append(cs)

    st = {k: jnp.stack(v) for k, v in outs.items()}
    token_major = lambda x: jnp.moveaxis(x, -1, 2)
    return (xp.reshape(bp, t, d), xs.reshape(bs, ts, d),
            token_major(st["a_kv_p"].reshape(depth, bp, 2, h_a, DH, t)),
            st["a_kv_s"].reshape(depth, bs, ts, 2, h_a, DH),
            token_major(st["a_ki_p"]), st["a_ki_s"],
            st["c_kv_p"].reshape(depth, bp, t, 2, h_c, 2 * DH),
            st["c_kv_s"].reshape(depth, bs, ts, 2, h_c, 2 * DH),
            token_major(st["d_kv_p"].reshape(depth, bp, 2, h_d, DH, t)),
            st["d_kv_s"].reshape(depth, bs, ts, 2, h_d, DH),
            token_major(st["lf_p"]), st["lf_s"],
            st["mem_kv"].reshape(depth, bp, n_mem, 2, h_x, DH_X),
            st["s5r_p"].reshape(depth, bp, s5_g, s5_p), st["s5r_s"].reshape(depth, bs, s5_g, s5_p),
            st["s5i_p"].reshape(depth, bp, s5_g, s5_p), st["s5i_s"].reshape(depth, bs, s5_g, s5_p),
            st["cs_p"], st["cs_s"])
```

```python
import functools
import math

import numpy as np
import jax
import jax.numpy as jnp
from jax import lax
from jax.experimental import pallas as pl
from jax.experimental.pallas import tpu as pltpu

F32 = jnp.float32
BF16 = jnp.bfloat16
I32 = jnp.int32

EPS = 1e-6
NEG = -1e30
M_INIT = -3e38
INT_MIN = -2 ** 31
BIG_IDX = 2 ** 30

DH = 64
H_IDX = 16
D_IDX = 64
TOPK_MAX = 256
S5_GROUP = 16
S5_P = 64
DH_X = 128
NUM_BUCKETS = 32
MAX_DISTANCE = 128
CONV_W = 3
PAGE = 128

LANES = 128
SUBLANES = 8
VMEM_LIMIT = 52 * 1024 * 1024

ATT_BLK = 256
PAGES_PER_STEP = 16
KTILE_BYTES = 2 * 1024 * 1024


def _t5_thresholds():
    exact = NUM_BUCKETS // 2
    n = np.arange(exact, MAX_DISTANCE + 1).astype(np.float64)
    large = exact + np.floor(np.log(n / exact) / math.log(MAX_DISTANCE / exact)
                             * (NUM_BUCKETS - exact)).astype(np.int64)
    bucket = np.minimum(large, NUM_BUCKETS - 1)
    return [int(n[np.argmax(bucket >= b)]) for b in range(exact + 1, NUM_BUCKETS)]


_T5_THR = _t5_thresholds()


def _cparams(sem):
    return pltpu.CompilerParams(dimension_semantics=sem, vmem_limit_bytes=VMEM_LIMIT)


def _dot(a, b):
    return jnp.dot(a, b, preferred_element_type=F32)


def _dot_nt(a, b):
    return lax.dot_general(a, b, (((1,), (1,)), ((), ())), preferred_element_type=F32)


def _rms(x, g):
    y = x * lax.rsqrt(jnp.mean(x * x, axis=-1, keepdims=True) + EPS)
    return y * g


def _gelu(x):
    c = math.sqrt(2.0 / math.pi)
    return 0.5 * x * (1.0 + jnp.tanh(c * (x + 0.044715 * (x * x * x))))


def _sigmoid(x):
    return 1.0 / (1.0 + jnp.exp(-x))


def _log_sigmoid(x):
    return jnp.minimum(x, 0.0) - jnp.log(1.0 + jnp.exp(-jnp.abs(x)))


def _norm_cast_kernel(x_ref, g_ref, o_ref):
    o_ref[...] = _rms(x_ref[...], g_ref[...]).astype(o_ref.dtype)


def _norm_cast(x, g):
    m, d = x.shape
    tm = min(m, 512)
    return pl.pallas_call(
        _norm_cast_kernel,
        grid=(m // tm,),
        in_specs=[pl.BlockSpec((tm, d), lambda i: (i, 0)),
                  pl.BlockSpec((1, d), lambda i: (0, 0))],
        out_specs=pl.BlockSpec((tm, d), lambda i: (i, 0)),
        out_shape=jax.ShapeDtypeStruct((m, d), BF16),
        compiler_params=_cparams(("parallel",)),
        name="norm_cast",
    )(x, g.reshape(1, d))


def _mm_kernel(a_ref, w_ref, o_ref):
    o_ref[...] = _dot(a_ref[...], w_ref[...]).astype(o_ref.dtype)


def _matmul(a, w, out_dtype):
    m, k = a.shape
    n = w.shape[1]
    tm = min(m, 1024)
    tn = min(n, 512)
    return pl.pallas_call(
        _mm_kernel,
        grid=(m // tm, n // tn),
        in_specs=[pl.BlockSpec((tm, k), lambda i, j: (i, 0)),
                  pl.BlockSpec((k, tn), lambda i, j: (0, j))],
        out_specs=pl.BlockSpec((tm, tn), lambda i, j: (i, j)),
        out_shape=jax.ShapeDtypeStruct((m, n), out_dtype),
        compiler_params=_cparams(("parallel", "parallel")),
        name="matmul",
    )(a, w)


def _mm_nt_kernel(w_ref, a_ref, *o_refs, plain, blocked):
    acc = _dot_nt(w_ref[...], a_ref[0])
    k = 0
    if plain:
        o_refs[k][0] = acc.astype(o_refs[k].dtype)
        k += 1
    if blocked:
        ob = o_refs[k]
        tk = ob.shape[3]
        for i in range(ob.shape[1]):
            ob[0, i] = acc[:, i * tk:(i + 1) * tk].astype(ob.dtype)


def _matmul_nt(w_t, a3, plain_dtype, blocked):
    n, k = w_t.shape
    bsz, t, _ = a3.shape
    tm = min(t, 1024)
    tn = min(n, 512)
    tk = min(t, ATT_BLK)
    out_specs, out_shape = [], []
    if plain_dtype is not None:
        out_specs.append(pl.BlockSpec((1, tn, tm), lambda b, i, j: (b, j, i)))
        out_shape.append(jax.ShapeDtypeStruct((bsz, n, t), plain_dtype))
    if blocked:
        out_specs.append(pl.BlockSpec((1, tm // tk, tn, tk), lambda b, i, j: (b, i, j, 0)))
        out_shape.append(jax.ShapeDtypeStruct((bsz, t // tk, n, tk), BF16))
    return pl.pallas_call(
        functools.partial(_mm_nt_kernel, plain=plain_dtype is not None, blocked=blocked),
        grid=(bsz, t // tm, n // tn),
        in_specs=[pl.BlockSpec((tn, k), lambda b, i, j: (j, 0)),
                  pl.BlockSpec((1, tm, k), lambda b, i, j: (b, i, 0))],
        out_specs=out_specs,
        out_shape=out_shape,
        compiler_params=_cparams(("parallel", "parallel", "parallel")),
        name="matmul_nt",
    )(w_t, a3)


def _proj_res_kernel(*refs, n_in, emit_next):
    a_refs = refs[:n_in]
    w_ref, x_ref, gp_ref, gn_ref = refs[n_in:n_in + 4]
    outs = refs[n_in + 4:]
    off = 0
    acc = None
    for a_ref in a_refs:
        kk = a_ref.shape[1]
        part = _dot(a_ref[...], w_ref[off:off + kk, :])
        acc = part if acc is None else acc + part
        off += kk
    xn = x_ref[...] + _rms(acc, gp_ref[...])
    outs[0][...] = xn
    if emit_next:
        outs[1][...] = _rms(xn, gn_ref[...]).astype(BF16)


def _proj_res(a_list, w, x, g_post, g_next):
    m, d = x.shape
    k = w.shape[0]
    tm = min(m, 256)
    emit_next = g_next is not None
    gn = g_next if emit_next else g_post
    in_specs = [pl.BlockSpec((tm, a.shape[1]), lambda i: (i, 0)) for a in a_list]
    in_specs += [pl.BlockSpec((k, d), lambda i: (0, 0)),
                 pl.BlockSpec((tm, d), lambda i: (i, 0)),
                 pl.BlockSpec((1, d), lambda i: (0, 0)),
                 pl.BlockSpec((1, d), lambda i: (0, 0))]
    out_specs = [pl.BlockSpec((tm, d), lambda i: (i, 0))]
    out_shape = [jax.ShapeDtypeStruct((m, d), F32)]
    if emit_next:
        out_specs.append(pl.BlockSpec((tm, d), lambda i: (i, 0)))
        out_shape.append(jax.ShapeDtypeStruct((m, d), BF16))
    res = pl.pallas_call(
        functools.partial(_proj_res_kernel, n_in=len(a_list), emit_next=emit_next),
        grid=(m // tm,),
        in_specs=in_specs,
        out_specs=out_specs,
        out_shape=out_shape,
        compiler_params=_cparams(("parallel",)),
        name="proj_res",
    )(*a_list, w, x, g_post.reshape(1, d), gn.reshape(1, d))
    return res[0], (res[1] if emit_next else None)


def _proj_res_kt_kernel(a_ref, w_ref, x_ref, gp_ref, gn_ref, *rest, emit_next):
    xo_ref = rest[0]
    kk = pl.program_id(1)

    @pl.when(kk == 0)
    def _():
        xo_ref[...] = _dot(a_ref[...], w_ref[...])

    @pl.when(kk > 0)
    def _():
        xo_ref[...] += _dot(a_ref[...], w_ref[...])

    @pl.when(kk == pl.num_programs(1) - 1)
    def _():
        xn = x_ref[...] + _rms(xo_ref[...], gp_ref[...])
        xo_ref[...] = xn
        if emit_next:
            rest[1][...] = _rms(xn, gn_ref[...]).astype(BF16)


def _proj_res_ktiled(a, w, x, g_post, g_next):
    m, d = x.shape
    k = w.shape[0]
    tm = min(m, 1024)
    tk = max(c for c in range(LANES, k + 1, LANES) if k % c == 0 and c * d * 2 <= KTILE_BYTES)
    emit_next = g_next is not None
    gn = g_next if emit_next else g_post
    out_specs = [pl.BlockSpec((tm, d), lambda i, j: (i, 0))]
    out_shape = [jax.ShapeDtypeStruct((m, d), F32)]
    if emit_next:
        out_specs.append(pl.BlockSpec((tm, d), lambda i, j: (i, 0)))
        out_shape.append(jax.ShapeDtypeStruct((m, d), BF16))
    res = pl.pallas_call(
        functools.partial(_proj_res_kt_kernel, emit_next=emit_next),
        grid=(m // tm, k // tk),
        in_specs=[pl.BlockSpec((tm, tk), lambda i, j: (i, j)),
                  pl.BlockSpec((tk, d), lambda i, j: (j, 0)),
                  pl.BlockSpec((tm, d), lambda i, j: (i, 0)),
                  pl.BlockSpec((1, d), lambda i, j: (0, 0)),
                  pl.BlockSpec((1, d), lambda i, j: (0, 0))],
        out_specs=out_specs,
        out_shape=out_shape,
        compiler_params=_cparams(("parallel", "arbitrary")),
        name="proj_res_ktiled",
    )(a, w, x, g_post.reshape(1, d), gn.reshape(1, d))
    return res[0], (res[1] if emit_next else None)


def _ffn_hidden_kernel(hn_ref, wg_ref, wu_ref, cw_ref, cb_ref, hb1_ref, hb2_ref,
                       h_ref, cs_ref, *, seq_len, n_seq):
    hn = hn_ref[...]
    g = _dot(hn, wg_ref[...])
    u = _dot(hn, wu_ref[...])
    w0 = cw_ref[0:1, :]
    w1 = cw_ref[1:2, :]
    w2 = cw_ref[2:3, :]
    cb = cb_ref[...]
    if seq_len > SUBLANES:
        gc = cb + w0 * pltpu.roll(g, 2, 0) + w1 * pltpu.roll(g, 1, 0) + w2 * g
        h_ref[...] = (_gelu(gc) * u).astype(h_ref.dtype)
    row = lax.broadcasted_iota(I32, (SUBLANES, 1), 0)
    for s in range(n_seq):
        r0 = s * seq_len
        g8 = g[r0:r0 + SUBLANES, :]
        p1 = jnp.where(row < 1, hb1_ref[s], pltpu.roll(g8, 1, 0))
        p2 = jnp.where(row < 2, hb2_ref[s], pltpu.roll(g8, 2, 0))
        gc8 = cb + w0 * p2 + w1 * p1 + w2 * g8
        h_ref[r0:r0 + SUBLANES, :] = (_gelu(gc8) * u[r0:r0 + SUBLANES, :]).astype(h_ref.dtype)
        cs_ref[s] = g[r0 + seq_len - 2:r0 + seq_len, :]


def _ffn_hidden(hn, w_gate, w_up, conv_w, conv_b, buf, seq_len):
    m, d = hn.shape
    f = w_gate.shape[1]
    bsz = m // seq_len
    n_seq = 1 if seq_len > SUBLANES else bsz
    tt = seq_len * n_seq
    tf = 512
    zeros = jnp.zeros((bsz, SUBLANES - 2, f), F32)
    hb1 = jnp.concatenate([buf[:, 1:2], jnp.zeros((bsz, 1, f), F32), zeros], axis=1)
    hb2 = jnp.concatenate([buf, zeros], axis=1)
    h, cs = pl.pallas_call(
        functools.partial(_ffn_hidden_kernel, seq_len=seq_len, n_seq=n_seq),
        grid=(m // tt, f // tf),
        in_specs=[pl.BlockSpec((tt, d), lambda i, j: (i, 0)),
                  pl.BlockSpec((d, tf), lambda i, j: (0, j)),
                  pl.BlockSpec((d, tf), lambda i, j: (0, j)),
                  pl.BlockSpec((CONV_W, tf), lambda i, j: (0, j)),
                  pl.BlockSpec((1, tf), lambda i, j: (0, j)),
                  pl.BlockSpec((n_seq, SUBLANES, tf), lambda i, j: (i, 0, j)),
                  pl.BlockSpec((n_seq, SUBLANES, tf), lambda i, j: (i, 0, j))],
        out_specs=[pl.BlockSpec((tt, tf), lambda i, j: (i, j)),
                   pl.BlockSpec((n_seq, CONV_W - 1, tf), lambda i, j: (i, 0, j))],
        out_shape=[jax.ShapeDtypeStruct((m, f), BF16),
                   jax.ShapeDtypeStruct((bsz, CONV_W - 1, f), F32)],
        compiler_params=_cparams(("parallel", "parallel")),
        name="ffn_hidden",
    )(hn, w_gate, w_up, conv_w, conv_b.reshape(1, f), hb1, hb2)
    return h, cs


def _bias_tile_kernel(tab_ref, o_ref, *, off, keys_on_rows):
    nh, r, c = o_ref.shape
    i = lax.broadcasted_iota(I32, (r, c), 0)
    j = lax.broadcasted_iota(I32, (r, c), 1)
    n = jnp.maximum(off + (j - i if keys_on_rows else i - j), 0)
    large = jnp.full((r, c), NUM_BUCKETS // 2, I32)
    for thr in _T5_THR:
        large = large + jnp.where(n >= thr, 1, 0)
    bucket = jnp.where(n < NUM_BUCKETS // 2, n, large)

    def head(h, carry):
        val = jnp.full((r, c), tab_ref[NUM_BUCKETS - 1, h], F32)
        for b in range(NUM_BUCKETS - 2, -1, -1):
            val = jnp.where(bucket == b, tab_ref[b, h], val)
        o_ref[h] = val
        return carry
    lax.fori_loop(0, nh, head, 0)


def _bias_tile(rel_bias, off, r, c, keys_on_rows=False):
    nh = rel_bias.shape[1]
    return pl.pallas_call(
        functools.partial(_bias_tile_kernel, off=off, keys_on_rows=keys_on_rows),
        in_specs=[pl.BlockSpec(memory_space=pltpu.SMEM)],
        out_specs=pl.BlockSpec(memory_space=pltpu.VMEM),
        out_shape=jax.ShapeDtypeStruct((nh, r, c), F32),
        compiler_params=pltpu.CompilerParams(vmem_limit_bytes=VMEM_LIMIT),
        name="t5_bias_tile",
    )(rel_bias)


def _logsig_cumsum_kernel(x_ref, b_ref, lf_ref, c_ref):
    t = x_ref.shape[1]
    lf = _log_sigmoid(x_ref[0] + b_ref[...])
    lf_ref[0] = lf
    row = lax.broadcasted_iota(I32, (t, 1), 0)
    c = lf
    s = 1
    while s < t:
        c = c + jnp.where(row >= s, pltpu.roll(c, s, 0), 0.0)
        s *= 2
    c_ref[0] = c


def _logsig_cumsum(x, bvec):
    bsz, t, w = x.shape
    return pl.pallas_call(
        _logsig_cumsum_kernel,
        grid=(bsz,),
        in_specs=[pl.BlockSpec((1, t, w), lambda b: (b, 0, 0)),
                  pl.BlockSpec((1, w), lambda b: (0, 0))],
        out_specs=[pl.BlockSpec((1, t, w), lambda b: (b, 0, 0)),
                   pl.BlockSpec((1, t, w), lambda b: (b, 0, 0))],
        out_shape=[jax.ShapeDtypeStruct((bsz, t, w), F32),
                   jax.ShapeDtypeStruct((bsz, t, w), F32)],
        compiler_params=_cparams(("parallel",)),
        name="logsig_cumsum",
    )(x, bvec)


def _logsig_cumsum_t_kernel(x_ref, b_ref, lf_ref, c_ref, *, row0):
    n_h, t = lf_ref.shape[1:]
    lf = _log_sigmoid(x_ref[0, row0:row0 + n_h, :] + b_ref[...])
    lf_ref[0] = lf
    lane = lax.broadcasted_iota(I32, (1, t), 1)
    c = lf
    s = 1
    while s < t:
        c = c + jnp.where(lane >= s, pltpu.roll(c, s, 1), 0.0)
        s *= 2
    c_ref[0] = c


def _logsig_cumsum_t(x_t, bcol, row0):
    bsz, w, t = x_t.shape
    n_h = bcol.shape[0]
    return pl.pallas_call(
        functools.partial(_logsig_cumsum_t_kernel, row0=row0),
        grid=(bsz,),
        in_specs=[pl.BlockSpec((1, w, t), lambda b: (b, 0, 0)),
                  pl.BlockSpec((n_h, 1), lambda b: (0, 0))],
        out_specs=[pl.BlockSpec((1, n_h, t), lambda b: (b, 0, 0)),
                   pl.BlockSpec((1, n_h, t), lambda b: (b, 0, 0))],
        out_shape=[jax.ShapeDtypeStruct((bsz, n_h, t), F32),
                   jax.ShapeDtypeStruct((bsz, n_h, t), F32)],
        compiler_params=_cparams(("parallel",)),
        name="logsig_cumsum_t",
    )(x_t, bcol)


def _page_suffix_kernel(pt_ref, *refs):
    del pt_ref
    pp = PAGES_PER_STEP
    page_refs = refs[:pp]
    o_ref, carry_ref = refs[pp:]
    s = pl.program_id(1)

    @pl.when(s == 0)
    def _():
        carry_ref[...] = jnp.zeros_like(carry_ref)

    x0 = jnp.concatenate([page_refs[p][0, 0] for p in range(pp)], axis=1)
    w = x0.shape[1]
    lane = lax.broadcasted_iota(I32, (1, w), 1) % PAGE
    x = x0
    sh = 1
    while sh < PAGE:
        x = x + jnp.where(lane + sh < PAGE, pltpu.roll(x, w - sh, 1), 0.0)
        sh *= 2
    run = carry_ref[...]
    pieces = [None] * pp
    for p in reversed(range(pp)):
        sl = slice(p * PAGE, (p + 1) * PAGE)
        pieces[p] = (x[:, sl] - x0[:, sl]) + run
        run = run + jnp.sum(x0[:, sl], axis=1, keepdims=True)
    o_ref[0, 0] = jnp.concatenate(pieces, axis=1)
    carry_ref[...] = run


def _page_suffix(lf_pool_t, layer, page_table):
    n_h = lf_pool_t.shape[2]
    bsz, n_pages = page_table.shape
    pp = PAGES_PER_STEP
    ns = n_pages // pp

    def page_map(p):
        return lambda b, s, pt: (layer, pt[b, (ns - 1 - s) * pp + p], 0, 0)

    return pl.pallas_call(
        _page_suffix_kernel,
        grid_spec=pltpu.PrefetchScalarGridSpec(
            num_scalar_prefetch=1,
            grid=(bsz, ns),
            in_specs=[pl.BlockSpec((1, 1, n_h, PAGE), page_map(p)) for p in range(pp)],
            out_specs=pl.BlockSpec((1, 1, n_h, pp * PAGE), lambda b, s, pt: (b, ns - 1 - s, 0, 0)),
            scratch_shapes=[pltpu.VMEM((n_h, 1), F32)]),
        out_shape=jax.ShapeDtypeStruct((bsz, ns, n_h, pp * PAGE), F32),
        compiler_params=_cparams(("parallel", "arbitrary")),
        name="page_suffix",
    )(page_table, *([lf_pool_t] * pp))


def _init_flash(m_ref, l_ref, acc_ref):
    m_ref[...] = jnp.full(m_ref.shape, M_INIT, F32)
    l_ref[...] = jnp.zeros_like(l_ref)
    acc_ref[...] = jnp.zeros_like(acc_ref)


def _sortable(x):
    bits = lax.bitcast_convert_type(x + 0.0, I32)
    return jnp.where(bits < 0, bits ^ 0x7FFFFFFF, bits)


def _lane_fold(x):
    n = x.shape[1] // LANES
    acc = x[:, 0:LANES]
    for i in range(1, n):
        acc = acc + x[:, i * LANES:(i + 1) * LANES]
    return acc


def _topk_select(count_fn, shape, k, n_idx_bits):
    kf = float(k)

    def bit_body(i, ans):
        cand = ans + lax.shift_left(jnp.int32(1), 31 - i)
        cnt = count_fn(lambda key, idx: key >= cand)
        return jnp.where(cnt >= kf, cand, ans)

    thr = lax.fori_loop(0, 32, bit_body, jnp.full(shape, INT_MIN, I32))
    n_ge = count_fn(lambda key, idx: key >= thr)
    n_gt = count_fn(lambda key, idx: key > thr)
    need = kf - n_gt

    def tie_search():
        def idx_body(i, c):
            cand = c + lax.shift_left(jnp.int32(1), n_idx_bits - 1 - i)
            cnt = count_fn(lambda key, idx: (key == thr) & (idx < cand))
            return jnp.where(cnt < need, cand, c)
        return lax.fori_loop(0, n_idx_bits, idx_body, jnp.zeros(shape, I32))

    excess = jnp.max(n_ge - kf) > 0.0
    cut = lax.cond(excess, tie_search, lambda: jnp.full(shape, BIG_IDX, I32))
    cut = jnp.where(n_ge > kf, cut, BIG_IDX)
    return thr, cut


def _causal_blocks(step_fn, qi):
    def far_body(j, c):
        step_fn(j, "far")
        return c
    lax.fori_loop(0, qi - 1, far_body, 0)

    @pl.when(qi >= 1)
    def _():
        step_fn(jnp.maximum(qi - 1, 0), "prev")

    step_fn(qi, "diag")


def _causal_t(tk, tq):
    key = lax.broadcasted_iota(I32, (tk, tq), 0)
    qry = lax.broadcasted_iota(I32, (tk, tq), 1)
    return key <= qry


def _mask_queries(qt_ref, qm_ref):
    n_feat, tq = qt_ref.shape[1:]
    rowi = lax.broadcasted_iota(I32, (n_feat, 1), 0)
    qt = qt_ref[0] * 0.125
    for v in range(qm_ref.shape[1] // tq):
        keep = (rowi >= v * DH) & (rowi < (v + 1) * DH)
        qm_ref[:, v * tq:(v + 1) * tq] = jnp.where(keep, qt, 0.0).astype(BF16)


def _flash_update_t(st, v_t, m_ref, l_ref, acc_ref, idx):
    m_old = m_ref[idx]
    m_new = jnp.maximum(m_old, jnp.max(st, axis=0, keepdims=True))
    alpha = jnp.exp(m_old - m_new)
    p = jnp.exp(st - m_new)
    l_ref[idx] = alpha * l_ref[idx] + jnp.sum(p, axis=0, keepdims=True)
    acc_ref[idx] = alpha * acc_ref[idx] + _dot(v_t, p.astype(BF16))
    m_ref[idx] = m_new


def _store_head_pairs(o_ref, m_ref, l_ref, acc_ref, n_h):
    del m_ref
    for p in range(n_h // 2):
        o2 = jnp.concatenate([acc_ref[2 * p] / l_ref[2 * p], acc_ref[2 * p + 1] / l_ref[2 * p + 1]], axis=0)
        o_ref[0, :, p * LANES:(p + 1) * LANES] = o2.T.astype(o_ref.dtype)


def _fox_prompt_kernel(qt_ref, kk_ref, vt_ref, c_ref, ct_ref, o_ref, qm_ref, m_ref, l_ref, acc_ref,
                       *, n_h, col0):
    tq = qt_ref.shape[2]
    tk = tq
    qi = pl.program_id(1)
    _init_flash(m_ref, l_ref, acc_ref)
    _mask_queries(qt_ref, qm_ref)
    ct = ct_ref[0]
    causal = _causal_t(tk, tq)

    def step(j, kind):
        rows = pl.ds(pl.multiple_of(j * tk, tk), tk)
        kblk = kk_ref[0, rows, :]
        cblk = c_ref[0, rows, :]
        sts = _dot(kblk, qm_ref[...])
        for h in range(n_h):
            st = sts[:, h * tq:(h + 1) * tq] + ct[h:h + 1, :] - cblk[:, col0 + h:col0 + h + 1]
            if kind == "diag":
                st = jnp.where(causal, st, NEG)
            _flash_update_t(st, vt_ref[0, j, h * DH:(h + 1) * DH, :], m_ref, l_ref, acc_ref, h)

    _causal_blocks(step, qi)
    _store_head_pairs(o_ref, m_ref, l_ref, acc_ref, n_h)


def _fox_prompt(q_t, q_blk, kk, k_blk, vt_b, c_all, c_t, n_h, col0):
    bsz, t, _ = kk.shape
    tq = min(t, ATT_BLK)
    nb = t // tq
    hw = n_h * DH
    return pl.pallas_call(
        functools.partial(_fox_prompt_kernel, n_h=n_h, col0=col0),
        grid=(bsz, nb),
        in_specs=[pl.BlockSpec((1, hw, tq), lambda b, i: (b, q_blk, i)),
                  pl.BlockSpec((1, t, hw), lambda b, i: (b, 0, k_blk)),
                  pl.BlockSpec((1, nb, hw, tq), lambda b, i: (b, 0, 1, 0)),
                  pl.BlockSpec((1, t, LANES), lambda b, i: (b, 0, 0)),
                  pl.BlockSpec((1, n_h, tq), lambda b, i: (b, 0, i))],
        out_specs=pl.BlockSpec((1, tq, hw), lambda b, i: (b, i, 0)),
        out_shape=jax.ShapeDtypeStruct((bsz, t, hw), BF16),
        scratch_shapes=[pltpu.VMEM((hw, n_h * tq), BF16),
                        pltpu.VMEM((n_h, 1, tq), F32), pltpu.VMEM((n_h, 1, tq), F32),
                        pltpu.VMEM((n_h, DH, tq), F32)],
        compiler_params=_cparams(("parallel", "parallel")),
        name="fox_prompt",
    )(q_t, kk, vt_b, c_all, c_t)


def _diff_finish(a0, l0, a1, l1, lam, gsub, scale):
    o = a0 / l0 - lam * (a1 / l1)
    return _rms(o, gsub) * scale


def _diff_prompt_kernel(far_ref, lam_ref, qt_ref, kk_ref, vt_ref, bias_ref, gs_ref, o_ref,
                        qm_ref, m_ref, l_ref, acc_ref, *, n_h, out_scale):
    tq = qt_ref.shape[2]
    tk = tq
    qi = pl.program_id(1)
    _init_flash(m_ref, l_ref, acc_ref)
    _mask_queries(qt_ref, qm_ref)
    causal = _causal_t(tk, tq)
    dv = 2 * DH

    def step(j, kind):
        kblk = kk_ref[0, pl.ds(pl.multiple_of(j * tk, tk), tk), :]
        sts = _dot(kblk, qm_ref[...])
        for h in range(n_h):
            v_t = vt_ref[0, j, h * dv:(h + 1) * dv, :]
            if kind == "far":
                bias = far_ref[h]
            else:
                bias = bias_ref[h, 0 if kind == "prev" else 1]
            for jj in range(2):
                vh = 2 * h + jj
                st = sts[:, vh * tq:(vh + 1) * tq] + bias
                if kind == "diag":
                    st = jnp.where(causal, st, NEG)
                _flash_update_t(st, v_t, m_ref, l_ref, acc_ref, vh)

    _causal_blocks(step, qi)
    lam = lam_ref[0]
    for h in range(n_h):
        o_t = acc_ref[2 * h] / l_ref[2 * h] - lam * (acc_ref[2 * h + 1] / l_ref[2 * h + 1])
        o = _rms(o_t.T, gs_ref[...]) * out_scale
        o_ref[0, :, h * dv:(h + 1) * dv] = o.astype(o_ref.dtype)


def _diff_prompt(q_t, q_blk, kk, k_blk, vt_b, bias_t, far, lam, gsub, n_h, out_scale):
    bsz, t, _ = kk.shape
    tq = min(t, ATT_BLK)
    nb = t // tq
    hw = n_h * 2 * DH
    smem = pl.BlockSpec(memory_space=pltpu.SMEM)
    return pl.pallas_call(
        functools.partial(_diff_prompt_kernel, n_h=n_h, out_scale=out_scale),
        grid=(bsz, nb),
        in_specs=[smem, smem,
                  pl.BlockSpec((1, hw, tq), lambda b, i: (b, q_blk, i)),
                  pl.BlockSpec((1, t, hw), lambda b, i: (b, 0, k_blk)),
                  pl.BlockSpec((1, nb, hw, tq), lambda b, i: (b, 0, 0, 0)),
                  pl.BlockSpec((n_h, 2, tq, tq), lambda b, i: (0, 0, 0, 0)),
                  pl.BlockSpec((1, 2 * DH), lambda b, i: (0, 0))],
        out_specs=pl.BlockSpec((1, tq, hw), lambda b, i: (b, i, 0)),
        out_shape=jax.ShapeDtypeStruct((bsz, t, hw), BF16),
        scratch_shapes=[pltpu.VMEM((hw, 2 * n_h * tq), BF16),
                        pltpu.VMEM((2 * n_h, 1, tq), F32), pltpu.VMEM((2 * n_h, 1, tq), F32),
                        pltpu.VMEM((2 * n_h, 2 * DH, tq), F32)],
        compiler_params=_cparams(("parallel", "parallel")),
        name="diff_prompt",
    )(far, lam, q_t, kk, vt_b, bias_t, gsub.reshape(1, 2 * DH))


def _dsa_prompt_kernel(far_ref, qt_ref, qit_ref, wt_ref, ki_ref, kk_ref, vt_ref, bias_ref, o_ref,
                       key_ref, qm_ref, m_ref, l_ref, acc_ref, *, n_h, topk, wi_row0):
    tq = qt_ref.shape[2]
    tk = tq
    nb = key_ref.shape[0]
    qi = pl.program_id(1)
    _init_flash(m_ref, l_ref, acc_ref)
    _mask_queries(qt_ref, qm_ref)
    causal = _causal_t(tk, tq)

    w_t = wt_ref[0, wi_row0:wi_row0 + H_IDX, :] * (1.0 / (math.sqrt(H_IDX) * math.sqrt(D_IDX)))

    def score_block(j, diag):
        kib = ki_ref[0, pl.ds(pl.multiple_of(j * tk, tk), tk), :][:, 0:D_IDX].astype(BF16)
        sc = jnp.zeros((tk, tq), F32)
        for h in range(H_IDX):
            d = _dot(kib, qit_ref[0, h * D_IDX:(h + 1) * D_IDX, :])
            sc = sc + jnp.maximum(d, 0.0) * w_t[h:h + 1, :]
        if diag:
            sc = jnp.where(causal, sc, -jnp.inf)
        key_ref[j] = _sortable(sc)

    def score_body(j, c):
        score_block(j, False)
        return c
    lax.fori_loop(0, qi, score_body, 0)
    score_block(qi, True)

    krow = lax.broadcasted_iota(I32, (tk, tq), 0)

    def count_fn(pred):
        def body(j, part):
            hit = jnp.where(pred(key_ref[j], krow + j * tk), 1.0, 0.0)
            return part + jnp.sum(hit.reshape(tk // SUBLANES, SUBLANES, tq), axis=0)
        part = lax.fori_loop(0, qi + 1, body, jnp.zeros((SUBLANES, tq), F32))
        return jnp.sum(part, axis=0, keepdims=True)

    n_bits = max(1, int(math.ceil(math.log2(nb * tk))))
    thr, cut = _topk_select(count_fn, (1, tq), topk, n_bits)

    def step(j, kind):
        kblk = kk_ref[0, pl.ds(pl.multiple_of(j * tk, tk), tk), :]
        key = key_ref[j]
        sel = (key > thr) | ((key == thr) & (krow + j * tk <= cut))
        if kind == "diag":
            sel = sel & causal
        sts = _dot(kblk, qm_ref[...])
        for h in range(n_h):
            if kind == "far":
                bias = far_ref[h]
            else:
                bias = bias_ref[h, 0 if kind == "prev" else 1]
            st = jnp.where(sel, sts[:, h * tq:(h + 1) * tq] + bias, NEG)
            _flash_update_t(st, vt_ref[0, j, h * DH:(h + 1) * DH, :], m_ref, l_ref, acc_ref, h)

    _causal_blocks(step, qi)
    _store_head_pairs(o_ref, m_ref, l_ref, acc_ref, n_h)


def _dsa_prompt(q_t, q_blk, small_t, small, kk, k_blk, vt_b, bias_t, far, n_h, topk, wi_row0):
    bsz, t, _ = kk.shape
    tq = min(t, ATT_BLK)
    nb = t // tq
    hw = n_h * DH
    wqi = H_IDX * D_IDX
    smem = pl.BlockSpec(memory_space=pltpu.SMEM)
    return pl.pallas_call(
        functools.partial(_dsa_prompt_kernel, n_h=n_h, topk=topk, wi_row0=wi_row0),
        grid=(bsz, nb),
        in_specs=[smem,
                  pl.BlockSpec((1, hw, tq), lambda b, i: (b, q_blk, i)),
                  pl.BlockSpec((1, wqi, tq), lambda b, i: (b, 0, i)),
                  pl.BlockSpec((1, LANES, tq), lambda b, i: (b, 0, i)),
                  pl.BlockSpec((1, t, LANES), lambda b, i: (b, 0, 0)),
                  pl.BlockSpec((1, t, hw), lambda b, i: (b, 0, k_blk)),
                  pl.BlockSpec((1, nb, hw, tq), lambda b, i: (b, 0, 1, 0)),
                  pl.BlockSpec((n_h, 2, tq, tq), lambda b, i: (0, 0, 0, 0))],
        out_specs=pl.BlockSpec((1, tq, hw), lambda b, i: (b, i, 0)),
        out_shape=jax.ShapeDtypeStruct((bsz, t, hw), BF16),
        scratch_shapes=[pltpu.VMEM((nb, tq, tq), I32),
                        pltpu.VMEM((hw, n_h * tq), BF16),
                        pltpu.VMEM((n_h, 1, tq), F32), pltpu.VMEM((n_h, 1, tq), F32),
                        pltpu.VMEM((n_h, DH, tq), F32)],
        compiler_params=_cparams(("parallel", "parallel")),
        name="dsa_prompt",
    )(far, q_t, q_t, small_t, small, kk, vt_b, bias_t)


def _dsa_index_sample_kernel(pt_ref, qi_ref, w_ref, *refs, topk, n_q):
    del pt_ref
    pp = PAGES_PER_STEP
    page_refs = refs[:pp]
    kinew_ref, o_ref = refs[pp:]
    s = pl.program_id(1)
    ns = pl.num_programs(1) - 1
    nblk = o_ref.shape[1]
    wblk = o_ref.shape[3]
    qrows = qi_ref[0]
    w = w_ref[0]

    def scores(ki_t):
        d = _dot(qrows, ki_t.astype(BF16))
        r = jnp.maximum(d, 0.0) * w
        sc = r[0:n_q, :]
        for h in range(1, H_IDX):
            sc = sc + r[h * n_q:(h + 1) * n_q, :]
        return sc

    @pl.when(s < ns)
    def _():
        o_ref[0, s] = jnp.concatenate([scores(page_refs[p][0, 0]) for p in range(pp)], axis=1)

    @pl.when(s == ns)
    def _():
        row = lax.broadcasted_iota(I32, (n_q, PAGE), 0)
        colp = lax.broadcasted_iota(I32, (n_q, PAGE), 1)
        sc = jnp.where(colp <= row, scores(kinew_ref[0]), -jnp.inf)
        pad = jnp.full((n_q, wblk - PAGE), -jnp.inf, F32)
        o_ref[0, ns] = jnp.concatenate([sc, pad], axis=1)

        col = lax.broadcasted_iota(I32, (n_q, wblk), 1)

        def count_fn(pred):
            def body(j, part):
                hit = pred(_sortable(o_ref[0, j]), col + j * wblk)
                return part + _lane_fold(jnp.where(hit, 1.0, 0.0))
            part = lax.fori_loop(0, nblk, body, jnp.zeros((n_q, LANES), F32))
            return jnp.sum(part, axis=1, keepdims=True)

        n_bits = max(1, int(math.ceil(math.log2(nblk * wblk))))
        thr, cut = _topk_select(count_fn, (n_q, 1), topk, n_bits)

        def mask_body(j, c):
            key = _sortable(o_ref[0, j])
            sel = (key > thr) | ((key == thr) & (col + j * wblk <= cut))
            valid = o_ref[0, j] > -jnp.inf
            o_ref[0, j] = jnp.where(sel & valid, 0.0, NEG)
            return c
        lax.fori_loop(0, nblk, mask_body, 0)


def _dsa_index_sample(qi_rows, w_rows, ki_pool, layer, page_table, ki_new, topk, n_q):
    bsz, n_pages = page_table.shape
    pp = PAGES_PER_STEP
    ns = n_pages // pp
    wblk = pp * PAGE

    def page_map(p):
        return lambda b, s, pt: (layer, pt[b, jnp.minimum(s, ns - 1) * pp + p], 0, 0)

    rows = qi_rows.shape[1]
    return pl.pallas_call(
        functools.partial(_dsa_index_sample_kernel, topk=topk, n_q=n_q),
        grid_spec=pltpu.PrefetchScalarGridSpec(
            num_scalar_prefetch=1,
            grid=(bsz, ns + 1),
            in_specs=[pl.BlockSpec((1, rows, D_IDX), lambda b, s, pt: (b, 0, 0)),
                      pl.BlockSpec((1, rows, 1), lambda b, s, pt: (b, 0, 0))]
                     + [pl.BlockSpec((1, 1, D_IDX, PAGE), page_map(p)) for p in range(pp)]
                     + [pl.BlockSpec((1, D_IDX, PAGE), lambda b, s, pt: (b, 0, 0))],
            out_specs=pl.BlockSpec((1, ns + 1, n_q, wblk), lambda b, s, pt: (b, 0, 0, 0))),
        out_shape=jax.ShapeDtypeStruct((bsz, ns + 1, n_q, wblk), F32),
        compiler_params=_cparams(("parallel", "arbitrary")),
        name="dsa_index_sample",
    )(page_table, qi_rows, w_rows, *([ki_pool] * pp), ki_new)


def _paged_attn_kernel(pt_ref, *refs, mode, n_q, n_h, dv, out_scale):
    del pt_ref
    pp = PAGES_PER_STEP
    it = iter(refs)
    lam_ref = next(it) if mode == "diff" else None
    q_ref = next(it)
    page_refs = [next(it) for _ in range(pp)]
    kvnew_ref = next(it)
    add_ref = next(it) if mode in ("dsa", "fox") else None
    addnew_ref = next(it) if mode in ("dsa", "fox") else None
    rowc_ref = next(it)
    bprev_ref = next(it)
    bnew_ref = next(it)
    gs_ref = next(it) if mode == "diff" else None
    o_ref, m_ref, l_ref, acc_ref = next(it), next(it), next(it), next(it)

    s = pl.program_id(1)
    ns = pl.num_programs(1) - 1
    rows = q_ref.shape[1]
    n_vh = rows // n_q

    @pl.when(s == 0)
    def _():
        _init_flash(m_ref, l_ref, acc_ref)

    q = q_ref[0]
    rowc = rowc_ref[0]

    if mode == "diff":
        rph = rows // n_h
        stride = 2 * n_h

        def logits(get):
            return jnp.concatenate(
                [_dot_nt(q[h * rph:(h + 1) * rph], get(pl.ds(h, PAGE, stride=stride)).astype(BF16))
                 for h in range(n_h)], axis=0)

        def weighted_values(p, get):
            return jnp.concatenate(
                [_dot(p[h * rph:(h + 1) * rph], get(pl.ds(n_h + h, PAGE, stride=stride)).astype(BF16))
                 for h in range(n_h)], axis=0)
    else:
        kw = q.shape[1]

        def logits(get):
            return _dot(q, get(slice(0, kw)).astype(BF16))

        def weighted_values(p, get):
            return _dot_nt(p, get(slice(kw, 2 * kw)).astype(BF16))

    def row_add(blk):
        if mode == "dsa":
            return jnp.concatenate([blk] * n_vh, axis=0)
        if mode == "fox":
            return jnp.concatenate(
                [jnp.broadcast_to(blk[h:h + 1, :], (n_q, blk.shape[1])) for h in range(n_vh)], axis=0)
        return None

    def update(x, values_fn):
        m_old = m_ref[0]
        m_new = jnp.maximum(m_old, jnp.max(x, axis=-1, keepdims=True))
        alpha = jnp.exp(m_old - m_new)
        p32 = jnp.exp(x - m_new)
        l_ref[0] = alpha * l_ref[0] + jnp.sum(p32, axis=-1, keepdims=True)
        acc_ref[0] = alpha * acc_ref[0] + values_fn(p32.astype(BF16))
        m_ref[0] = m_new

    @pl.when(s < ns)
    def _():
        gets = [(lambda idx, r=page_refs[p]: r[0, 0, idx, :]) for p in range(pp)]
        x = jnp.concatenate([logits(g) for g in gets], axis=1)
        x = x + rowc
        if add_ref is not None:
            x = x + row_add(add_ref[0, 0])
        last = jnp.where(s == ns - 1, bprev_ref[...], 0.0)
        x = jnp.concatenate([x[:, 0:(pp - 1) * PAGE], x[:, (pp - 1) * PAGE:] + last], axis=1)

        def values_fn(p):
            pv = weighted_values(p[:, 0:PAGE], gets[0])
            for i in range(1, pp):
                pv = pv + weighted_values(p[:, i * PAGE:(i + 1) * PAGE], gets[i])
            return pv
        update(x, values_fn)

    @pl.when(s == ns)
    def _():
        get = lambda idx: kvnew_ref[0, idx, :]
        x = logits(get) + rowc + bnew_ref[...]
        if add_ref is not None:
            x = x + row_add(addnew_ref[0])
        update(x, lambda p: weighted_values(p, get))
        acc = acc_ref[0]
        l = l_ref[0]
        if mode == "diff":
            lam = lam_ref[0]
            outs = []
            for h in range(n_h):
                r0 = h * rph
                r1 = r0 + n_q
                outs.append(_diff_finish(acc[r0:r0 + n_q], l[r0:r0 + n_q], acc[r1:r1 + n_q], l[r1:r1 + n_q],
                                         lam, gs_ref[...], out_scale))
        else:
            outs = [acc[h * n_q:(h + 1) * n_q, h * dv:(h + 1) * dv] / l[h * n_q:(h + 1) * n_q]
                    for h in range(n_h)]
        o_ref[0] = jnp.concatenate(outs, axis=1).astype(o_ref.dtype)


def _paged_attn(mode, q, pool, layer, page_table, kv_new, add, add_new, rowc, bprev, bnew,
                n_q, n_h, dv, lam=None, gsub=None, out_scale=1.0):
    bsz, n_pages = page_table.shape
    pp = PAGES_PER_STEP
    ns = n_pages // pp
    rows, qw = q.shape[1:]
    prow, pcol = pool.shape[2:]
    acc_w = dv if mode == "diff" else qw

    def page_map(p):
        return lambda b, s, pt: (layer, pt[b, jnp.minimum(s, ns - 1) * pp + p], 0, 0)

    in_specs = []
    args = []
    if mode == "diff":
        in_specs.append(pl.BlockSpec(memory_space=pltpu.SMEM))
        args.append(lam)
    in_specs.append(pl.BlockSpec((1, rows, qw), lambda b, s, pt: (b, 0, 0)))
    args.append(q)
    in_specs += [pl.BlockSpec((1, 1, prow, pcol), page_map(p)) for p in range(pp)]
    args += [pool] * pp
    in_specs.append(pl.BlockSpec((1, prow, pcol), lambda b, s, pt: (b, 0, 0)))
    args.append(kv_new)
    if add is not None:
        in_specs.append(pl.BlockSpec((1, 1, add.shape[2], add.shape[3]),
                                     lambda b, s, pt: (b, jnp.minimum(s, ns - 1), 0, 0)))
        args.append(add)
        in_specs.append(pl.BlockSpec((1, add_new.shape[1], PAGE), lambda b, s, pt: (b, 0, 0)))
        args.append(add_new)
    in_specs.append(pl.BlockSpec((1, rows, 1), lambda b, s, pt: (b, 0, 0)))
    args.append(rowc)
    in_specs.append(pl.BlockSpec((rows, PAGE), lambda b, s, pt: (0, 0)))
    args.append(bprev)
    in_specs.append(pl.BlockSpec((rows, PAGE), lambda b, s, pt: (0, 0)))
    args.append(bnew)
    if mode == "diff":
        in_specs.append(pl.BlockSpec((1, dv), lambda b, s, pt: (0, 0)))
        args.append(gsub.reshape(1, dv))
    return pl.pallas_call(
        functools.partial(_paged_attn_kernel, mode=mode, n_q=n_q, n_h=n_h, dv=dv, out_scale=out_scale),
        grid_spec=pltpu.PrefetchScalarGridSpec(
            num_scalar_prefetch=1,
            grid=(bsz, ns + 1),
            in_specs=in_specs,
            out_specs=pl.BlockSpec((1, n_q, n_h * dv), lambda b, s, pt: (b, 0, 0)),
            scratch_shapes=[pltpu.VMEM((1, rows, 1), F32), pltpu.VMEM((1, rows, 1), F32),
                            pltpu.VMEM((1, rows, acc_w), F32)]),
        out_shape=jax.ShapeDtypeStruct((bsz, n_q, n_h * dv), BF16),
        compiler_params=_cparams(("parallel", "arbitrary")),
        name="paged_attn_" + mode,
    )(page_table, *args)


def _s5_kernel(u_ref, h0r_ref, h0i_ref, bre_ref, bim_ref, cre_ref, cim_ref, alr_ref, ali_ref,
               acr_ref, aci_ref, d_ref, wglu_ref, o_ref, sr_ref, si_ref,
               hr_ref, hi_ref, cr_ref, ci_ref):
    t = pl.program_id(1)
    ln = u_ref.shape[1]

    @pl.when(t == 0)
    def _():
        cr_ref[...] = h0r_ref[0]
        ci_ref[...] = h0i_ref[0]

    u = u_ref[0]
    ub = u.astype(BF16)
    xr = _dot(ub, bre_ref[...])
    xi = _dot(ub, bim_ref[...])
    row = lax.broadcasted_iota(I32, (ln, 1), 0) % SUBLANES
    for kk, sh in enumerate((1, 2, 4)):
        ar = alr_ref[kk:kk + 1, :]
        ai = ali_ref[kk:kk + 1, :]
        pr = pltpu.roll(xr, sh, 0)
        pi = pltpu.roll(xi, sh, 0)
        keep = row >= sh
        xr, xi = (xr + jnp.where(keep, ar * pr - ai * pi, 0.0),
                  xi + jnp.where(keep, ar * pi + ai * pr, 0.0))
    hr_ref[...] = xr
    hi_ref[...] = xi
    acr = acr_ref[...]
    aci = aci_ref[...]

    def group(r, carry):
        cr, ci = carry
        sl = pl.ds(pl.multiple_of(r * SUBLANES, SUBLANES), SUBLANES)
        br = hr_ref[sl, :] + acr * cr - aci * ci
        bi = hi_ref[sl, :] + acr * ci + aci * cr
        hr_ref[sl, :] = br
        hi_ref[sl, :] = bi
        return br[SUBLANES - 1:SUBLANES, :], bi[SUBLANES - 1:SUBLANES, :]

    cr, ci = lax.fori_loop(0, ln // SUBLANES, group, (cr_ref[...], ci_ref[...]))
    cr_ref[...] = cr
    ci_ref[...] = ci
    y = _dot(hr_ref[...].astype(BF16), cre_ref[...]) - _dot(hi_ref[...].astype(BF16), cim_ref[...])
    y = _gelu(y + d_ref[...] * u)
    z = _dot(y.astype(BF16), wglu_ref[...])
    o_ref[0] = (y * _sigmoid(z)).astype(o_ref.dtype)
    sr_ref[0] = cr
    si_ref[0] = ci


def _s5_mixer(u, h0r, h0i, prm):
    bsz, t, ch = u.shape
    n_state = h0r.shape[-1]
    ln = min(t, 256)
    const = lambda shape: pl.BlockSpec(shape, lambda b, i: (0, 0))
    state = pl.BlockSpec((1, 1, n_state), lambda b, i: (b, 0, 0))
    return pl.pallas_call(
        _s5_kernel,
        grid=(bsz, t // ln),
        in_specs=[pl.BlockSpec((1, ln, ch), lambda b, i: (b, i, 0)), state, state,
                  const((ch, n_state)), const((ch, n_state)),
                  const((n_state, ch)), const((n_state, ch)),
                  const((SUBLANES, n_state)), const((SUBLANES, n_state)),
                  const((SUBLANES, n_state)), const((SUBLANES, n_state)),
                  const((1, ch)), const((ch, ch))],
        out_specs=[pl.BlockSpec((1, ln, ch), lambda b, i: (b, i, 0)), state, state],
        out_shape=[jax.ShapeDtypeStruct((bsz, t, ch), BF16),
                   jax.ShapeDtypeStruct((bsz, 1, n_state), F32),
                   jax.ShapeDtypeStruct((bsz, 1, n_state), F32)],
        scratch_shapes=[pltpu.VMEM((ln, n_state), F32), pltpu.VMEM((ln, n_state), F32),
                        pltpu.VMEM((1, n_state), F32), pltpu.VMEM((1, n_state), F32)],
        compiler_params=_cparams(("parallel", "arbitrary")),
        name="s5_mixer",
    )(u, h0r, h0i, prm["bre"], prm["bim"], prm["cre"], prm["cim"],
      prm["alr"], prm["ali"], prm["acr"], prm["aci"], prm["d"], prm["wglu"])


def _s5_params(a_re, a_im, log_dt, b_re, b_im, c_re, c_im, d, w_glu):
    g, p = a_re.shape
    c = b_re.shape[-1]
    dt = jnp.exp(log_dt)[:, None]
    mag = jnp.exp(a_re * dt)
    ar = mag * jnp.cos(a_im * dt)
    ai = mag * jnp.sin(a_im * dt)
    den = a_re * a_re + a_im * a_im
    fr = ((ar - 1.0) * a_re + ai * a_im) / den
    fi = (ai * a_re - (ar - 1.0) * a_im) / den
    bbr = fr[..., None] * b_re - fi[..., None] * b_im
    bbi = fr[..., None] * b_im + fi[..., None] * b_re
    eye = jnp.eye(g, dtype=F32)

    def in_proj(x):
        return jnp.einsum("gpc,gh->gchp", x, eye).reshape(g * c, g * p).astype(BF16)

    def out_proj(x):
        return jnp.einsum("gcp,gh->gphc", x, eye).reshape(g * p, g * c).astype(BF16)

    def powers(n_list):
        rs, is_ = [], []
        for n in n_list:
            m = jnp.exp(a_re * dt * n)
            rs.append((m * jnp.cos(a_im * dt * n)).reshape(1, g * p))
            is_.append((m * jnp.sin(a_im * dt * n)).reshape(1, g * p))
        pad = SUBLANES - len(n_list)
        if pad:
            rs += [jnp.zeros((pad, g * p), F32)]
            is_ += [jnp.zeros((pad, g * p), F32)]
        return jnp.concatenate(rs, axis=0), jnp.concatenate(is_, axis=0)

    alr, ali = powers([1, 2, 4])
    acr, aci = powers(list(range(1, SUBLANES + 1)))
    return {"bre": in_proj(bbr), "bim": in_proj(bbi), "cre": out_proj(c_re), "cim": out_proj(c_im),
            "alr": alr, "ali": ali, "acr": acr, "aci": aci,
            "d": d.reshape(1, g * c), "wglu": w_glu.astype(BF16)}


def _xattn_kernel(hn_ref, wq_ref, mkv_ref, o_ref, *, n_h):
    qx = _dot(hn_ref[0], wq_ref[...])
    mkv = mkv_ref[0]
    hw = n_h * DH_X
    scale = DH_X ** -0.5
    for h in range(n_h):
        qh = qx[:, h * DH_X:(h + 1) * DH_X].astype(BF16)
        kh = mkv[:, h * DH_X:(h + 1) * DH_X].astype(BF16)
        vh = mkv[:, hw + h * DH_X:hw + (h + 1) * DH_X].astype(BF16)
        s = _dot_nt(qh, kh) * scale
        m = jnp.max(s, axis=-1, keepdims=True)
        p = jnp.exp(s - m)
        l = jnp.sum(p, axis=-1, keepdims=True)
        o = _dot(p.astype(BF16), vh) / l
        o_ref[0, :, h * DH_X:(h + 1) * DH_X] = o.astype(o_ref.dtype)


def _xattn(hn, w_qx, mem_kv, n_h):
    bsz, t, d = hn.shape
    n_mem = mem_kv.shape[1]
    hw = n_h * DH_X
    tq = min(t, 256)
    return pl.pallas_call(
        functools.partial(_xattn_kernel, n_h=n_h),
        grid=(bsz, t // tq),
        in_specs=[pl.BlockSpec((1, tq, d), lambda b, i: (b, i, 0)),
                  pl.BlockSpec((d, hw), lambda b, i: (0, 0)),
                  pl.BlockSpec((1, n_mem, 2 * hw), lambda b, i: (b, 0, 0))],
        out_specs=pl.BlockSpec((1, tq, hw), lambda b, i: (b, i, 0)),
        out_shape=jax.ShapeDtypeStruct((bsz, t, hw), BF16),
        compiler_params=_cparams(("parallel", "parallel")),
        name="xattn",
    )(hn, w_qx, mem_kv)


def _block_diag_q(q, n_vh, scale):
    bsz, n_q, _ = q.shape
    qh = (q.astype(F32) * scale).reshape(bsz, n_q, n_vh, DH)
    eye = jnp.eye(n_vh, dtype=F32)
    out = jnp.einsum("bqhd,hg->bhqgd", qh, eye)
    return out.reshape(bsz, n_vh * n_q, n_vh * DH).astype(BF16)


def _map_diag_q(q, n_h, scale):
    bsz, n_q, _ = q.shape
    qh = (q.astype(F32) * scale).reshape(bsz, n_q, n_h, 2, DH)
    eye = jnp.eye(2, dtype=F32)
    out = jnp.einsum("bqhjd,jg->bhjqgd", qh, eye)
    return out.reshape(bsz, n_h * 2 * n_q, 2 * DH).astype(BF16)


def kernel(x_prompt, x_sample, mem_prompt, cache_a_kv, cache_a_idx_k, cache_c_kv, cache_d_kv,
           cache_d_logf, cache_mem_kv, state_s5_re, state_s5_im, state_ffn_conv, page_table,
           rel_bias, g_mix_pre, w_in, s5_a_re, s5_a_im, s5_log_dt, s5_b_re, s5_b_im, s5_c_re,
           s5_c_im, s5_d, s5_w_glu, lam_q1, lam_k1, lam_q2, lam_k2, diff_sub_g, fox_b_f, w_out,
           g_mix_post, g_x_pre, w_qx, w_kvx, w_ox, g_x_post, g_ffn_pre, w_gate, w_up, conv_w,
           conv_b, w_down, g_ffn_post):
    bp, t, d = x_prompt.shape
    bs, ts, _ = x_sample.shape
    depth = w_in.shape[0]
    n_pool = cache_a_kv.shape[1]
    n_pages = page_table.shape[1]
    past = n_pages * PAGE
    h_a = cache_a_kv.shape[4]
    h_c = cache_c_kv.shape[4]
    h_d = cache_d_kv.shape[4]
    h_x = cache_mem_kv.shape[4]
    n_mem = cache_mem_kv.shape[2]
    s5_g, s5_p = s5_a_re.shape[1:]
    s5_ch = s5_g * S5_GROUP
    n_state = s5_g * s5_p
    d_ff = w_gate.shape[2]
    wa, wc, wd = h_a * DH, h_c * 2 * DH, h_d * DH
    wqi = H_IDX * D_IDX
    assert wa == wc == wd == s5_ch and ts == SUBLANES and past % (PAGES_PER_STEP * PAGE) == 0
    pp = PAGES_PER_STEP
    ns = n_pages // pp
    topk_p = min(TOPK_MAX, t // 4)
    topk_s = min(TOPK_MAX, (past + ts) // 4)

    sizes = (wa, wa, wa, wqi, D_IDX, H_IDX, s5_ch, wc, wc, wc, wd, wd, wd, h_d)
    offs = np.concatenate([[0], np.cumsum(sizes)]).tolist()
    (o_aq, o_ak, o_av, o_aqi, o_aki, o_awi, o_bu, o_cq, o_ck, o_cv, o_dq, o_dk, o_dv, o_df) = offs[:-1]
    col_wi, col_df = D_IDX, D_IDX + H_IDX

    tq = min(t, ATT_BLK)
    bias_p = jnp.stack([_bias_tile(rel_bias, tq, tq, tq, keys_on_rows=True),
                        _bias_tile(rel_bias, 0, tq, tq, keys_on_rows=True)], axis=1)
    bias_s_prev = _bias_tile(rel_bias, PAGE, ts, PAGE)
    bias_s_new = _bias_tile(rel_bias, 0, ts, PAGE)
    far = rel_bias[NUM_BUCKETS - 1]
    causal_new = jnp.where(jnp.arange(PAGE)[None, :] <= jnp.arange(ts)[:, None], 0.0, NEG).astype(F32)

    def sample_tiles(h0, n_h, rep):
        hs = [h0 + i // rep for i in range(n_h * rep)]
        farc = jnp.stack([jnp.full((ts, 1), 1.0, F32) * far[h] for h in hs]).reshape(-1, 1)
        prev = jnp.stack([bias_s_prev[h] for h in hs]).reshape(-1, PAGE) - farc
        new = jnp.stack([bias_s_new[h] + causal_new for h in hs]).reshape(-1, PAGE) - farc
        return jnp.broadcast_to(farc[None], (bs,) + farc.shape), prev, new

    rowc_a, bprev_a, bnew_a = sample_tiles(0, h_a, 1)
    rowc_c, bprev_c, bnew_c = sample_tiles(h_a, h_c, 2)
    bprev_d = jnp.zeros((h_d * ts, PAGE), F32)
    bnew_d = jnp.tile(causal_new, (h_d, 1))

    key_minor = lambda c: jnp.moveaxis(c, 2, -1)
    pool_a = key_minor(cache_a_kv).reshape(depth, n_pool, 2 * wa, PAGE)
    pool_d = key_minor(cache_d_kv).reshape(depth, n_pool, 2 * wd, PAGE)
    pool_c = cache_c_kv.reshape(depth, n_pool, PAGE * 2 * h_c, 2 * DH)
    pool_ki = key_minor(cache_a_idx_k)
    pool_lf = key_minor(cache_d_logf)

    xp = x_prompt.reshape(bp * t, d)
    xs = x_sample.reshape(bs * ts, d)
    mem_bf = mem_prompt.reshape(bp * n_mem, d).astype(BF16)
    hn_p = _norm_cast(xp, g_mix_pre[0])
    hn_s = _norm_cast(xs, g_mix_pre[0])

    outs = {k: [] for k in ("a_kv_p", "a_kv_s", "a_ki_p", "a_ki_s", "c_kv_p", "c_kv_s", "d_kv_p",
                            "d_kv_s", "lf_p", "lf_s", "mem_kv", "s5r_p", "s5r_s", "s5i_p", "s5i_s",
                            "cs_p", "cs_s")}

    for l in range(depth):
        wl = w_in[l]
        w_akv = wl[:, o_ak:o_aqi].astype(BF16)
        w_ckv = wl[:, o_ck:o_dq].astype(BF16)
        w_dkv = wl[:, o_dk:o_df].astype(BF16)
        w_q = jnp.concatenate([wl[:, o_aqi:o_aki], wl[:, o_aq:o_ak], wl[:, o_cq:o_ck],
                               wl[:, o_dq:o_dk]], axis=1).astype(BF16)
        w_bu = wl[:, o_bu:o_cq].astype(BF16)
        w_small = jnp.concatenate([wl[:, o_aki:o_bu], wl[:, o_df:],
                                   jnp.zeros((d, LANES - D_IDX - H_IDX - h_d), F32)], axis=1).astype(BF16)
        qblk_a, qblk_c, qblk_d = wqi // wa, wqi // wa + 1, wqi // wa + 2
        wt = jnp.swapaxes(wl, 0, 1)
        wt_akv = wt[o_ak:o_aqi].astype(BF16)
        wt_dkv = wt[o_dk:o_df].astype(BF16)
        wt_cv = wt[o_cv:o_dq].astype(BF16)
        wt_q = jnp.concatenate([wt[o_aqi:o_aki], wt[o_aq:o_ak], wt[o_cq:o_ck], wt[o_dq:o_dk]],
                               axis=0).astype(BF16)
        wt_small = jnp.concatenate([wt[o_aki:o_bu], wt[o_df:],
                                    jnp.zeros((LANES - D_IDX - H_IDX - h_d, d), F32)], axis=0).astype(BF16)
        w_keys = jnp.concatenate([wl[:, o_ak:o_av], wl[:, o_ck:o_cv], wl[:, o_dk:o_dv]], axis=1).astype(BF16)
        lam_init = 0.8 - 0.6 * math.exp(-0.3 * l)
        lam = (jnp.exp(jnp.sum(lam_q1[l] * lam_k1[l])) - jnp.exp(jnp.sum(lam_q2[l] * lam_k2[l]))
               + lam_init).reshape(1)
        bvec = jnp.zeros((1, LANES), F32).at[0, col_df:col_df + h_d].set(fox_b_f[l])
        s5p = _s5_params(s5_a_re[l], s5_a_im[l], s5_log_dt[l], s5_b_re[l], s5_b_im[l],
                         s5_c_re[l], s5_c_im[l], s5_d[l], s5_w_glu[l])
        w_out_b = w_out[l].astype(BF16)
        w_qx_b = w_qx[l].astype(BF16)
        w_ox_b = w_ox[l].astype(BF16)
        w_gate_b = w_gate[l].astype(BF16)
        w_up_b = w_up[l].astype(BF16)
        w_down_b = w_down[l].astype(BF16)
        g_next = g_mix_pre[l + 1] if l + 1 < depth else None

        mkv_p = _matmul(mem_bf, w_kvx[l].astype(BF16), F32).reshape(bp, n_mem, 2 * h_x * DH_X)
        outs["mem_kv"].append(mkv_p)

        def mixer_inputs(hn):
            return (_matmul(hn, w_akv, F32), _matmul(hn, w_ckv, F32), _matmul(hn, w_dkv, F32),
                    _matmul(hn, w_q, BF16), _matmul(hn, w_bu, F32), _matmul(hn, w_small, F32))

        hn3 = hn_p.reshape(bp, t, d)
        a_kv_t, a_vt_b = _matmul_nt(wt_akv, hn3, F32, True)
        d_kv_t, d_vt_b = _matmul_nt(wt_dkv, hn3, F32, True)
        (c_vt_b,) = _matmul_nt(wt_cv, hn3, None, True)
        (q_t,) = _matmul_nt(wt_q, hn3, BF16, False)
        (small_t,) = _matmul_nt(wt_small, hn3, F32, False)
        kk = _matmul(hn_p, w_keys, BF16).reshape(bp, t, -1)
        c_kv3 = _matmul(hn_p, w_ckv, F32).reshape(bp, t, 2 * wc)
        b_u = _matmul(hn_p, w_bu, F32)
        small3 = _matmul(hn_p, w_small, F32).reshape(bp, t, LANES)
        _, c3 = _logsig_cumsum(small3, bvec)
        lf_t, c_t = _logsig_cumsum_t(small_t, fox_b_f[l].reshape(h_d, 1), col_df)
        o_a = _dsa_prompt(q_t, qblk_a, small_t, small3, kk, 0, a_vt_b, bias_p[:h_a], far[:h_a],
                          h_a, topk_p, col_wi)
        o_b, s5r, s5i = _s5_mixer(b_u.reshape(bp, t, s5_ch), jnp.zeros((bp, 1, n_state), F32),
                                  jnp.zeros((bp, 1, n_state), F32), s5p)
        o_c = _diff_prompt(q_t, qblk_c, kk, 1, c_vt_b, bias_p[h_a:], far[h_a:], lam, diff_sub_g[l], h_c,
                           1.0 - lam_init)
        o_d = _fox_prompt(q_t, qblk_d, kk, 2, d_vt_b, c3, c_t, h_d, col_df)
        mix = [o.reshape(bp * t, -1) for o in (o_a, o_b, o_c, o_d)]
        xp, hn = _proj_res(mix, w_out_b, xp, g_mix_post[l], g_x_pre[l])
        ox = _xattn(hn.reshape(bp, t, d), w_qx_b, mkv_p, h_x)
        xp, hn = _proj_res([ox.reshape(bp * t, -1)], w_ox_b, xp, g_x_post[l], g_ffn_pre[l])
        hid, cs = _ffn_hidden(hn, w_gate_b, w_up_b, conv_w[l], conv_b[l],
                              jnp.zeros((bp, CONV_W - 1, d_ff), F32), t)
        xp, hn_p = _proj_res_ktiled(hid, w_down_b, xp, g_ffn_post[l], g_next)
        outs["a_kv_p"].append(a_kv_t)
        outs["a_ki_p"].append(small_t[:, 0:D_IDX, :])
        outs["c_kv_p"].append(c_kv3)
        outs["d_kv_p"].append(d_kv_t)
        outs["lf_p"].append(lf_t)
        outs["s5r_p"].append(s5r)
        outs["s5i_p"].append(s5i)
        outs["cs_p"].append(cs)

        a_kv, c_kv, d_kv, q_all, b_u, small = mixer_inputs(hn_s)
        a_kv3 = a_kv.reshape(bs, ts, 2 * wa)
        c_kv3 = c_kv.reshape(bs, ts, 2 * wc)
        d_kv3 = d_kv.reshape(bs, ts, 2 * wd)
        q3 = q_all.reshape(bs, ts, -1)
        small3 = small.reshape(bs, ts, LANES)
        lf3, c3 = _logsig_cumsum(small3, bvec)
        keys_on_lanes = lambda x: jnp.pad(jnp.swapaxes(x, 1, 2), ((0, 0), (0, 0), (0, PAGE - ts)))
        keys_on_rows = lambda x: jnp.pad(x, ((0, 0), (0, PAGE - ts), (0, 0)))

        qi_rows = jnp.swapaxes(q3[:, :, 0:wqi].reshape(bs, ts, H_IDX, D_IDX), 1, 2)
        qi_rows = qi_rows.reshape(bs, H_IDX * ts, D_IDX)
        w_rows = jnp.swapaxes(small3[:, :, col_wi:col_wi + H_IDX], 1, 2).reshape(bs, H_IDX * ts, 1)
        w_rows = w_rows * (1.0 / (math.sqrt(H_IDX) * math.sqrt(D_IDX)))
        amask = _dsa_index_sample(qi_rows, w_rows, pool_ki, l, page_table,
                                  keys_on_lanes(small3[:, :, 0:D_IDX]), topk_s, ts)
        o_a = _paged_attn("dsa", _block_diag_q(q3[:, :, wqi:wqi + wa], h_a, 0.125), pool_a, l,
                          page_table, keys_on_lanes(a_kv3), amask, amask[:, ns, :, 0:PAGE],
                          rowc_a, bprev_a, bnew_a, ts, h_a, DH)
        o_b, s5r, s5i = _s5_mixer(b_u.reshape(bs, ts, s5_ch),
                                  state_s5_re[l].reshape(bs, 1, n_state),
                                  state_s5_im[l].reshape(bs, 1, n_state), s5p)
        o_c = _paged_attn("diff", _map_diag_q(q3[:, :, wqi + wa:wqi + wa + wc], h_c, 0.125),
                          pool_c, l, page_table,
                          keys_on_rows(c_kv3).reshape(bs, PAGE * 2 * h_c, 2 * DH), None, None,
                          rowc_c, bprev_c, bnew_c,
                          ts, h_c, 2 * DH, lam=lam, gsub=diff_sub_g[l], out_scale=1.0 - lam_init)
        c_new = c3[:, :, col_df:col_df + h_d]
        nck_past = _page_suffix(pool_lf, l, page_table)
        nck_new = -keys_on_lanes(c_new)
        rowc_d = jnp.swapaxes(c_new, 1, 2).reshape(bs, h_d * ts, 1)
        o_d = _paged_attn("fox", _block_diag_q(q3[:, :, wqi + wa + wc:], h_d, 0.125), pool_d, l,
                          page_table, keys_on_lanes(d_kv3), nck_past, nck_new,
                          rowc_d, bprev_d, bnew_d, ts, h_d, DH)
        mix = [o.reshape(bs * ts, -1) for o in (o_a, o_b, o_c, o_d)]
        xs, hn = _proj_res(mix, w_out_b, xs, g_mix_post[l], g_x_pre[l])
        mkv_s = cache_mem_kv[l].reshape(bs, n_mem, 2 * h_x * DH_X)
        ox = _xattn(hn.reshape(bs, ts, d), w_qx_b, mkv_s, h_x)
        xs, hn = _proj_res([ox.reshape(bs * ts, -1)], w_ox_b, xs, g_x_post[l], g_ffn_pre[l])
        hid, cs = _ffn_hidden(hn, w_gate_b, w_up_b, conv_w[l], conv_b[l], state_ffn_conv[l], ts)
        xs, hn_s = _proj_res_ktiled(hid, w_down_b, xs, g_ffn_post[l], g_next)
        outs["a_kv_s"].append(a_kv3)
        outs["a_ki_s"].append(small3[:, :, 0:D_IDX])
        outs["c_kv_s"].append(c_kv3)
        outs["d_kv_s"].append(d_kv3)
        outs["lf_s"].append(lf3[:, :, col_df:col_df + h_d])
        outs["s5r_s"].append(s5r)
        outs["s5i_s"].append(s5i)
        outs["cs_s"].append(cs)

    st = {k: jnp.stack(v) for k, v in outs.items()}
    token_major = lambda x: jnp.moveaxis(x, -1, 2)
    return (xp.reshape(bp, t, d), xs.reshape(bs, ts, d),
            token_major(st["a_kv_p"].reshape(depth, bp, 2, h_a, DH, t)),
            st["a_kv_s"].reshape(depth, bs, ts, 2, h_a, DH),
            token_major(st["a_ki_p"]), st["a_ki_s"],
            st["c_kv_p"].reshape(depth, bp, t, 2, h_c, 2 * DH),
            st["c_kv_s"].reshape(depth, bs, ts, 2, h_c, 2 * DH),
            token_major(st["d_kv_p"].reshape(depth, bp, 2, h_d, DH, t)),
            st["d_kv_s"].reshape(depth, bs, ts, 2, h_d, DH),
            token_major(st["lf_p"]), st["lf_s"],
            st["mem_kv"].reshape(depth, bp, n_mem, 2, h_x, DH_X),
            st["s5r_p"].reshape(depth, bp, s5_g, s5_p), st["s5r_s"].reshape(depth, bs, s5_g, s5_p),
            st["s5i_p"].reshape(depth, bp, s5_g, s5_p), st["s5i_s"].reshape(depth, bs, s5_g, s5_p),
            st["cs_p"], st["cs_s"])
```

```python
import functools
import math

import numpy as np
import jax
import jax.numpy as jnp
from jax import lax
from jax.experimental import pallas as pl
from jax.experimental.pallas import tpu as pltpu

F32 = jnp.float32
BF16 = jnp.bfloat16
I32 = jnp.int32

EPS = 1e-6
NEG = -1e30
M_INIT = -3e38
INT_MIN = -2 ** 31
BIG_IDX = 2 ** 30

DH = 64
H_IDX = 16
D_IDX = 64
TOPK_MAX = 256
S5_GROUP = 16
S5_P = 64
DH_X = 128
NUM_BUCKETS = 32
MAX_DISTANCE = 128
CONV_W = 3
PAGE = 128

LANES = 128
SUBLANES = 8
VMEM_LIMIT = 52 * 1024 * 1024

ATT_BLK = 256
PAGES_PER_STEP = 16
KTILE_BYTES = 2 * 1024 * 1024


def _t5_thresholds():
    exact = NUM_BUCKETS // 2
    n = np.arange(exact, MAX_DISTANCE + 1).astype(np.float64)
    large = exact + np.floor(np.log(n / exact) / math.log(MAX_DISTANCE / exact)
                             * (NUM_BUCKETS - exact)).astype(np.int64)
    bucket = np.minimum(large, NUM_BUCKETS - 1)
    return [int(n[np.argmax(bucket >= b)]) for b in range(exact + 1, NUM_BUCKETS)]


_T5_THR = _t5_thresholds()


def _cparams(sem):
    return pltpu.CompilerParams(dimension_semantics=sem, vmem_limit_bytes=VMEM_LIMIT)


def _dot(a, b):
    return jnp.dot(a, b, preferred_element_type=F32)


def _dot_nt(a, b):
    return lax.dot_general(a, b, (((1,), (1,)), ((), ())), preferred_element_type=F32)


def _rms(x, g):
    y = x * lax.rsqrt(jnp.mean(x * x, axis=-1, keepdims=True) + EPS)
    return y * g


def _gelu(x):
    c = math.sqrt(2.0 / math.pi)
    return 0.5 * x * (1.0 + jnp.tanh(c * (x + 0.044715 * (x * x * x))))


def _sigmoid(x):
    return 1.0 / (1.0 + jnp.exp(-x))


def _log_sigmoid(x):
    return jnp.minimum(x, 0.0) - jnp.log(1.0 + jnp.exp(-jnp.abs(x)))


def _norm_cast_kernel(x_ref, g_ref, o_ref):
    o_ref[...] = _rms(x_ref[...], g_ref[...]).astype(o_ref.dtype)


def _norm_cast(x, g):
    m, d = x.shape
    tm = min(m, 512)
    return pl.pallas_call(
        _norm_cast_kernel,
        grid=(m // tm,),
        in_specs=[pl.BlockSpec((tm, d), lambda i: (i, 0)),
                  pl.BlockSpec((1, d), lambda i: (0, 0))],
        out_specs=pl.BlockSpec((tm, d), lambda i: (i, 0)),
        out_shape=jax.ShapeDtypeStruct((m, d), BF16),
        compiler_params=_cparams(("parallel",)),
        name="norm_cast",
    )(x, g.reshape(1, d))


def _mm_kernel(a_ref, w_ref, o_ref):
    o_ref[...] = _dot(a_ref[...], w_ref[...]).astype(o_ref.dtype)


def _matmul(a, w, out_dtype):
    m, k = a.shape
    n = w.shape[1]
    tm = min(m, 1024)
    tn = min(n, 512)
    return pl.pallas_call(
        _mm_kernel,
        grid=(m // tm, n // tn),
        in_specs=[pl.BlockSpec((tm, k), lambda i, j: (i, 0)),
                  pl.BlockSpec((k, tn), lambda i, j: (0, j))],
        out_specs=pl.BlockSpec((tm, tn), lambda i, j: (i, j)),
        out_shape=jax.ShapeDtypeStruct((m, n), out_dtype),
        compiler_params=_cparams(("parallel", "parallel")),
        name="matmul",
    )(a, w)


def _mm_nt_kernel(w_ref, a_ref, *o_refs, plain, blocked):
    acc = _dot_nt(w_ref[...], a_ref[0])
    k = 0
    if plain:
        o_refs[k][0] = acc.astype(o_refs[k].dtype)
        k += 1
    if blocked:
        ob = o_refs[k]
        tk = ob.shape[3]
        for i in range(ob.shape[1]):
            ob[0, i] = acc[:, i * tk:(i + 1) * tk].astype(ob.dtype)


def _matmul_nt(w_t, a3, plain_dtype, blocked):
    n, k = w_t.shape
    bsz, t, _ = a3.shape
    tm = min(t, 1024)
    tn = min(n, 512)
    tk = min(t, ATT_BLK)
    out_specs, out_shape = [], []
    if plain_dtype is not None:
        out_specs.append(pl.BlockSpec((1, tn, tm), lambda b, i, j: (b, j, i)))
        out_shape.append(jax.ShapeDtypeStruct((bsz, n, t), plain_dtype))
    if blocked:
        out_specs.append(pl.BlockSpec((1, tm // tk, tn, tk), lambda b, i, j: (b, i, j, 0)))
        out_shape.append(jax.ShapeDtypeStruct((bsz, t // tk, n, tk), BF16))
    return pl.pallas_call(
        functools.partial(_mm_nt_kernel, plain=plain_dtype is not None, blocked=blocked),
        grid=(bsz, t // tm, n // tn),
        in_specs=[pl.BlockSpec((tn, k), lambda b, i, j: (j, 0)),
                  pl.BlockSpec((1, tm, k), lambda b, i, j: (b, i, 0))],
        out_specs=out_specs,
        out_shape=out_shape,
        compiler_params=_cparams(("parallel", "parallel", "parallel")),
        name="matmul_nt",
    )(w_t, a3)


def _proj_res_kernel(*refs, n_in, emit_next):
    a_refs = refs[:n_in]
    w_ref, x_ref, gp_ref, gn_ref = refs[n_in:n_in + 4]
    outs = refs[n_in + 4:]
    off = 0
    acc = None
    for a_ref in a_refs:
        kk = a_ref.shape[1]
        part = _dot(a_ref[...], w_ref[off:off + kk, :])
        acc = part if acc is None else acc + part
        off += kk
    xn = x_ref[...] + _rms(acc, gp_ref[...])
    outs[0][...] = xn
    if emit_next:
        outs[1][...] = _rms(xn, gn_ref[...]).astype(BF16)


def _proj_res(a_list, w, x, g_post, g_next):
    m, d = x.shape
    k = w.shape[0]
    tm = min(m, 256)
    emit_next = g_next is not None
    gn = g_next if emit_next else g_post
    in_specs = [pl.BlockSpec((tm, a.shape[1]), lambda i: (i, 0)) for a in a_list]
    in_specs += [pl.BlockSpec((k, d), lambda i: (0, 0)),
                 pl.BlockSpec((tm, d), lambda i: (i, 0)),
                 pl.BlockSpec((1, d), lambda i: (0, 0)),
                 pl.BlockSpec((1, d), lambda i: (0, 0))]
    out_specs = [pl.BlockSpec((tm, d), lambda i: (i, 0))]
    out_shape = [jax.ShapeDtypeStruct((m, d), F32)]
    if emit_next:
        out_specs.append(pl.BlockSpec((tm, d), lambda i: (i, 0)))
        out_shape.append(jax.ShapeDtypeStruct((m, d), BF16))
    res = pl.pallas_call(
        functools.partial(_proj_res_kernel, n_in=len(a_list), emit_next=emit_next),
        grid=(m // tm,),
        in_specs=in_specs,
        out_specs=out_specs,
        out_shape=out_shape,
        compiler_params=_cparams(("parallel",)),
        name="proj_res",
    )(*a_list, w, x, g_post.reshape(1, d), gn.reshape(1, d))
    return res[0], (res[1] if emit_next else None)


def _proj_res_kt_kernel(a_ref, w_ref, x_ref, gp_ref, gn_ref, *rest, emit_next):
    xo_ref = rest[0]
    kk = pl.program_id(1)

    @pl.when(kk == 0)
    def _():
        xo_ref[...] = _dot(a_ref[...], w_ref[...])

    @pl.when(kk > 0)
    def _():
        xo_ref[...] += _dot(a_ref[...], w_ref[...])

    @pl.when(kk == pl.num_programs(1) - 1)
    def _():
        xn = x_ref[...] + _rms(xo_ref[...], gp_ref[...])
        xo_ref[...] = xn
        if emit_next:
            rest[1][...] = _rms(xn, gn_ref[...]).astype(BF16)


def _proj_res_ktiled(a, w, x, g_post, g_next):
    m, d = x.shape
    k = w.shape[0]
    tm = min(m, 1024)
    tk = max(c for c in range(LANES, k + 1, LANES) if k % c == 0 and c * d * 2 <= KTILE_BYTES)
    emit_next = g_next is not None
    gn = g_next if emit_next else g_post
    out_specs = [pl.BlockSpec((tm, d), lambda i, j: (i, 0))]
    out_shape = [jax.ShapeDtypeStruct((m, d), F32)]
    if emit_next:
        out_specs.append(pl.BlockSpec((tm, d), lambda i, j: (i, 0)))
        out_shape.append(jax.ShapeDtypeStruct((m, d), BF16))
    res = pl.pallas_call(
        functools.partial(_proj_res_kt_kernel, emit_next=emit_next),
        grid=(m // tm, k // tk),
        in_specs=[pl.BlockSpec((tm, tk), lambda i, j: (i, j)),
                  pl.BlockSpec((tk, d), lambda i, j: (j, 0)),
                  pl.BlockSpec((tm, d), lambda i, j: (i, 0)),
                  pl.BlockSpec((1, d), lambda i, j: (0, 0)),
                  pl.BlockSpec((1, d), lambda i, j: (0, 0))],
        out_specs=out_specs,
        out_shape=out_shape,
        compiler_params=_cparams(("parallel", "arbitrary")),
        name="proj_res_ktiled",
    )(a, w, x, g_post.reshape(1, d), gn.reshape(1, d))
    return res[0], (res[1] if emit_next else None)


def _ffn_hidden_kernel(hn_ref, wg_ref, wu_ref, cw_ref, cb_ref, hb1_ref, hb2_ref,
                       h_ref, cs_ref, *, seq_len, n_seq):
    hn = hn_ref[...]
    g = _dot(hn, wg_ref[...])
    u = _dot(hn, wu_ref[...])
    w0 = cw_ref[0:1, :]
    w1 = cw_ref[1:2, :]
    w2 = cw_ref[2:3, :]
    cb = cb_ref[...]
    if seq_len > SUBLANES:
        gc = cb + w0 * pltpu.roll(g, 2, 0) + w1 * pltpu.roll(g, 1, 0) + w2 * g
        h_ref[...] = (_gelu(gc) * u).astype(h_ref.dtype)
    row = lax.broadcasted_iota(I32, (SUBLANES, 1), 0)
    for s in range(n_seq):
        r0 = s * seq_len
        g8 = g[r0:r0 + SUBLANES, :]
        p1 = jnp.where(row < 1, hb1_ref[s], pltpu.roll(g8, 1, 0))
        p2 = jnp.where(row < 2, hb2_ref[s], pltpu.roll(g8, 2, 0))
        gc8 = cb + w0 * p2 + w1 * p1 + w2 * g8
        h_ref[r0:r0 + SUBLANES, :] = (_gelu(gc8) * u[r0:r0 + SUBLANES, :]).astype(h_ref.dtype)
        cs_ref[s] = g[r0 + seq_len - 2:r0 + seq_len, :]


def _ffn_hidden(hn, w_gate, w_up, conv_w, conv_b, buf, seq_len):
    m, d = hn.shape
    f = w_gate.shape[1]
    bsz = m // seq_len
    n_seq = 1 if seq_len > SUBLANES else bsz
    tt = seq_len * n_seq
    tf = 512
    zeros = jnp.zeros((bsz, SUBLANES - 2, f), F32)
    hb1 = jnp.concatenate([buf[:, 1:2], jnp.zeros((bsz, 1, f), F32), zeros], axis=1)
    hb2 = jnp.concatenate([buf, zeros], axis=1)
    h, cs = pl.pallas_call(
        functools.partial(_ffn_hidden_kernel, seq_len=seq_len, n_seq=n_seq),
        grid=(m // tt, f // tf),
        in_specs=[pl.BlockSpec((tt, d), lambda i, j: (i, 0)),
                  pl.BlockSpec((d, tf), lambda i, j: (0, j)),
                  pl.BlockSpec((d, tf), lambda i, j: (0, j)),
                  pl.BlockSpec((CONV_W, tf), lambda i, j: (0, j)),
                  pl.BlockSpec((1, tf), lambda i, j: (0, j)),
                  pl.BlockSpec((n_seq, SUBLANES, tf), lambda i, j: (i, 0, j)),
                  pl.BlockSpec((n_seq, SUBLANES, tf), lambda i, j: (i, 0, j))],
        out_specs=[pl.BlockSpec((tt, tf), lambda i, j: (i, j)),
                   pl.BlockSpec((n_seq, CONV_W - 1, tf), lambda i, j: (i, 0, j))],
        out_shape=[jax.ShapeDtypeStruct((m, f), BF16),
                   jax.ShapeDtypeStruct((bsz, CONV_W - 1, f), F32)],
        compiler_params=_cparams(("parallel", "parallel")),
        name="ffn_hidden",
    )(hn, w_gate, w_up, conv_w, conv_b.reshape(1, f), hb1, hb2)
    return h, cs


def _bias_tile_kernel(tab_ref, o_ref, *, off, keys_on_rows):
    nh, r, c = o_ref.shape
    i = lax.broadcasted_iota(I32, (r, c), 0)
    j = lax.broadcasted_iota(I32, (r, c), 1)
    n = jnp.maximum(off + (j - i if keys_on_rows else i - j), 0)
    large = jnp.full((r, c), NUM_BUCKETS // 2, I32)
    for thr in _T5_THR:
        large = large + jnp.where(n >= thr, 1, 0)
    bucket = jnp.where(n < NUM_BUCKETS // 2, n, large)

    def head(h, carry):
        val = jnp.full((r, c), tab_ref[NUM_BUCKETS - 1, h], F32)
        for b in range(NUM_BUCKETS - 2, -1, -1):
            val = jnp.where(bucket == b, tab_ref[b, h], val)
        o_ref[h] = val
        return carry
    lax.fori_loop(0, nh, head, 0)


def _bias_tile(rel_bias, off, r, c, keys_on_rows=False):
    nh = rel_bias.shape[1]
    return pl.pallas_call(
        functools.partial(_bias_tile_kernel, off=off, keys_on_rows=keys_on_rows),
        in_specs=[pl.BlockSpec(memory_space=pltpu.SMEM)],
        out_specs=pl.BlockSpec(memory_space=pltpu.VMEM),
        out_shape=jax.ShapeDtypeStruct((nh, r, c), F32),
        compiler_params=pltpu.CompilerParams(vmem_limit_bytes=VMEM_LIMIT),
        name="t5_bias_tile",
    )(rel_bias)


def _logsig_cumsum_kernel(x_ref, b_ref, lf_ref, c_ref):
    t = x_ref.shape[1]
    lf = _log_sigmoid(x_ref[0] + b_ref[...])
    lf_ref[0] = lf
    row = lax.broadcasted_iota(I32, (t, 1), 0)
    c = lf
    s = 1
    while s < t:
        c = c + jnp.where(row >= s, pltpu.roll(c, s, 0), 0.0)
        s *= 2
    c_ref[0] = c


def _logsig_cumsum(x, bvec):
    bsz, t, w = x.shape
    return pl.pallas_call(
        _logsig_cumsum_kernel,
        grid=(bsz,),
        in_specs=[pl.BlockSpec((1, t, w), lambda b: (b, 0, 0)),
                  pl.BlockSpec((1, w), lambda b: (0, 0))],
        out_specs=[pl.BlockSpec((1, t, w), lambda b: (b, 0, 0)),
                   pl.BlockSpec((1, t, w), lambda b: (b, 0, 0))],
        out_shape=[jax.ShapeDtypeStruct((bsz, t, w), F32),
                   jax.ShapeDtypeStruct((bsz, t, w), F32)],
        compiler_params=_cparams(("parallel",)),
        name="logsig_cumsum",
    )(x, bvec)


def _logsig_cumsum_t_kernel(x_ref, b_ref, lf_ref, c_ref, *, row0):
    n_h, t = lf_ref.shape[1:]
    lf = _log_sigmoid(x_ref[0, row0:row0 + n_h, :] + b_ref[...])
    lf_ref[0] = lf
    lane = lax.broadcasted_iota(I32, (1, t), 1)
    c = lf
    s = 1
    while s < t:
        c = c + jnp.where(lane >= s, pltpu.roll(c, s, 1), 0.0)
        s *= 2
    c_ref[0] = c


def _logsig_cumsum_t(x_t, bcol, row0):
    bsz, w, t = x_t.shape
    n_h = bcol.shape[0]
    return pl.pallas_call(
        functools.partial(_logsig_cumsum_t_kernel, row0=row0),
        grid=(bsz,),
        in_specs=[pl.BlockSpec((1, w, t), lambda b: (b, 0, 0)),
                  pl.BlockSpec((n_h, 1), lambda b: (0, 0))],
        out_specs=[pl.BlockSpec((1, n_h, t), lambda b: (b, 0, 0)),
                   pl.BlockSpec((1, n_h, t), lambda b: (b, 0, 0))],
        out_shape=[jax.ShapeDtypeStruct((bsz, n_h, t), F32),
                   jax.ShapeDtypeStruct((bsz, n_h, t), F32)],
        compiler_params=_cparams(("parallel",)),
        name="logsig_cumsum_t",
    )(x_t, bcol)


def _page_suffix_kernel(pt_ref, *refs):
    del pt_ref
    pp = PAGES_PER_STEP
    page_refs = refs[:pp]
    o_ref, carry_ref = refs[pp:]
    s = pl.program_id(1)

    @pl.when(s == 0)
    def _():
        carry_ref[...] = jnp.zeros_like(carry_ref)

    x0 = jnp.concatenate([page_refs[p][0, 0] for p in range(pp)], axis=1)
    w = x0.shape[1]
    lane = lax.broadcasted_iota(I32, (1, w), 1) % PAGE
    x = x0
    sh = 1
    while sh < PAGE:
        x = x + jnp.where(lane + sh < PAGE, pltpu.roll(x, w - sh, 1), 0.0)
        sh *= 2
    run = carry_ref[...]
    pieces = [None] * pp
    for p in reversed(range(pp)):
        sl = slice(p * PAGE, (p + 1) * PAGE)
        pieces[p] = (x[:, sl] - x0[:, sl]) + run
        run = run + jnp.sum(x0[:, sl], axis=1, keepdims=True)
    o_ref[0, 0] = jnp.concatenate(pieces, axis=1)
    carry_ref[...] = run


def _page_suffix(lf_pool_t, layer, page_table):
    n_h = lf_pool_t.shape[2]
    bsz, n_pages = page_table.shape
    pp = PAGES_PER_STEP
    ns = n_pages // pp

    def page_map(p):
        return lambda b, s, pt: (layer, pt[b, (ns - 1 - s) * pp + p], 0, 0)

    return pl.pallas_call(
        _page_suffix_kernel,
        grid_spec=pltpu.PrefetchScalarGridSpec(
            num_scalar_prefetch=1,
            grid=(bsz, ns),
            in_specs=[pl.BlockSpec((1, 1, n_h, PAGE), page_map(p)) for p in range(pp)],
            out_specs=pl.BlockSpec((1, 1, n_h, pp * PAGE), lambda b, s, pt: (b, ns - 1 - s, 0, 0)),
            scratch_shapes=[pltpu.VMEM((n_h, 1), F32)]),
        out_shape=jax.ShapeDtypeStruct((bsz, ns, n_h, pp * PAGE), F32),
        compiler_params=_cparams(("parallel", "arbitrary")),
        name="page_suffix",
    )(page_table, *([lf_pool_t] * pp))


def _init_flash(m_ref, l_ref, acc_ref):
    m_ref[...] = jnp.full(m_ref.shape, M_INIT, F32)
    l_ref[...] = jnp.zeros_like(l_ref)
    acc_ref[...] = jnp.zeros_like(acc_ref)


def _sortable(x):
    bits = lax.bitcast_convert_type(x + 0.0, I32)
    return jnp.where(bits < 0, bits ^ 0x7FFFFFFF, bits)


def _lane_fold(x):
    n = x.shape[1] // LANES
    acc = x[:, 0:LANES]
    for i in range(1, n):
        acc = acc + x[:, i * LANES:(i + 1) * LANES]
    return acc


def _topk_select(count_fn, shape, k, n_idx_bits):
    kf = float(k)

    def unsettled(state):
        i, _, n_ge = state
        return (i < 32) & (jnp.max(jnp.abs(n_ge - kf)) > 0.0)

    def bit_body(state):
        i, ans, n_ge = state
        cand = ans + lax.shift_left(jnp.int32(1), 31 - i)
        cnt = count_fn(lambda key, idx: key >= cand)
        take = cnt >= kf
        return i + 1, jnp.where(take, cand, ans), jnp.where(take, cnt, n_ge)

    start = jnp.full(shape, INT_MIN, I32)
    _, thr, n_ge = lax.while_loop(unsettled, bit_body,
                                  (jnp.int32(0), start, count_fn(lambda key, idx: key >= start)))

    def tie_search():
        need = kf - count_fn(lambda key, idx: key > thr)

        def idx_body(i, c):
            cand = c + lax.shift_left(jnp.int32(1), n_idx_bits - 1 - i)
            cnt = count_fn(lambda key, idx: (key == thr) & (idx < cand))
            return jnp.where(cnt < need, cand, c)
        return lax.fori_loop(0, n_idx_bits, idx_body, jnp.zeros(shape, I32))

    excess = jnp.max(n_ge - kf) > 0.0
    cut = lax.cond(excess, tie_search, lambda: jnp.full(shape, BIG_IDX, I32))
    cut = jnp.where(n_ge > kf, cut, BIG_IDX)
    return thr, cut


def _causal_blocks(step_fn, qi):
    def far_body(j, c):
        step_fn(j, "far")
        return c
    lax.fori_loop(0, qi - 1, far_body, 0)

    @pl.when(qi >= 1)
    def _():
        step_fn(jnp.maximum(qi - 1, 0), "prev")

    step_fn(qi, "diag")


def _causal_t(tk, tq):
    key = lax.broadcasted_iota(I32, (tk, tq), 0)
    qry = lax.broadcasted_iota(I32, (tk, tq), 1)
    return key <= qry


def _mask_queries(qt_ref, qm_ref):
    n_feat, tq = qt_ref.shape[1:]
    rowi = lax.broadcasted_iota(I32, (n_feat, 1), 0)
    qt = qt_ref[0] * 0.125
    for v in range(qm_ref.shape[1] // tq):
        keep = (rowi >= v * DH) & (rowi < (v + 1) * DH)
        qm_ref[:, v * tq:(v + 1) * tq] = jnp.where(keep, qt, 0.0).astype(BF16)


def _flash_update_t(st, v_t, m_ref, l_ref, acc_ref, idx):
    m_old = m_ref[idx]
    m_new = jnp.maximum(m_old, jnp.max(st, axis=0, keepdims=True))
    alpha = jnp.exp(m_old - m_new)
    p = jnp.exp(st - m_new)
    l_ref[idx] = alpha * l_ref[idx] + jnp.sum(p, axis=0, keepdims=True)
    acc_ref[idx] = alpha * acc_ref[idx] + _dot(v_t, p.astype(BF16))
    m_ref[idx] = m_new


def _store_head_pairs(o_ref, m_ref, l_ref, acc_ref, n_h):
    del m_ref
    for p in range(n_h // 2):
        o2 = jnp.concatenate([acc_ref[2 * p] / l_ref[2 * p], acc_ref[2 * p + 1] / l_ref[2 * p + 1]], axis=0)
        o_ref[0, :, p * LANES:(p + 1) * LANES] = o2.T.astype(o_ref.dtype)


def _fox_prompt_kernel(qt_ref, kk_ref, vt_ref, c_ref, ct_ref, o_ref, qm_ref, m_ref, l_ref, acc_ref,
                       *, n_h, col0):
    tq = qt_ref.shape[2]
    tk = tq
    qi = pl.program_id(1)
    _init_flash(m_ref, l_ref, acc_ref)
    _mask_queries(qt_ref, qm_ref)
    ct = ct_ref[0]
    causal = _causal_t(tk, tq)

    def step(j, kind):
        rows = pl.ds(pl.multiple_of(j * tk, tk), tk)
        kblk = kk_ref[0, rows, :]
        cblk = c_ref[0, rows, :]
        sts = _dot(kblk, qm_ref[...])
        for h in range(n_h):
            st = sts[:, h * tq:(h + 1) * tq] + ct[h:h + 1, :] - cblk[:, col0 + h:col0 + h + 1]
            if kind == "diag":
                st = jnp.where(causal, st, NEG)
            _flash_update_t(st, vt_ref[0, j, h * DH:(h + 1) * DH, :], m_ref, l_ref, acc_ref, h)

    _causal_blocks(step, qi)
    _store_head_pairs(o_ref, m_ref, l_ref, acc_ref, n_h)


def _fox_prompt(q_t, q_blk, kk, k_blk, vt_b, c_all, c_t, n_h, col0):
    bsz, t, _ = kk.shape
    tq = min(t, ATT_BLK)
    nb = t // tq
    hw = n_h * DH
    return pl.pallas_call(
        functools.partial(_fox_prompt_kernel, n_h=n_h, col0=col0),
        grid=(bsz, nb),
        in_specs=[pl.BlockSpec((1, hw, tq), lambda b, i: (b, q_blk, i)),
                  pl.BlockSpec((1, t, hw), lambda b, i: (b, 0, k_blk)),
                  pl.BlockSpec((1, nb, hw, tq), lambda b, i: (b, 0, 1, 0)),
                  pl.BlockSpec((1, t, LANES), lambda b, i: (b, 0, 0)),
                  pl.BlockSpec((1, n_h, tq), lambda b, i: (b, 0, i))],
        out_specs=pl.BlockSpec((1, tq, hw), lambda b, i: (b, i, 0)),
        out_shape=jax.ShapeDtypeStruct((bsz, t, hw), BF16),
        scratch_shapes=[pltpu.VMEM((hw, n_h * tq), BF16),
                        pltpu.VMEM((n_h, 1, tq), F32), pltpu.VMEM((n_h, 1, tq), F32),
                        pltpu.VMEM((n_h, DH, tq), F32)],
        compiler_params=_cparams(("parallel", "parallel")),
        name="fox_prompt",
    )(q_t, kk, vt_b, c_all, c_t)


def _diff_finish(a0, l0, a1, l1, lam, gsub, scale):
    o = a0 / l0 - lam * (a1 / l1)
    return _rms(o, gsub) * scale


def _diff_prompt_kernel(far_ref, lam_ref, qt_ref, kk_ref, vt_ref, bias_ref, gs_ref, o_ref,
                        qm_ref, m_ref, l_ref, acc_ref, *, n_h, out_scale):
    tq = qt_ref.shape[2]
    tk = tq
    qi = pl.program_id(1)
    _init_flash(m_ref, l_ref, acc_ref)
    _mask_queries(qt_ref, qm_ref)
    causal = _causal_t(tk, tq)
    dv = 2 * DH

    def step(j, kind):
        kblk = kk_ref[0, pl.ds(pl.multiple_of(j * tk, tk), tk), :]
        sts = _dot(kblk, qm_ref[...])
        for h in range(n_h):
            v_t = vt_ref[0, j, h * dv:(h + 1) * dv, :]
            if kind == "far":
                bias = far_ref[h]
            else:
                bias = bias_ref[h, 0 if kind == "prev" else 1]
            for jj in range(2):
                vh = 2 * h + jj
                st = sts[:, vh * tq:(vh + 1) * tq] + bias
                if kind == "diag":
                    st = jnp.where(causal, st, NEG)
                _flash_update_t(st, v_t, m_ref, l_ref, acc_ref, vh)

    _causal_blocks(step, qi)
    lam = lam_ref[0]
    for h in range(n_h):
        o_t = acc_ref[2 * h] / l_ref[2 * h] - lam * (acc_ref[2 * h + 1] / l_ref[2 * h + 1])
        o = _rms(o_t.T, gs_ref[...]) * out_scale
        o_ref[0, :, h * dv:(h + 1) * dv] = o.astype(o_ref.dtype)


def _diff_prompt(q_t, q_blk, kk, k_blk, vt_b, bias_t, far, lam, gsub, n_h, out_scale):
    bsz, t, _ = kk.shape
    tq = min(t, ATT_BLK)
    nb = t // tq
    hw = n_h * 2 * DH
    smem = pl.BlockSpec(memory_space=pltpu.SMEM)
    return pl.pallas_call(
        functools.partial(_diff_prompt_kernel, n_h=n_h, out_scale=out_scale),
        grid=(bsz, nb),
        in_specs=[smem, smem,
                  pl.BlockSpec((1, hw, tq), lambda b, i: (b, q_blk, i)),
                  pl.BlockSpec((1, t, hw), lambda b, i: (b, 0, k_blk)),
                  pl.BlockSpec((1, nb, hw, tq), lambda b, i: (b, 0, 0, 0)),
                  pl.BlockSpec((n_h, 2, tq, tq), lambda b, i: (0, 0, 0, 0)),
                  pl.BlockSpec((1, 2 * DH), lambda b, i: (0, 0))],
        out_specs=pl.BlockSpec((1, tq, hw), lambda b, i: (b, i, 0)),
        out_shape=jax.ShapeDtypeStruct((bsz, t, hw), BF16),
        scratch_shapes=[pltpu.VMEM((hw, 2 * n_h * tq), BF16),
                        pltpu.VMEM((2 * n_h, 1, tq), F32), pltpu.VMEM((2 * n_h, 1, tq), F32),
                        pltpu.VMEM((2 * n_h, 2 * DH, tq), F32)],
        compiler_params=_cparams(("parallel", "parallel")),
        name="diff_prompt",
    )(far, lam, q_t, kk, vt_b, bias_t, gsub.reshape(1, 2 * DH))


def _dsa_prompt_kernel(far_ref, qt_ref, qit_ref, wt_ref, ki_ref, kk_ref, vt_ref, bias_ref, o_ref,
                       key_ref, qm_ref, m_ref, l_ref, acc_ref, *, n_h, topk, wi_row0):
    tq = qt_ref.shape[2]
    tk = tq
    nb = key_ref.shape[0]
    qi = pl.program_id(1)
    _init_flash(m_ref, l_ref, acc_ref)
    _mask_queries(qt_ref, qm_ref)
    causal = _causal_t(tk, tq)

    w_t = wt_ref[0, wi_row0:wi_row0 + H_IDX, :] * (1.0 / (math.sqrt(H_IDX) * math.sqrt(D_IDX)))

    def score_block(j, diag):
        kib = ki_ref[0, pl.ds(pl.multiple_of(j * tk, tk), tk), :][:, 0:D_IDX].astype(BF16)
        sc = jnp.zeros((tk, tq), F32)
        for h in range(H_IDX):
            d = _dot(kib, qit_ref[0, h * D_IDX:(h + 1) * D_IDX, :])
            sc = sc + jnp.maximum(d, 0.0) * w_t[h:h + 1, :]
        if diag:
            sc = jnp.where(causal, sc, -jnp.inf)
        key_ref[j] = _sortable(sc)

    def score_body(j, c):
        score_block(j, False)
        return c
    lax.fori_loop(0, qi, score_body, 0)
    score_block(qi, True)

    krow = lax.broadcasted_iota(I32, (tk, tq), 0)

    def count_fn(pred):
        def body(j, part):
            hit = jnp.where(pred(key_ref[j], krow + j * tk), 1.0, 0.0)
            return part + jnp.sum(hit.reshape(tk // SUBLANES, SUBLANES, tq), axis=0)
        part = lax.fori_loop(0, qi + 1, body, jnp.zeros((SUBLANES, tq), F32))
        return jnp.sum(part, axis=0, keepdims=True)

    n_bits = max(1, int(math.ceil(math.log2(nb * tk))))
    thr, cut = _topk_select(count_fn, (1, tq), topk, n_bits)

    def step(j, kind):
        kblk = kk_ref[0, pl.ds(pl.multiple_of(j * tk, tk), tk), :]
        key = key_ref[j]
        sel = (key > thr) | ((key == thr) & (krow + j * tk <= cut))
        if kind == "diag":
            sel = sel & causal
        sts = _dot(kblk, qm_ref[...])
        for h in range(n_h):
            if kind == "far":
                bias = far_ref[h]
            else:
                bias = bias_ref[h, 0 if kind == "prev" else 1]
            st = jnp.where(sel, sts[:, h * tq:(h + 1) * tq] + bias, NEG)
            _flash_update_t(st, vt_ref[0, j, h * DH:(h + 1) * DH, :], m_ref, l_ref, acc_ref, h)

    _causal_blocks(step, qi)
    _store_head_pairs(o_ref, m_ref, l_ref, acc_ref, n_h)


def _dsa_prompt(q_t, q_blk, small_t, small, kk, k_blk, vt_b, bias_t, far, n_h, topk, wi_row0):
    bsz, t, _ = kk.shape
    tq = min(t, ATT_BLK)
    nb = t // tq
    hw = n_h * DH
    wqi = H_IDX * D_IDX
    smem = pl.BlockSpec(memory_space=pltpu.SMEM)
    return pl.pallas_call(
        functools.partial(_dsa_prompt_kernel, n_h=n_h, topk=topk, wi_row0=wi_row0),
        grid=(bsz, nb),
        in_specs=[smem,
                  pl.BlockSpec((1, hw, tq), lambda b, i: (b, q_blk, i)),
                  pl.BlockSpec((1, wqi, tq), lambda b, i: (b, 0, i)),
                  pl.BlockSpec((1, LANES, tq), lambda b, i: (b, 0, i)),
                  pl.BlockSpec((1, t, LANES), lambda b, i: (b, 0, 0)),
                  pl.BlockSpec((1, t, hw), lambda b, i: (b, 0, k_blk)),
                  pl.BlockSpec((1, nb, hw, tq), lambda b, i: (b, 0, 1, 0)),
                  pl.BlockSpec((n_h, 2, tq, tq), lambda b, i: (0, 0, 0, 0))],
        out_specs=pl.BlockSpec((1, tq, hw), lambda b, i: (b, i, 0)),
        out_shape=jax.ShapeDtypeStruct((bsz, t, hw), BF16),
        scratch_shapes=[pltpu.VMEM((nb, tq, tq), I32),
                        pltpu.VMEM((hw, n_h * tq), BF16),
                        pltpu.VMEM((n_h, 1, tq), F32), pltpu.VMEM((n_h, 1, tq), F32),
                        pltpu.VMEM((n_h, DH, tq), F32)],
        compiler_params=_cparams(("parallel", "parallel")),
        name="dsa_prompt",
    )(far, q_t, q_t, small_t, small, kk, vt_b, bias_t)


def _dsa_index_sample_kernel(pt_ref, qi_ref, w_ref, *refs, topk, n_q):
    del pt_ref
    pp = PAGES_PER_STEP
    page_refs = refs[:pp]
    kinew_ref, o_ref = refs[pp:]
    s = pl.program_id(1)
    ns = pl.num_programs(1) - 1
    nblk = o_ref.shape[1]
    wblk = o_ref.shape[3]
    qrows = qi_ref[0]
    w = w_ref[0]

    def scores(ki_t):
        d = _dot(qrows, ki_t.astype(BF16))
        r = jnp.maximum(d, 0.0) * w
        sc = r[0:n_q, :]
        for h in range(1, H_IDX):
            sc = sc + r[h * n_q:(h + 1) * n_q, :]
        return sc

    @pl.when(s < ns)
    def _():
        o_ref[0, s] = jnp.concatenate([scores(page_refs[p][0, 0]) for p in range(pp)], axis=1)

    @pl.when(s == ns)
    def _():
        row = lax.broadcasted_iota(I32, (n_q, PAGE), 0)
        colp = lax.broadcasted_iota(I32, (n_q, PAGE), 1)
        sc = jnp.where(colp <= row, scores(kinew_ref[0]), -jnp.inf)
        pad = jnp.full((n_q, wblk - PAGE), -jnp.inf, F32)
        o_ref[0, ns] = jnp.concatenate([sc, pad], axis=1)

        col = lax.broadcasted_iota(I32, (n_q, wblk), 1)

        def count_fn(pred):
            def body(j, part):
                hit = pred(_sortable(o_ref[0, j]), col + j * wblk)
                return part + _lane_fold(jnp.where(hit, 1.0, 0.0))
            part = lax.fori_loop(0, nblk, body, jnp.zeros((n_q, LANES), F32))
            return jnp.sum(part, axis=1, keepdims=True)

        n_bits = max(1, int(math.ceil(math.log2(nblk * wblk))))
        thr, cut = _topk_select(count_fn, (n_q, 1), topk, n_bits)

        def mask_body(j, c):
            key = _sortable(o_ref[0, j])
            sel = (key > thr) | ((key == thr) & (col + j * wblk <= cut))
            valid = o_ref[0, j] > -jnp.inf
            o_ref[0, j] = jnp.where(sel & valid, 0.0, NEG)
            return c
        lax.fori_loop(0, nblk, mask_body, 0)


def _dsa_index_sample(qi_rows, w_rows, ki_pool, layer, page_table, ki_new, topk, n_q):
    bsz, n_pages = page_table.shape
    pp = PAGES_PER_STEP
    ns = n_pages // pp
    wblk = pp * PAGE

    def page_map(p):
        return lambda b, s, pt: (layer, pt[b, jnp.minimum(s, ns - 1) * pp + p], 0, 0)

    rows = qi_rows.shape[1]
    return pl.pallas_call(
        functools.partial(_dsa_index_sample_kernel, topk=topk, n_q=n_q),
        grid_spec=pltpu.PrefetchScalarGridSpec(
            num_scalar_prefetch=1,
            grid=(bsz, ns + 1),
            in_specs=[pl.BlockSpec((1, rows, D_IDX), lambda b, s, pt: (b, 0, 0)),
                      pl.BlockSpec((1, rows, 1), lambda b, s, pt: (b, 0, 0))]
                     + [pl.BlockSpec((1, 1, D_IDX, PAGE), page_map(p)) for p in range(pp)]
                     + [pl.BlockSpec((1, D_IDX, PAGE), lambda b, s, pt: (b, 0, 0))],
            out_specs=pl.BlockSpec((1, ns + 1, n_q, wblk), lambda b, s, pt: (b, 0, 0, 0))),
        out_shape=jax.ShapeDtypeStruct((bsz, ns + 1, n_q, wblk), F32),
        compiler_params=_cparams(("parallel", "arbitrary")),
        name="dsa_index_sample",
    )(page_table, qi_rows, w_rows, *([ki_pool] * pp), ki_new)


def _paged_attn_kernel(pt_ref, *refs, mode, n_q, n_h, dv, out_scale):
    del pt_ref
    pp = PAGES_PER_STEP
    it = iter(refs)
    lam_ref = next(it) if mode == "diff" else None
    q_ref = next(it)
    page_refs = [next(it) for _ in range(pp)]
    kvnew_ref = next(it)
    add_ref = next(it) if mode in ("dsa", "fox") else None
    addnew_ref = next(it) if mode in ("dsa", "fox") else None
    rowc_ref = next(it)
    bprev_ref = next(it)
    bnew_ref = next(it)
    gs_ref = next(it) if mode == "diff" else None
    o_ref, m_ref, l_ref, acc_ref = next(it), next(it), next(it), next(it)

    s = pl.program_id(1)
    ns = pl.num_programs(1) - 1
    rows = q_ref.shape[1]
    n_vh = rows // n_q

    @pl.when(s == 0)
    def _():
        _init_flash(m_ref, l_ref, acc_ref)

    q = q_ref[0]
    rowc = rowc_ref[0]

    if mode == "diff":
        rph = rows // n_h
        stride = 2 * n_h

        def logits(get):
            return jnp.concatenate(
                [_dot_nt(q[h * rph:(h + 1) * rph], get(pl.ds(h, PAGE, stride=stride)).astype(BF16))
                 for h in range(n_h)], axis=0)

        def weighted_values(p, get):
            return jnp.concatenate(
                [_dot(p[h * rph:(h + 1) * rph], get(pl.ds(n_h + h, PAGE, stride=stride)).astype(BF16))
                 for h in range(n_h)], axis=0)
    else:
        kw = q.shape[1]

        def logits(get):
            return _dot(q, get(slice(0, kw)).astype(BF16))

        def weighted_values(p, get):
            return _dot_nt(p, get(slice(kw, 2 * kw)).astype(BF16))

    def row_add(blk):
        if mode == "dsa":
            return jnp.concatenate([blk] * n_vh, axis=0)
        if mode == "fox":
            return jnp.concatenate(
                [jnp.broadcast_to(blk[h:h + 1, :], (n_q, blk.shape[1])) for h in range(n_vh)], axis=0)
        return None

    def update(x, values_fn):
        m_old = m_ref[0]
        m_new = jnp.maximum(m_old, jnp.max(x, axis=-1, keepdims=True))
        alpha = jnp.exp(m_old - m_new)
        p32 = jnp.exp(x - m_new)
        l_ref[0] = alpha * l_ref[0] + jnp.sum(p32, axis=-1, keepdims=True)
        acc_ref[0] = alpha * acc_ref[0] + values_fn(p32.astype(BF16))
        m_ref[0] = m_new

    @pl.when(s < ns)
    def _():
        gets = [(lambda idx, r=page_refs[p]: r[0, 0, idx, :]) for p in range(pp)]
        x = jnp.concatenate([logits(g) for g in gets], axis=1)
        x = x + rowc
        if add_ref is not None:
            x = x + row_add(add_ref[0, 0])
        last = jnp.where(s == ns - 1, bprev_ref[...], 0.0)
        x = jnp.concatenate([x[:, 0:(pp - 1) * PAGE], x[:, (pp - 1) * PAGE:] + last], axis=1)

        def values_fn(p):
            pv = weighted_values(p[:, 0:PAGE], gets[0])
            for i in range(1, pp):
                pv = pv + weighted_values(p[:, i * PAGE:(i + 1) * PAGE], gets[i])
            return pv
        update(x, values_fn)

    @pl.when(s == ns)
    def _():
        get = lambda idx: kvnew_ref[0, idx, :]
        x = logits(get) + rowc + bnew_ref[...]
        if add_ref is not None:
            x = x + row_add(addnew_ref[0])
        update(x, lambda p: weighted_values(p, get))
        acc = acc_ref[0]
        l = l_ref[0]
        if mode == "diff":
            lam = lam_ref[0]
            outs = []
            for h in range(n_h):
                r0 = h * rph
                r1 = r0 + n_q
                outs.append(_diff_finish(acc[r0:r0 + n_q], l[r0:r0 + n_q], acc[r1:r1 + n_q], l[r1:r1 + n_q],
                                         lam, gs_ref[...], out_scale))
        else:
            outs = [acc[h * n_q:(h + 1) * n_q, h * dv:(h + 1) * dv] / l[h * n_q:(h + 1) * n_q]
                    for h in range(n_h)]
        o_ref[0] = jnp.concatenate(outs, axis=1).astype(o_ref.dtype)


def _paged_attn(mode, q, pool, layer, page_table, kv_new, add, add_new, rowc, bprev, bnew,
                n_q, n_h, dv, lam=None, gsub=None, out_scale=1.0):
    bsz, n_pages = page_table.shape
    pp = PAGES_PER_STEP
    ns = n_pages // pp
    rows, qw = q.shape[1:]
    prow, pcol = pool.shape[2:]
    acc_w = dv if mode == "diff" else qw

    def page_map(p):
        return lambda b, s, pt: (layer, pt[b, jnp.minimum(s, ns - 1) * pp + p], 0, 0)

    in_specs = []
    args = []
    if mode == "diff":
        in_specs.append(pl.BlockSpec(memory_space=pltpu.SMEM))
        args.append(lam)
    in_specs.append(pl.BlockSpec((1, rows, qw), lambda b, s, pt: (b, 0, 0)))
    args.append(q)
    in_specs += [pl.BlockSpec((1, 1, prow, pcol), page_map(p)) for p in range(pp)]
    args += [pool] * pp
    in_specs.append(pl.BlockSpec((1, prow, pcol), lambda b, s, pt: (b, 0, 0)))
    args.append(kv_new)
    if add is not None:
        in_specs.append(pl.BlockSpec((1, 1, add.shape[2], add.shape[3]),
                                     lambda b, s, pt: (b, jnp.minimum(s, ns - 1), 0, 0)))
        args.append(add)
        in_specs.append(pl.BlockSpec((1, add_new.shape[1], PAGE), lambda b, s, pt: (b, 0, 0)))
        args.append(add_new)
    in_specs.append(pl.BlockSpec((1, rows, 1), lambda b, s, pt: (b, 0, 0)))
    args.append(rowc)
    in_specs.append(pl.BlockSpec((rows, PAGE), lambda b, s, pt: (0, 0)))
    args.append(bprev)
    in_specs.append(pl.BlockSpec((rows, PAGE), lambda b, s, pt: (0, 0)))
    args.append(bnew)
    if mode == "diff":
        in_specs.append(pl.BlockSpec((1, dv), lambda b, s, pt: (0, 0)))
        args.append(gsub.reshape(1, dv))
    return pl.pallas_call(
        functools.partial(_paged_attn_kernel, mode=mode, n_q=n_q, n_h=n_h, dv=dv, out_scale=out_scale),
        grid_spec=pltpu.PrefetchScalarGridSpec(
            num_scalar_prefetch=1,
            grid=(bsz, ns + 1),
            in_specs=in_specs,
            out_specs=pl.BlockSpec((1, n_q, n_h * dv), lambda b, s, pt: (b, 0, 0)),
            scratch_shapes=[pltpu.VMEM((1, rows, 1), F32), pltpu.VMEM((1, rows, 1), F32),
                            pltpu.VMEM((1, rows, acc_w), F32)]),
        out_shape=jax.ShapeDtypeStruct((bsz, n_q, n_h * dv), BF16),
        compiler_params=_cparams(("parallel", "arbitrary")),
        name="paged_attn_" + mode,
    )(page_table, *args)


def _s5_kernel(u_ref, h0r_ref, h0i_ref, bre_ref, bim_ref, cre_ref, cim_ref, alr_ref, ali_ref,
               acr_ref, aci_ref, d_ref, wglu_ref, o_ref, sr_ref, si_ref,
               hr_ref, hi_ref, cr_ref, ci_ref):
    t = pl.program_id(1)
    ln = u_ref.shape[1]

    @pl.when(t == 0)
    def _():
        cr_ref[...] = h0r_ref[0]
        ci_ref[...] = h0i_ref[0]

    u = u_ref[0]
    ub = u.astype(BF16)
    xr = _dot(ub, bre_ref[...])
    xi = _dot(ub, bim_ref[...])
    n_state = xr.shape[1]
    grp = (ln // SUBLANES, SUBLANES, n_state)
    for kk, sh in enumerate((1, 2, 4)):
        ar = alr_ref[kk * SUBLANES:(kk + 1) * SUBLANES, :]
        ai = ali_ref[kk * SUBLANES:(kk + 1) * SUBLANES, :]
        pr = pltpu.roll(xr, sh, 0).reshape(grp)
        pi = pltpu.roll(xi, sh, 0).reshape(grp)
        xr, xi = (xr + (ar * pr - ai * pi).reshape(ln, n_state),
                  xi + (ar * pi + ai * pr).reshape(ln, n_state))
    hr_ref[...] = xr
    hi_ref[...] = xi
    acr = acr_ref[...]
    aci = aci_ref[...]

    def group(r, carry):
        cr, ci = carry
        sl = pl.ds(pl.multiple_of(r * SUBLANES, SUBLANES), SUBLANES)
        br = hr_ref[sl, :] + acr * cr - aci * ci
        bi = hi_ref[sl, :] + acr * ci + aci * cr
        hr_ref[sl, :] = br
        hi_ref[sl, :] = bi
        return br[SUBLANES - 1:SUBLANES, :], bi[SUBLANES - 1:SUBLANES, :]

    cr, ci = lax.fori_loop(0, ln // SUBLANES, group, (cr_ref[...], ci_ref[...]))
    cr_ref[...] = cr
    ci_ref[...] = ci
    y = _dot(hr_ref[...].astype(BF16), cre_ref[...]) - _dot(hi_ref[...].astype(BF16), cim_ref[...])
    y = _gelu(y + d_ref[...] * u)
    z = _dot(y.astype(BF16), wglu_ref[...])
    o_ref[0] = (y * _sigmoid(z)).astype(o_ref.dtype)
    sr_ref[0] = cr
    si_ref[0] = ci


def _s5_mixer(u, h0r, h0i, prm):
    bsz, t, ch = u.shape
    n_state = h0r.shape[-1]
    ln = min(t, 512)
    const = lambda shape: pl.BlockSpec(shape, lambda b, i: (0, 0))
    state = pl.BlockSpec((1, 1, n_state), lambda b, i: (b, 0, 0))
    return pl.pallas_call(
        _s5_kernel,
        grid=(bsz, t // ln),
        in_specs=[pl.BlockSpec((1, ln, ch), lambda b, i: (b, i, 0)), state, state,
                  const((ch, n_state)), const((ch, n_state)),
                  const((n_state, ch)), const((n_state, ch)),
                  const((3 * SUBLANES, n_state)), const((3 * SUBLANES, n_state)),
                  const((SUBLANES, n_state)), const((SUBLANES, n_state)),
                  const((1, ch)), const((ch, ch))],
        out_specs=[pl.BlockSpec((1, ln, ch), lambda b, i: (b, i, 0)), state, state],
        out_shape=[jax.ShapeDtypeStruct((bsz, t, ch), BF16),
                   jax.ShapeDtypeStruct((bsz, 1, n_state), F32),
                   jax.ShapeDtypeStruct((bsz, 1, n_state), F32)],
        scratch_shapes=[pltpu.VMEM((ln, n_state), F32), pltpu.VMEM((ln, n_state), F32),
                        pltpu.VMEM((1, n_state), F32), pltpu.VMEM((1, n_state), F32)],
        compiler_params=_cparams(("parallel", "arbitrary")),
        name="s5_mixer",
    )(u, h0r, h0i, prm["bre"], prm["bim"], prm["cre"], prm["cim"],
      prm["alr"], prm["ali"], prm["acr"], prm["aci"], prm["d"], prm["wglu"])


def _s5_params(a_re, a_im, log_dt, b_re, b_im, c_re, c_im, d, w_glu):
    g, p = a_re.shape
    c = b_re.shape[-1]
    dt = jnp.exp(log_dt)[:, None]
    mag = jnp.exp(a_re * dt)
    ar = mag * jnp.cos(a_im * dt)
    ai = mag * jnp.sin(a_im * dt)
    den = a_re * a_re + a_im * a_im
    fr = ((ar - 1.0) * a_re + ai * a_im) / den
    fi = (ai * a_re - (ar - 1.0) * a_im) / den
    bbr = fr[..., None] * b_re - fi[..., None] * b_im
    bbi = fr[..., None] * b_im + fi[..., None] * b_re
    eye = jnp.eye(g, dtype=F32)

    def in_proj(x):
        return jnp.einsum("gpc,gh->gchp", x, eye).reshape(g * c, g * p).astype(BF16)

    def out_proj(x):
        return jnp.einsum("gcp,gh->gphc", x, eye).reshape(g * p, g * c).astype(BF16)

    def powers(n_list):
        rs, is_ = [], []
        for n in n_list:
            m = jnp.exp(a_re * dt * n)
            rs.append((m * jnp.cos(a_im * dt * n)).reshape(1, g * p))
            is_.append((m * jnp.sin(a_im * dt * n)).reshape(1, g * p))
        pad = SUBLANES - len(n_list)
        if pad:
            rs += [jnp.zeros((pad, g * p), F32)]
            is_ += [jnp.zeros((pad, g * p), F32)]
        return jnp.concatenate(rs, axis=0), jnp.concatenate(is_, axis=0)

    def doubling_tables(shifts):
        rs, is_ = [], []
        for sh in shifts:
            pr, pi = powers([sh] * SUBLANES)
            keep = (jnp.arange(SUBLANES) >= sh)[:, None]
            rs.append(jnp.where(keep, pr, 0.0))
            is_.append(jnp.where(keep, pi, 0.0))
        return jnp.concatenate(rs, axis=0), jnp.concatenate(is_, axis=0)

    alr, ali = doubling_tables([1, 2, 4])
    acr, aci = powers(list(range(1, SUBLANES + 1)))
    return {"bre": in_proj(bbr), "bim": in_proj(bbi), "cre": out_proj(c_re), "cim": out_proj(c_im),
            "alr": alr, "ali": ali, "acr": acr, "aci": aci,
            "d": d.reshape(1, g * c), "wglu": w_glu.astype(BF16)}


def _xattn_kernel(hn_ref, wq_ref, mkv_ref, o_ref, *, n_h):
    qx = _dot(hn_ref[0], wq_ref[...])
    mkv = mkv_ref[0]
    hw = n_h * DH_X
    scale = DH_X ** -0.5
    for h in range(n_h):
        qh = qx[:, h * DH_X:(h + 1) * DH_X].astype(BF16)
        kh = mkv[:, h * DH_X:(h + 1) * DH_X].astype(BF16)
        vh = mkv[:, hw + h * DH_X:hw + (h + 1) * DH_X].astype(BF16)
        s = _dot_nt(qh, kh) * scale
        m = jnp.max(s, axis=-1, keepdims=True)
        p = jnp.exp(s - m)
        l = jnp.sum(p, axis=-1, keepdims=True)
        o = _dot(p.astype(BF16), vh) / l
        o_ref[0, :, h * DH_X:(h + 1) * DH_X] = o.astype(o_ref.dtype)


def _xattn(hn, w_qx, mem_kv, n_h):
    bsz, t, d = hn.shape
    n_mem = mem_kv.shape[1]
    hw = n_h * DH_X
    tq = min(t, 256)
    return pl.pallas_call(
        functools.partial(_xattn_kernel, n_h=n_h),
        grid=(bsz, t // tq),
        in_specs=[pl.BlockSpec((1, tq, d), lambda b, i: (b, i, 0)),
                  pl.BlockSpec((d, hw), lambda b, i: (0, 0)),
                  pl.BlockSpec((1, n_mem, 2 * hw), lambda b, i: (b, 0, 0))],
        out_specs=pl.BlockSpec((1, tq, hw), lambda b, i: (b, i, 0)),
        out_shape=jax.ShapeDtypeStruct((bsz, t, hw), BF16),
        compiler_params=_cparams(("parallel", "parallel")),
        name="xattn",
    )(hn, w_qx, mem_kv)


def _block_diag_q(q, n_vh, scale):
    bsz, n_q, _ = q.shape
    qh = (q.astype(F32) * scale).reshape(bsz, n_q, n_vh, DH)
    eye = jnp.eye(n_vh, dtype=F32)
    out = jnp.einsum("bqhd,hg->bhqgd", qh, eye)
    return out.reshape(bsz, n_vh * n_q, n_vh * DH).astype(BF16)


def _map_diag_q(q, n_h, scale):
    bsz, n_q, _ = q.shape
    qh = (q.astype(F32) * scale).reshape(bsz, n_q, n_h, 2, DH)
    eye = jnp.eye(2, dtype=F32)
    out = jnp.einsum("bqhjd,jg->bhjqgd", qh, eye)
    return out.reshape(bsz, n_h * 2 * n_q, 2 * DH).astype(BF16)


def kernel(x_prompt, x_sample, mem_prompt, cache_a_kv, cache_a_idx_k, cache_c_kv, cache_d_kv,
           cache_d_logf, cache_mem_kv, state_s5_re, state_s5_im, state_ffn_conv, page_table,
           rel_bias, g_mix_pre, w_in, s5_a_re, s5_a_im, s5_log_dt, s5_b_re, s5_b_im, s5_c_re,
           s5_c_im, s5_d, s5_w_glu, lam_q1, lam_k1, lam_q2, lam_k2, diff_sub_g, fox_b_f, w_out,
           g_mix_post, g_x_pre, w_qx, w_kvx, w_ox, g_x_post, g_ffn_pre, w_gate, w_up, conv_w,
           conv_b, w_down, g_ffn_post):
    bp, t, d = x_prompt.shape
    bs, ts, _ = x_sample.shape
    depth = w_in.shape[0]
    n_pool = cache_a_kv.shape[1]
    n_pages = page_table.shape[1]
    past = n_pages * PAGE
    h_a = cache_a_kv.shape[4]
    h_c = cache_c_kv.shape[4]
    h_d = cache_d_kv.shape[4]
    h_x = cache_mem_kv.shape[4]
    n_mem = cache_mem_kv.shape[2]
    s5_g, s5_p = s5_a_re.shape[1:]
    s5_ch = s5_g * S5_GROUP
    n_state = s5_g * s5_p
    d_ff = w_gate.shape[2]
    wa, wc, wd = h_a * DH, h_c * 2 * DH, h_d * DH
    wqi = H_IDX * D_IDX
    assert wa == wc == wd == s5_ch and ts == SUBLANES and past % (PAGES_PER_STEP * PAGE) == 0
    pp = PAGES_PER_STEP
    ns = n_pages // pp
    topk_p = min(TOPK_MAX, t // 4)
    topk_s = min(TOPK_MAX, (past + ts) // 4)

    sizes = (wa, wa, wa, wqi, D_IDX, H_IDX, s5_ch, wc, wc, wc, wd, wd, wd, h_d)
    offs = np.concatenate([[0], np.cumsum(sizes)]).tolist()
    (o_aq, o_ak, o_av, o_aqi, o_aki, o_awi, o_bu, o_cq, o_ck, o_cv, o_dq, o_dk, o_dv, o_df) = offs[:-1]
    col_wi, col_df = D_IDX, D_IDX + H_IDX

    tq = min(t, ATT_BLK)
    bias_p = jnp.stack([_bias_tile(rel_bias, tq, tq, tq, keys_on_rows=True),
                        _bias_tile(rel_bias, 0, tq, tq, keys_on_rows=True)], axis=1)
    bias_s_prev = _bias_tile(rel_bias, PAGE, ts, PAGE)
    bias_s_new = _bias_tile(rel_bias, 0, ts, PAGE)
    far = rel_bias[NUM_BUCKETS - 1]
    causal_new = jnp.where(jnp.arange(PAGE)[None, :] <= jnp.arange(ts)[:, None], 0.0, NEG).astype(F32)

    def sample_tiles(h0, n_h, rep):
        hs = [h0 + i // rep for i in range(n_h * rep)]
        farc = jnp.stack([jnp.full((ts, 1), 1.0, F32) * far[h] for h in hs]).reshape(-1, 1)
        prev = jnp.stack([bias_s_prev[h] for h in hs]).reshape(-1, PAGE) - farc
        new = jnp.stack([bias_s_new[h] + causal_new for h in hs]).reshape(-1, PAGE) - farc
        return jnp.broadcast_to(farc[None], (bs,) + farc.shape), prev, new

    rowc_a, bprev_a, bnew_a = sample_tiles(0, h_a, 1)
    rowc_c, bprev_c, bnew_c = sample_tiles(h_a, h_c, 2)
    bprev_d = jnp.zeros((h_d * ts, PAGE), F32)
    bnew_d = jnp.tile(causal_new, (h_d, 1))

    key_minor = lambda c: jnp.moveaxis(c, 2, -1)
    pool_a = key_minor(cache_a_kv).reshape(depth, n_pool, 2 * wa, PAGE)
    pool_d = key_minor(cache_d_kv).reshape(depth, n_pool, 2 * wd, PAGE)
    pool_c = cache_c_kv.reshape(depth, n_pool, PAGE * 2 * h_c, 2 * DH)
    pool_ki = key_minor(cache_a_idx_k)
    pool_lf = key_minor(cache_d_logf)

    xp = x_prompt.reshape(bp * t, d)
    xs = x_sample.reshape(bs * ts, d)
    mem_bf = mem_prompt.reshape(bp * n_mem, d).astype(BF16)
    hn_p = _norm_cast(xp, g_mix_pre[0])
    hn_s = _norm_cast(xs, g_mix_pre[0])

    outs = {k: [] for k in ("a_kv_p", "a_kv_s", "a_ki_p", "a_ki_s", "c_kv_p", "c_kv_s", "d_kv_p",
                            "d_kv_s", "lf_p", "lf_s", "mem_kv", "s5r_p", "s5r_s", "s5i_p", "s5i_s",
                            "cs_p", "cs_s")}

    for l in range(depth):
        wl = w_in[l]
        w_akv = wl[:, o_ak:o_aqi].astype(BF16)
        w_ckv = wl[:, o_ck:o_dq].astype(BF16)
        w_dkv = wl[:, o_dk:o_df].astype(BF16)
        w_q = jnp.concatenate([wl[:, o_aqi:o_aki], wl[:, o_aq:o_ak], wl[:, o_cq:o_ck],
                               wl[:, o_dq:o_dk]], axis=1).astype(BF16)
        w_bu = wl[:, o_bu:o_cq].astype(BF16)
        w_small = jnp.concatenate([wl[:, o_aki:o_bu], wl[:, o_df:],
                                   jnp.zeros((d, LANES - D_IDX - H_IDX - h_d), F32)], axis=1).astype(BF16)
        qblk_a, qblk_c, qblk_d = wqi // wa, wqi // wa + 1, wqi // wa + 2
        wt = jnp.swapaxes(wl, 0, 1)
        wt_akv = wt[o_ak:o_aqi].astype(BF16)
        wt_dkv = wt[o_dk:o_df].astype(BF16)
        wt_cv = wt[o_cv:o_dq].astype(BF16)
        wt_q = jnp.concatenate([wt[o_aqi:o_aki], wt[o_aq:o_ak], wt[o_cq:o_ck], wt[o_dq:o_dk]],
                               axis=0).astype(BF16)
        wt_small = jnp.concatenate([wt[o_aki:o_bu], wt[o_df:],
                                    jnp.zeros((LANES - D_IDX - H_IDX - h_d, d), F32)], axis=0).astype(BF16)
        w_keys = jnp.concatenate([wl[:, o_ak:o_av], wl[:, o_ck:o_cv], wl[:, o_dk:o_dv]], axis=1).astype(BF16)
        lam_init = 0.8 - 0.6 * math.exp(-0.3 * l)
        lam = (jnp.exp(jnp.sum(lam_q1[l] * lam_k1[l])) - jnp.exp(jnp.sum(lam_q2[l] * lam_k2[l]))
               + lam_init).reshape(1)
        bvec = jnp.zeros((1, LANES), F32).at[0, col_df:col_df + h_d].set(fox_b_f[l])
        s5p = _s5_params(s5_a_re[l], s5_a_im[l], s5_log_dt[l], s5_b_re[l], s5_b_im[l],
                         s5_c_re[l], s5_c_im[l], s5_d[l], s5_w_glu[l])
        w_out_b = w_out[l].astype(BF16)
        w_qx_b = w_qx[l].astype(BF16)
        w_ox_b = w_ox[l].astype(BF16)
        w_gate_b = w_gate[l].astype(BF16)
        w_up_b = w_up[l].astype(BF16)
        w_down_b = w_down[l].astype(BF16)
        g_next = g_mix_pre[l + 1] if l + 1 < depth else None

        mkv_p = _matmul(mem_bf, w_kvx[l].astype(BF16), F32).reshape(bp, n_mem, 2 * h_x * DH_X)
        outs["mem_kv"].append(mkv_p)

        def mixer_inputs(hn):
            return (_matmul(hn, w_akv, F32), _matmul(hn, w_ckv, F32), _matmul(hn, w_dkv, F32),
                    _matmul(hn, w_q, BF16), _matmul(hn, w_bu, F32), _matmul(hn, w_small, F32))

        hn3 = hn_p.reshape(bp, t, d)
        a_kv_t, a_vt_b = _matmul_nt(wt_akv, hn3, F32, True)
        d_kv_t, d_vt_b = _matmul_nt(wt_dkv, hn3, F32, True)
        (c_vt_b,) = _matmul_nt(wt_cv, hn3, None, True)
        (q_t,) = _matmul_nt(wt_q, hn3, BF16, False)
        (small_t,) = _matmul_nt(wt_small, hn3, F32, False)
        kk = _matmul(hn_p, w_keys, BF16).reshape(bp, t, -1)
        c_kv3 = _matmul(hn_p, w_ckv, F32).reshape(bp, t, 2 * wc)
        b_u = _matmul(hn_p, w_bu, F32)
        small3 = _matmul(hn_p, w_small, F32).reshape(bp, t, LANES)
        _, c3 = _logsig_cumsum(small3, bvec)
        lf_t, c_t = _logsig_cumsum_t(small_t, fox_b_f[l].reshape(h_d, 1), col_df)
        o_a = _dsa_prompt(q_t, qblk_a, small_t, small3, kk, 0, a_vt_b, bias_p[:h_a], far[:h_a],
                          h_a, topk_p, col_wi)
        o_b, s5r, s5i = _s5_mixer(b_u.reshape(bp, t, s5_ch), jnp.zeros((bp, 1, n_state), F32),
                                  jnp.zeros((bp, 1, n_state), F32), s5p)
        o_c = _diff_prompt(q_t, qblk_c, kk, 1, c_vt_b, bias_p[h_a:], far[h_a:], lam, diff_sub_g[l], h_c,
                           1.0 - lam_init)
        o_d = _fox_prompt(q_t, qblk_d, kk, 2, d_vt_b, c3, c_t, h_d, col_df)
        mix = [o.reshape(bp * t, -1) for o in (o_a, o_b, o_c, o_d)]
        xp, hn = _proj_res(mix, w_out_b, xp, g_mix_post[l], g_x_pre[l])
        ox = _xattn(hn.reshape(bp, t, d), w_qx_b, mkv_p, h_x)
        xp, hn = _proj_res([ox.reshape(bp * t, -1)], w_ox_b, xp, g_x_post[l], g_ffn_pre[l])
        hid, cs = _ffn_hidden(hn, w_gate_b, w_up_b, conv_w[l], conv_b[l],
                              jnp.zeros((bp, CONV_W - 1, d_ff), F32), t)
        xp, hn_p = _proj_res_ktiled(hid, w_down_b, xp, g_ffn_post[l], g_next)
        outs["a_kv_p"].append(a_kv_t)
        outs["a_ki_p"].append(small_t[:, 0:D_IDX, :])
        outs["c_kv_p"].append(c_kv3)
        outs["d_kv_p"].append(d_kv_t)
        outs["lf_p"].append(lf_t)
        outs["s5r_p"].append(s5r)
        outs["s5i_p"].append(s5i)
        outs["cs_p"].append(cs)

        a_kv, c_kv, d_kv, q_all, b_u, small = mixer_inputs(hn_s)
        a_kv3 = a_kv.reshape(bs, ts, 2 * wa)
        c_kv3 = c_kv.reshape(bs, ts, 2 * wc)
        d_kv3 = d_kv.reshape(bs, ts, 2 * wd)
        q3 = q_all.reshape(bs, ts, -1)
        small3 = small.reshape(bs, ts, LANES)
        lf3, c3 = _logsig_cumsum(small3, bvec)
        keys_on_lanes = lambda x: jnp.pad(jnp.swapaxes(x, 1, 2), ((0, 0), (0, 0), (0, PAGE - ts)))
        keys_on_rows = lambda x: jnp.pad(x, ((0, 0), (0, PAGE - ts), (0, 0)))

        qi_rows = jnp.swapaxes(q3[:, :, 0:wqi].reshape(bs, ts, H_IDX, D_IDX), 1, 2)
        qi_rows = qi_rows.reshape(bs, H_IDX * ts, D_IDX)
        w_rows = jnp.swapaxes(small3[:, :, col_wi:col_wi + H_IDX], 1, 2).reshape(bs, H_IDX * ts, 1)
        w_rows = w_rows * (1.0 / (math.sqrt(H_IDX) * math.sqrt(D_IDX)))
        amask = _dsa_index_sample(qi_rows, w_rows, pool_ki, l, page_table,
                                  keys_on_lanes(small3[:, :, 0:D_IDX]), topk_s, ts)
        o_a = _paged_attn("dsa", _block_diag_q(q3[:, :, wqi:wqi + wa], h_a, 0.125), pool_a, l,
                          page_table, keys_on_lanes(a_kv3), amask, amask[:, ns, :, 0:PAGE],
                          rowc_a, bprev_a, bnew_a, ts, h_a, DH)
        o_b, s5r, s5i = _s5_mixer(b_u.reshape(bs, ts, s5_ch),
                                  state_s5_re[l].reshape(bs, 1, n_state),
                                  state_s5_im[l].reshape(bs, 1, n_state), s5p)
        o_c = _paged_attn("diff", _map_diag_q(q3[:, :, wqi + wa:wqi + wa + wc], h_c, 0.125),
                          pool_c, l, page_table,
                          keys_on_rows(c_kv3).reshape(bs, PAGE * 2 * h_c, 2 * DH), None, None,
                          rowc_c, bprev_c, bnew_c,
                          ts, h_c, 2 * DH, lam=lam, gsub=diff_sub_g[l], out_scale=1.0 - lam_init)
        c_new = c3[:, :, col_df:col_df + h_d]
        nck_past = _page_suffix(pool_lf, l, page_table)
        nck_new = -keys_on_lanes(c_new)
        rowc_d = jnp.swapaxes(c_new, 1, 2).reshape(bs, h_d * ts, 1)
        o_d = _paged_attn("fox", _block_diag_q(q3[:, :, wqi + wa + wc:], h_d, 0.125), pool_d, l,
                          page_table, keys_on_lanes(d_kv3), nck_past, nck_new,
                          rowc_d, bprev_d, bnew_d, ts, h_d, DH)
        mix = [o.reshape(bs * ts, -1) for o in (o_a, o_b, o_c, o_d)]
        xs, hn = _proj_res(mix, w_out_b, xs, g_mix_post[l], g_x_pre[l])
        mkv_s = cache_mem_kv[l].reshape(bs, n_mem, 2 * h_x * DH_X)
        ox = _xattn(hn.reshape(bs, ts, d), w_qx_b, mkv_s, h_x)
        xs, hn = _proj_res([ox.reshape(bs * ts, -1)], w_ox_b, xs, g_x_post[l], g_ffn_pre[l])
        hid, cs = _ffn_hidden(hn, w_gate_b, w_up_b, conv_w[l], conv_b[l], state_ffn_conv[l], ts)
        xs, hn_s = _proj_res_ktiled(hid, w_down_b, xs, g_ffn_post[l], g_next)
        outs["a_kv_s"].append(a_kv3)
        outs["a_ki_s"].append(small3[:, :, 0:D_IDX])
        outs["c_kv_s"].append(c_kv3)
        outs["d_kv_s"].append(d_kv3)
        outs["lf_s"].append(lf3[:, :, col_df:col_df + h_d])
        outs["s5r_s"].append(s5r)
        outs["s5i_s"].append(s5i)
        outs["cs_s"].append(cs)

    st = {k: jnp.stack(v) for k, v in outs.items()}
    token_major = lambda x: jnp.moveaxis(x, -1, 2)
    return (xp.reshape(bp, t, d), xs.reshape(bs, ts, d),
            token_major(st["a_kv_p"].reshape(depth, bp, 2, h_a, DH, t)),
            st["a_kv_s"].reshape(depth, bs, ts, 2, h_a, DH),
            token_major(st["a_ki_p"]), st["a_ki_s"],
            st["c_kv_p"].reshape(depth, bp, t, 2, h_c, 2 * DH),
            st["c_kv_s"].reshape(depth, bs, ts, 2, h_c, 2 * DH),
            token_major(st["d_kv_p"].reshape(depth, bp, 2, h_d, DH, t)),
            st["d_kv_s"].reshape(depth, bs, ts, 2, h_d, DH),
            token_major(st["lf_p"]), st["lf_s"],
            st["mem_kv"].reshape(depth, bp, n_mem, 2, h_x, DH_X),
            st["s5r_p"].reshape(depth, bp, s5_g, s5_p), st["s5r_s"].reshape(depth, bs, s5_g, s5_p),
            st["s5i_p"].reshape(depth, bp, s5_g, s5_p), st["s5i_s"].reshape(depth, bs, s5_g, s5_p),
            st["cs_p"], st["cs_s"])
```
